```python
import jax
import jax.numpy as jnp
from jax import lax
import numpy as np

D_MODEL = 1024
BATCH = 2
SEQ = 8192
DEPTH = 4
DEC_BATCH = 128
DEC_SEQ = 1
PAST_LEN = 2048
PAGE_SIZE = 128

HEAD_DIM = 64
NSA_HEADS = D_MODEL // (2 * HEAD_DIM)
NSA_KV_HEADS = 2
NSA_HPG = NSA_HEADS // NSA_KV_HEADS
NSA_WIDTH = NSA_HEADS * HEAD_DIM
KV_W = NSA_KV_HEADS * HEAD_DIM
CMP_BLOCK = 32
CMP_STRIDE = 16
CMP_RATIO = CMP_BLOCK // CMP_STRIDE
CMP_HIDDEN = 256
SEL_BLOCK = 64
SEL_TOPK = 16
WINDOW = 512
Q_BLOCK = 128
FORCE_SCORE = 1e6
ROPE_DIM = HEAD_DIM // 4
ROPE_THETA = 500000.0
RG_WIDTH = D_MODEL // 4
RG_BLOCKS = 4
RG_BW = RG_WIDTH // RG_BLOCKS
RG_C = 8.0
CONV_W = 4
HG_HEADS = 4
HG_DK = 64
HG_DV = D_MODEL // 4 // HG_HEADS
HG_WIDTH = HG_HEADS * HG_DV
HG_CHUNK = 64
MIX_WIDTH = NSA_WIDTH + RG_WIDTH + HG_WIDTH
D_FF = 4 * D_MODEL
EPS = 1e-6
IN_SIZES = (NSA_WIDTH, 6 * KV_W, 3 * NSA_HEADS, RG_WIDTH, RG_WIDTH, HG_HEADS * HG_DK, HG_HEADS * HG_DK, HG_WIDTH, HG_WIDTH)
D_IN = sum(IN_SIZES)

kernel_name = 'nsa_rglru_hgrn2_parallel_hybrid_step'


def _split(a, sizes):
    out, o = [], 0
    for s in sizes:
        out.append(a[..., o:o + s])
        o += s
    return out


def rms_norm(x, g):
    x32 = x.astype(jnp.float32)
    y = x32 * lax.rsqrt(jnp.mean(x32 * x32, axis=-1, keepdims=True) + EPS)
    return (y * g.astype(jnp.float32)).astype(x.dtype)


def partial_rope(x, pos):
    half = ROPE_DIM // 2
    inv = ROPE_THETA ** (-jnp.arange(half, dtype=jnp.float32) * 2.0 / ROPE_DIM)
    ang = pos.astype(jnp.float32)[:, None] * inv
    cos = jnp.cos(ang)[:, None, :].astype(x.dtype)
    sin = jnp.sin(ang)[:, None, :].astype(x.dtype)
    x1, x2, rest = x[..., :half], x[..., half:ROPE_DIM], x[..., ROPE_DIM:]
    return jnp.concatenate([x1 * cos - x2 * sin, x2 * cos + x1 * sin, rest], axis=-1)


def masked_softmax(s, mask):
    s = jnp.where(mask, s, -1e30)
    m = jnp.max(s, axis=-1, keepdims=True)
    e = jnp.where(mask, jnp.exp(s - m), 0.0)
    return e / jnp.maximum(jnp.sum(e, axis=-1, keepdims=True), 1e-30)


def compress_kv(raw, pos_emb, w1, b1, w2, b2):
    B, L = raw.shape[0], raw.shape[1]
    nch = L // CMP_STRIDE
    n_cmp = nch - CMP_RATIO + 1
    chunks = raw[:, :nch * CMP_STRIDE].reshape(B, nch, CMP_STRIDE, NSA_KV_HEADS, HEAD_DIM)
    h = b1
    for r in range(CMP_RATIO):
        sl = slice(r * CMP_STRIDE, (r + 1) * CMP_STRIDE)
        pre = jnp.einsum('bnsgd,sdh->bngh', chunks + pos_emb[sl][:, None, :], w1[sl])
        h = h + pre[:, r:r + n_cmp]
    out = jax.nn.gelu(h) @ w2 + b2
    end = jnp.arange(n_cmp, dtype=jnp.int32) * CMP_STRIDE + CMP_BLOCK - 1
    return out, end


def selection_map(n_cmp, n_sel):
    c0 = jnp.arange(n_cmp) * CMP_STRIDE
    s0 = jnp.arange(n_sel) * SEL_BLOCK
    ov = (c0[:, None] < s0[None, :] + SEL_BLOCK) & (c0[:, None] + CMP_BLOCK > s0[None, :])
    return ov.astype(jnp.float32)


def nsa_attend_block(q, gate, t_pos, kw, vw, w_pos, k_cmp, v_cmp, c_end, ks_blk, vs_blk, smap):
    f32 = jnp.float32
    scale = HEAD_DIM ** -0.5
    s_c = jnp.einsum('bqghd,bngd->bqghn', q, k_cmp).astype(f32) * scale
    m_c = (c_end[None, :] <= t_pos[:, None])[None, :, None, None, :]
    p_c = masked_softmax(s_c, m_c)
    o_c = jnp.einsum('bqghn,bngd->bqghd', p_c.astype(v_cmp.dtype), v_cmp)
    imp = jnp.einsum('bqghn,nj->bqgj', p_c, smap)
    n_sel = ks_blk.shape[1]
    j = jnp.arange(n_sel)[None, :]
    cur = (t_pos // SEL_BLOCK)[:, None]
    valid = j * SEL_BLOCK <= t_pos[:, None]
    forced = (j == 0) | (j == cur) | (j == cur - 1)
    score = jnp.where(valid[None, :, None, :], jnp.where(forced[None, :, None, :], FORCE_SCORE, imp), -FORCE_SCORE)
    _, idx = lax.top_k(score, min(SEL_TOPK, n_sel))
    bi = jnp.arange(q.shape[0])[:, None, None, None]
    gi = jnp.arange(NSA_KV_HEADS)[None, None, :, None]
    kb = ks_blk[bi, idx, gi]
    vb = vs_blk[bi, idx, gi]
    kpos = idx[..., None] * SEL_BLOCK + jnp.arange(SEL_BLOCK)
    m_s = (kpos <= t_pos[None, :, None, None, None])[:, :, :, None]
    s_s = jnp.einsum('bqghd,bqgkld->bqghkl', q, kb).astype(f32) * scale
    shp = s_s.shape
    p_s = masked_softmax(s_s.reshape(*shp[:4], -1), m_s.reshape(*m_s.shape[:4], -1)).reshape(shp)
    o_s = jnp.einsum('bqghkl,bqgkld->bqghd', p_s.astype(vb.dtype), vb)
    s_w = jnp.einsum('bqghd,bwgd->bqghw', q, kw).astype(f32) * scale
    dist = t_pos[:, None] - w_pos[None, :]
    m_w = ((dist >= 0) & (dist < WINDOW) & (w_pos[None, :] >= 0))[None, :, None, None, :]
    p_w = masked_softmax(s_w, m_w)
    o_w = jnp.einsum('bqghw,bwgd->bqghd', p_w.astype(vw.dtype), vw)
    return gate[..., 0:1] * o_c + gate[..., 1:2] * o_s + gate[..., 2:3] * o_w


def nsa_mixer(q, kvs, gate, pos, cmp_pos, cmp_w1, cmp_b1, cmp_w2, cmp_b2, past):
    B, T = q.shape[:2]
    G, HD = NSA_KV_HEADS, HEAD_DIM
    q = partial_rope(q.reshape(B, T, NSA_HEADS, HD), pos).reshape(B, T, G, NSA_HPG, HD)
    kc, vc, ks, vs, kw, vw = [a.reshape(B, T, G, HD) for a in _split(kvs, (KV_W,) * 6)]
    ks = partial_rope(ks, pos)
    kw = partial_rope(kw, pos)
    gate = jax.nn.sigmoid(gate.reshape(B, T, G, NSA_HPG, 3))
    new_cmp = jnp.stack([kc, vc], axis=2)
    new_sel = jnp.stack([ks, vs], axis=2)
    new_win = jnp.stack([kw, vw], axis=2)
    if past is None:
        cmp_all, sel_all = new_cmp, new_sel
        win_all = jnp.pad(new_win, ((0, 0), (WINDOW, 0), (0, 0), (0, 0), (0, 0)))
        w_pos = pos[0] - WINDOW + jnp.arange(T + WINDOW, dtype=jnp.int32)
        win_state = new_win[:, T - min(WINDOW, T):]
    else:
        cmp_past, sel_past, win_buf = past
        cmp_all = jnp.concatenate([cmp_past, new_cmp], axis=1)
        sel_all = jnp.concatenate([sel_past, new_sel], axis=1)
        win_all = jnp.concatenate([win_buf, new_win], axis=1)
        nb = win_buf.shape[1]
        w_pos = pos[0] - nb + jnp.arange(nb + T, dtype=jnp.int32)
        win_state = win_all[:, T:]
    L = cmp_all.shape[1]
    k_cmp, c_end = compress_kv(cmp_all[:, :, 0], cmp_pos[0], cmp_w1[0], cmp_b1[0], cmp_w2[0], cmp_b2[0])
    v_cmp, _ = compress_kv(cmp_all[:, :, 1], cmp_pos[1], cmp_w1[1], cmp_b1[1], cmp_w2[1], cmp_b2[1])
    k_cmp = partial_rope(k_cmp, c_end)
    n_sel = -(-L // SEL_BLOCK)
    sel_pad = jnp.pad(sel_all, ((0, 0), (0, n_sel * SEL_BLOCK - L), (0, 0), (0, 0), (0, 0)))
    sel_blk = sel_pad.reshape(B, n_sel, SEL_BLOCK, 2, G, HD).transpose(3, 0, 1, 4, 2, 5)
    smap = selection_map(k_cmp.shape[1], n_sel)
    kw_all, vw_all = win_all[:, :, 0], win_all[:, :, 1]
    if past is None and T > Q_BLOCK and T % Q_BLOCK == 0:
        def one_block(n):
            q0 = n * Q_BLOCK
            return nsa_attend_block(
                lax.dynamic_slice_in_dim(q, q0, Q_BLOCK, axis=1),
                lax.dynamic_slice_in_dim(gate, q0, Q_BLOCK, axis=1),
                lax.dynamic_slice_in_dim(pos, q0, Q_BLOCK, axis=0),
                lax.dynamic_slice_in_dim(kw_all, q0, Q_BLOCK + WINDOW, axis=1),
                lax.dynamic_slice_in_dim(vw_all, q0, Q_BLOCK + WINDOW, axis=1),
                lax.dynamic_slice_in_dim(w_pos, q0, Q_BLOCK + WINDOW, axis=0),
                k_cmp, v_cmp, c_end, sel_blk[0], sel_blk[1], smap)
        o = lax.map(one_block, jnp.arange(T // Q_BLOCK))
        o = jnp.moveaxis(o, 0, 1).reshape(B, T, NSA_WIDTH)
    else:
        o = nsa_attend_block(q, gate, pos, kw_all, vw_all, w_pos, k_cmp, v_cmp, c_end,
                             sel_blk[0], sel_blk[1], smap).reshape(B, T, NSA_WIDTH)
    return o, new_cmp, new_sel, win_state


def _lin_combine(e1, e2):
    a1, b1 = e1
    a2, b2 = e2
    return a1 * a2, a2 * b1 + b2


def rglru_mixer(xr, gr, conv_buf, h0, conv_w, conv_b, wa, ba, wx, bx, lam):
    B, T = xr.shape[:2]
    f32 = jnp.float32
    xcat = jnp.concatenate([conv_buf.astype(xr.dtype), xr], axis=1)
    xc = conv_b + sum(conv_w[k] * xcat[:, k:k + T] for k in range(CONV_W))
    new_buf = xcat[:, T:]
    xb = xc.reshape(B, T, RG_BLOCKS, RG_BW)
    r = jax.nn.sigmoid(jnp.einsum('btnd,nde->btne', xb, wa).reshape(B, T, RG_WIDTH) + ba)
    i = jax.nn.sigmoid(jnp.einsum('btnd,nde->btne', xb, wx).reshape(B, T, RG_WIDTH) + bx)
    log_a = -RG_C * r.astype(f32) * jax.nn.softplus(-lam.astype(f32))
    a = jnp.exp(log_a)
    b = jnp.sqrt(-jnp.expm1(2.0 * log_a)) * (i * xc).astype(f32)
    b = b.at[:, 0].add(a[:, 0] * h0.astype(f32))
    _, h = lax.associative_scan(_lin_combine, (a, b), axis=1)
    out = h.astype(xr.dtype) * jax.nn.gelu(gr)
    return out, h[:, -1].astype(h0.dtype), new_buf


def gated_recurrence(q, k, v, logf, s0):
    B, T, H, DK = q.shape
    DV = v.shape[-1]
    C = HG_CHUNK if T % HG_CHUNK == 0 else T
    nc = T // C

    def to_chunks(a):
        return jnp.moveaxis(a.reshape(B, nc, C, *a.shape[2:]), 1, 0)

    tri = jnp.tril(jnp.ones((C, C), dtype=bool))[None, :, :, None, None]

    def step(S, inp):
        qc, kc, vc, gc = inp
        bcum = jnp.cumsum(gc, axis=1)
        o = jnp.einsum('bthk,bhkv->bthv', qc * jnp.exp(bcum), S)
        dec = jnp.exp(jnp.where(tri, bcum[:, :, None] - bcum[:, None, :], -jnp.inf))
        A = jnp.einsum('bthk,bshk,btshk->bhts', qc, kc, dec)
        o = o + jnp.einsum('bhts,bshv->bthv', A, vc)
        bl = bcum[:, -1]
        S = jnp.exp(bl)[..., None] * S + jnp.einsum('bshk,bshv->bhkv', kc * jnp.exp(bl[:, None] - bcum), vc)
        return S, o

    s_fin, o = lax.scan(step, s0, (to_chunks(q), to_chunks(k), to_chunks(v), to_chunks(logf)))
    return jnp.moveaxis(o, 0, 1).reshape(B, T, H, DV), s_fin


def hgrn2_mixer(hq, hf, hi, hg, s0, lb, gain):
    B, T = hq.shape[:2]
    f32 = jnp.float32
    q = jax.nn.silu(hq.astype(f32)).reshape(B, T, HG_HEADS, HG_DK)
    lbh = lb.reshape(HG_HEADS, HG_DK)
    f = lbh + (1.0 - lbh) * jax.nn.sigmoid(hf.astype(f32).reshape(B, T, HG_HEADS, HG_DK))
    v = hi.astype(f32).reshape(B, T, HG_HEADS, HG_DV)
    o, s_fin = gated_recurrence(q, 1.0 - f, v, jnp.log(f), s0.astype(f32))
    o = rms_norm(o, gain.reshape(HG_HEADS, HG_DV)).reshape(B, T, HG_WIDTH).astype(hq.dtype)
    return o * jax.nn.silu(hg), s_fin.astype(s0.dtype)


def forward_layer(x, pos, w, past):
    (norm_mix, w_in, w_out, norm_ffn, w_up, w_down, cmp_pos, cmp_w1, cmp_b1, cmp_w2, cmp_b2,
     rg_conv_w, rg_conv_b, rg_wa, rg_ba, rg_wx, rg_bx, rg_lambda, hg_lb, hg_gain) = w
    B = x.shape[0]
    y = rms_norm(x, norm_mix)
    q, kvs, gate, rg_x, rg_g, hg_q, hg_f, hg_i, hg_g = _split(y @ w_in, IN_SIZES)
    if past is None:
        nsa_past = None
        rg_buf = jnp.zeros((B, CONV_W - 1, RG_WIDTH), x.dtype)
        rg_h0 = jnp.zeros((B, RG_WIDTH), x.dtype)
        hg_s0 = jnp.zeros((B, HG_HEADS, HG_DK, HG_DV), x.dtype)
    else:
        cmp_past, sel_past, win_buf, rg_h0, rg_buf, hg_s0 = past
        nsa_past = (cmp_past, sel_past, win_buf)
    o_nsa, new_cmp, new_sel, new_win = nsa_mixer(q, kvs, gate, pos, cmp_pos, cmp_w1, cmp_b1, cmp_w2, cmp_b2, nsa_past)
    o_rg, new_h, new_buf = rglru_mixer(rg_x, rg_g, rg_buf, rg_h0, rg_conv_w, rg_conv_b, rg_wa, rg_ba, rg_wx, rg_bx, rg_lambda)
    o_hg, new_s = hgrn2_mixer(hg_q, hg_f, hg_i, hg_g, hg_s0, hg_lb, hg_gain)
    x = x + jnp.concatenate([o_nsa, o_rg, o_hg], axis=-1) @ w_out
    hmid = jax.nn.relu(rms_norm(x, norm_ffn) @ w_up)
    x = x + (hmid * hmid) @ w_down
    return x, (new_cmp, new_sel, new_win, new_h, new_buf, new_s)


def setup_inputs(seed: int = 0) -> dict:
    key = jax.random.key(seed)
    keys = iter(jax.random.split(key, 48))
    f32 = jnp.float32

    def nrm(shape, scale):
        return jax.random.normal(next(keys), shape, f32) * scale

    n_pages = PAST_LEN // PAGE_SIZE
    n_used = DEC_BATCH * n_pages
    n_pool = (n_used * 5) // 4
    win_buf = min(WINDOW, PAST_LEN)
    x_prompt = nrm((BATCH, SEQ, D_MODEL), 1.0)
    x_sample = nrm((DEC_BATCH, DEC_SEQ, D_MODEL), 1.0)
    cache_nsa_cmp_kv = nrm((DEPTH, n_pool, PAGE_SIZE, 2, NSA_KV_HEADS, HEAD_DIM), 1.0)
    cache_nsa_sel_kv = nrm((DEPTH, n_pool, PAGE_SIZE, 2, NSA_KV_HEADS, HEAD_DIM), 1.0)
    cache_nsa_win_kv = nrm((DEPTH, DEC_BATCH, win_buf, 2, NSA_KV_HEADS, HEAD_DIM), 1.0)
    state_rglru_h = nrm((DEPTH, DEC_BATCH, RG_WIDTH), 0.5)
    state_rglru_conv = nrm((DEPTH, DEC_BATCH, CONV_W - 1, RG_WIDTH), 1.0)
    state_hgrn_s = nrm((DEPTH, DEC_BATCH, HG_HEADS, HG_DK, HG_DV), 0.3)
    page_table = jax.random.permutation(next(keys), n_pool)[:n_used].reshape(DEC_BATCH, n_pages).astype(jnp.int32)
    u = jax.random.uniform(next(keys), (DEPTH, RG_WIDTH), f32, minval=0.9, maxval=0.999)
    return {
        'x_prompt': x_prompt,
        'x_sample': x_sample,
        'cache_nsa_cmp_kv': cache_nsa_cmp_kv,
        'cache_nsa_sel_kv': cache_nsa_sel_kv,
        'cache_nsa_win_kv': cache_nsa_win_kv,
        'state_rglru_h': state_rglru_h,
        'state_rglru_conv': state_rglru_conv,
        'state_hgrn_s': state_hgrn_s,
        'page_table': page_table,
        'norm_mix': 1.0 + nrm((DEPTH, D_MODEL), 0.01),
        'w_in': nrm((DEPTH, D_MODEL, D_IN), D_MODEL ** -0.5),
        'w_out': nrm((DEPTH, MIX_WIDTH, D_MODEL), MIX_WIDTH ** -0.5),
        'norm_ffn': 1.0 + nrm((DEPTH, D_MODEL), 0.01),
        'w_up': nrm((DEPTH, D_MODEL, D_FF), D_MODEL ** -0.5),
        'w_down': nrm((DEPTH, D_FF, D_MODEL), D_FF ** -0.5),
        'cmp_pos': nrm((DEPTH, 2, CMP_BLOCK, HEAD_DIM), 0.1),
        'cmp_w1': nrm((DEPTH, 2, CMP_BLOCK, HEAD_DIM, CMP_HIDDEN), (CMP_BLOCK * HEAD_DIM) ** -0.5),
        'cmp_b1': nrm((DEPTH, 2, CMP_HIDDEN), 0.01),
        'cmp_w2': nrm((DEPTH, 2, CMP_HIDDEN, HEAD_DIM), CMP_HIDDEN ** -0.5),
        'cmp_b2': nrm((DEPTH, 2, HEAD_DIM), 0.01),
        'rg_conv_w': nrm((DEPTH, CONV_W, RG_WIDTH), CONV_W ** -0.5),
        'rg_conv_b': nrm((DEPTH, RG_WIDTH), 0.01),
        'rg_wa': nrm((DEPTH, RG_BLOCKS, RG_BW, RG_BW), RG_BW ** -0.5),
        'rg_ba': nrm((DEPTH, RG_WIDTH), 0.01),
        'rg_wx': nrm((DEPTH, RG_BLOCKS, RG_BW, RG_BW), RG_BW ** -0.5),
        'rg_bx': nrm((DEPTH, RG_WIDTH), 0.01),
        'rg_lambda': jnp.log(u) - jnp.log1p(-u),
        'hg_lower_bounds': nrm((DEPTH, HG_HEADS * HG_DK), 0.1),
        'hg_gain': 1.0 + nrm((DEPTH, HG_WIDTH), 0.01),
        'final_norm': 1.0 + nrm((D_MODEL,), 0.01),
    }


def reference(x_prompt, x_sample, cache_nsa_cmp_kv, cache_nsa_sel_kv, cache_nsa_win_kv, state_rglru_h,
              state_rglru_conv, state_hgrn_s, page_table, norm_mix, w_in, w_out, norm_ffn, w_up, w_down,
              cmp_pos, cmp_w1, cmp_b1, cmp_w2, cmp_b2, rg_conv_w, rg_conv_b, rg_wa, rg_ba, rg_wx, rg_bx,
              rg_lambda, hg_lower_bounds, hg_gain, final_norm):
    lb = jnp.cumsum(jax.nn.softmax(hg_lower_bounds.astype(jnp.float32), axis=0), axis=0)
    lb = lb - lb[0]
    past_len = page_table.shape[1] * PAGE_SIZE
    n_dec = x_sample.shape[0]
    pos_p = jnp.arange(x_prompt.shape[1], dtype=jnp.int32)
    pos_s = past_len + jnp.arange(x_sample.shape[1], dtype=jnp.int32)
    xp, xs = x_prompt, x_sample
    st_p, st_s = [], []
    for l in range(DEPTH):
        w = (norm_mix[l], w_in[l], w_out[l], norm_ffn[l], w_up[l], w_down[l], cmp_pos[l], cmp_w1[l], cmp_b1[l],
             cmp_w2[l], cmp_b2[l], rg_conv_w[l], rg_conv_b[l], rg_wa[l], rg_ba[l], rg_wx[l], rg_bx[l],
             rg_lambda[l], lb[l], hg_gain[l])
        xp, sp = forward_layer(xp, pos_p, w, None)
        past = (cache_nsa_cmp_kv[l][page_table].reshape(n_dec, past_len, 2, NSA_KV_HEADS, HEAD_DIM),
                cache_nsa_sel_kv[l][page_table].reshape(n_dec, past_len, 2, NSA_KV_HEADS, HEAD_DIM),
                cache_nsa_win_kv[l], state_rglru_h[l], state_rglru_conv[l], state_hgrn_s[l])
        xs, ss = forward_layer(xs, pos_s, w, past)
        st_p.append(sp)
        st_s.append(ss)

    def stack(sts, i):
        return jnp.stack([s[i] for s in sts], axis=0)

    y_prompt = rms_norm(xp, final_norm)
    y_sample = rms_norm(xs, final_norm)
    return (y_prompt, y_sample, stack(st_p, 0), stack(st_p, 1), stack(st_p, 2), stack(st_p, 3), stack(st_p, 4),
            stack(st_p, 5), stack(st_s, 0), stack(st_s, 1), stack(st_s, 2), stack(st_s, 3), stack(st_s, 4),
            stack(st_s, 5))
```

```python
import functools

import jax
import jax.numpy as jnp
from jax import lax
from jax.experimental import pallas as pl
from jax.experimental.pallas import tpu as pltpu

F32 = jnp.float32
BF16 = jnp.bfloat16

D_MODEL = 1024
DEPTH = 4
PAGE_SIZE = 128
HEAD_DIM = 64
NSA_HEADS = D_MODEL // (2 * HEAD_DIM)
NSA_KV_HEADS = 2
NSA_HPG = NSA_HEADS // NSA_KV_HEADS
NSA_WIDTH = NSA_HEADS * HEAD_DIM
KV_W = NSA_KV_HEADS * HEAD_DIM
CMP_BLOCK = 32
CMP_STRIDE = 16
CMP_RATIO = CMP_BLOCK // CMP_STRIDE
CMP_HIDDEN = 256
SEL_BLOCK = 64
SEL_TOPK = 16
WINDOW = 512
Q_BLOCK = 128
FORCE_SCORE = 1e6
ROPE_DIM = HEAD_DIM // 4
ROPE_THETA = 500000.0
RG_WIDTH = D_MODEL // 4
RG_BLOCKS = 4
RG_BW = RG_WIDTH // RG_BLOCKS
RG_C = 8.0
CONV_W = 4
HG_HEADS = 4
HG_DK = 64
HG_DV = D_MODEL // 4 // HG_HEADS
HG_WIDTH = HG_HEADS * HG_DV
HG_CHUNK = 64
MIX_WIDTH = NSA_WIDTH + RG_WIDTH + HG_WIDTH
D_FF = 4 * D_MODEL
EPS = 1e-6
IN_SIZES = (NSA_WIDTH, 6 * KV_W, 3 * NSA_HEADS, RG_WIDTH, RG_WIDTH, HG_HEADS * HG_DK, HG_HEADS * HG_DK, HG_WIDTH, HG_WIDTH)
D_IN = sum(IN_SIZES)

LANE = 128
VMEM_LIMIT = 56 * 1024 * 1024


def _round_up(n, m):
    return -(-n // m) * m


def _row_tile(m, want):
    t = min(m, want)
    while m % t:
        t //= 2
    return t


def _rms(x, g):
    return x * lax.rsqrt(jnp.mean(x * x, axis=-1, keepdims=True) + EPS) * g


def _norm_matmul_body(x_ref, g_ref, w_ref, o_ref):
    y = _rms(x_ref[...], g_ref[...]).astype(BF16)
    o_ref[...] = jnp.dot(y, w_ref[...], preferred_element_type=F32)


def norm_matmul(x, g, w_bf16, tm=512):
    m, d = x.shape
    n = w_bf16.shape[1]
    tm = _row_tile(m, tm)
    return pl.pallas_call(
        _norm_matmul_body,
        grid=(m // tm,),
        in_specs=[
            pl.BlockSpec((tm, d), lambda i: (i, 0)),
            pl.BlockSpec((1, d), lambda i: (0, 0)),
            pl.BlockSpec((d, n), lambda i: (0, 0)),
        ],
        out_specs=pl.BlockSpec((tm, n), lambda i: (i, 0)),
        out_shape=jax.ShapeDtypeStruct((m, n), F32),
        compiler_params=pltpu.CompilerParams(
            dimension_semantics=("arbitrary",), vmem_limit_bytes=VMEM_LIMIT),
        name="norm_in_proj",
    )(x, g.reshape(1, d), w_bf16)


def _out_ffn_body(x_ref, mix_ref, wo_ref, g_ref, wu_ref, wd_ref, o_ref, *, f_tile):
    x1 = x_ref[...] + jnp.dot(mix_ref[...].astype(BF16), wo_ref[...], preferred_element_type=F32)
    y = _rms(x1, g_ref[...]).astype(BF16)
    acc = x1
    for j in range(wu_ref.shape[1] // f_tile):
        h = jnp.maximum(jnp.dot(y, wu_ref[:, j * f_tile:(j + 1) * f_tile], preferred_element_type=F32), 0.0)
        acc = acc + jnp.dot((h * h).astype(BF16), wd_ref[j * f_tile:(j + 1) * f_tile, :], preferred_element_type=F32)
    o_ref[...] = acc


def out_ffn(x, mix, wo, g, wu, wd, tm=512, f_tile=512):
    m, d = x.shape
    dm = mix.shape[1]
    dff = wu.shape[1]
    tm = _row_tile(m, tm)
    const = lambda i: (0, 0)
    return pl.pallas_call(
        functools.partial(_out_ffn_body, f_tile=f_tile),
        grid=(m // tm,),
        in_specs=[
            pl.BlockSpec((tm, d), lambda i: (i, 0)),
            pl.BlockSpec((tm, dm), lambda i: (i, 0)),
            pl.BlockSpec((dm, d), const),
            pl.BlockSpec((1, d), const),
            pl.BlockSpec((d, dff), const),
            pl.BlockSpec((dff, d), const),
        ],
        out_specs=pl.BlockSpec((tm, d), lambda i: (i, 0)),
        out_shape=jax.ShapeDtypeStruct((m, d), F32),
        compiler_params=pltpu.CompilerParams(
            dimension_semantics=("arbitrary",), vmem_limit_bytes=VMEM_LIMIT),
        name="out_proj_ffn",
    )(x, mix, wo, g.reshape(1, d), wu, wd)


def _final_norm_body(x_ref, g_ref, o_ref):
    o_ref[...] = _rms(x_ref[...], g_ref[...])


def final_rms(x, g, tm=1024):
    m, d = x.shape
    tm = _row_tile(m, tm)
    return pl.pallas_call(
        _final_norm_body,
        grid=(m // tm,),
        in_specs=[pl.BlockSpec((tm, d), lambda i: (i, 0)), pl.BlockSpec((1, d), lambda i: (0, 0))],
        out_specs=pl.BlockSpec((tm, d), lambda i: (i, 0)),
        out_shape=jax.ShapeDtypeStruct((m, d), F32),
        name="final_norm",
    )(x, g.reshape(1, d))


def _split(a, sizes):
    out, o = [], 0
    for s in sizes:
        out.append(a[..., o:o + s])
        o += s
    return out


def rms_norm(x, g):
    x32 = x.astype(jnp.float32)
    y = x32 * lax.rsqrt(jnp.mean(x32 * x32, axis=-1, keepdims=True) + EPS)
    return (y * g.astype(jnp.float32)).astype(x.dtype)


def partial_rope(x, pos):
    half = ROPE_DIM // 2
    inv = ROPE_THETA ** (-jnp.arange(half, dtype=jnp.float32) * 2.0 / ROPE_DIM)
    ang = pos.astype(jnp.float32)[:, None] * inv
    cos = jnp.cos(ang)[:, None, :].astype(x.dtype)
    sin = jnp.sin(ang)[:, None, :].astype(x.dtype)
    x1, x2, rest = x[..., :half], x[..., half:ROPE_DIM], x[..., ROPE_DIM:]
    return jnp.concatenate([x1 * cos - x2 * sin, x2 * cos + x1 * sin, rest], axis=-1)


def masked_softmax(s, mask):
    s = jnp.where(mask, s, -1e30)
    m = jnp.max(s, axis=-1, keepdims=True)
    e = jnp.where(mask, jnp.exp(s - m), 0.0)
    return e / jnp.maximum(jnp.sum(e, axis=-1, keepdims=True), 1e-30)


def compress_kv(raw, pos_emb, w1, b1, w2, b2):
    B, L = raw.shape[0], raw.shape[1]
    nch = L // CMP_STRIDE
    n_cmp = nch - CMP_RATIO + 1
    chunks = raw[:, :nch * CMP_STRIDE].reshape(B, nch, CMP_STRIDE, NSA_KV_HEADS, HEAD_DIM)
    h = b1
    for r in range(CMP_RATIO):
        sl = slice(r * CMP_STRIDE, (r + 1) * CMP_STRIDE)
        pre = jnp.einsum('bnsgd,sdh->bngh', chunks + pos_emb[sl][:, None, :], w1[sl])
        h = h + pre[:, r:r + n_cmp]
    out = jax.nn.gelu(h) @ w2 + b2
    end = jnp.arange(n_cmp, dtype=jnp.int32) * CMP_STRIDE + CMP_BLOCK - 1
    return out, end


def selection_map(n_cmp, n_sel):
    c0 = jnp.arange(n_cmp) * CMP_STRIDE
    s0 = jnp.arange(n_sel) * SEL_BLOCK
    ov = (c0[:, None] < s0[None, :] + SEL_BLOCK) & (c0[:, None] + CMP_BLOCK > s0[None, :])
    return ov.astype(jnp.float32)


def nsa_attend_block(q, gate, t_pos, kw, vw, w_pos, k_cmp, v_cmp, c_end, ks_blk, vs_blk, smap):
    f32 = jnp.float32
    scale = HEAD_DIM ** -0.5
    s_c = jnp.einsum('bqghd,bngd->bqghn', q, k_cmp).astype(f32) * scale
    m_c = (c_end[None, :] <= t_pos[:, None])[None, :, None, None, :]
    p_c = masked_softmax(s_c, m_c)
    o_c = jnp.einsum('bqghn,bngd->bqghd', p_c.astype(v_cmp.dtype), v_cmp)
    imp = jnp.einsum('bqghn,nj->bqgj', p_c, smap)
    n_sel = ks_blk.shape[1]
    j = jnp.arange(n_sel)[None, :]
    cur = (t_pos // SEL_BLOCK)[:, None]
    valid = j * SEL_BLOCK <= t_pos[:, None]
    forced = (j == 0) | (j == cur) | (j == cur - 1)
    score = jnp.where(valid[None, :, None, :], jnp.where(forced[None, :, None, :], FORCE_SCORE, imp), -FORCE_SCORE)
    _, idx = lax.top_k(score, min(SEL_TOPK, n_sel))
    bi = jnp.arange(q.shape[0])[:, None, None, None]
    gi = jnp.arange(NSA_KV_HEADS)[None, None, :, None]
    kb = ks_blk[bi, idx, gi]
    vb = vs_blk[bi, idx, gi]
    kpos = idx[..., None] * SEL_BLOCK + jnp.arange(SEL_BLOCK)
    m_s = (kpos <= t_pos[None, :, None, None, None])[:, :, :, None]
    s_s = jnp.einsum('bqghd,bqgkld->bqghkl', q, kb).astype(f32) * scale
    shp = s_s.shape
    p_s = masked_softmax(s_s.reshape(*shp[:4], -1), m_s.reshape(*m_s.shape[:4], -1)).reshape(shp)
    o_s = jnp.einsum('bqghkl,bqgkld->bqghd', p_s.astype(vb.dtype), vb)
    s_w = jnp.einsum('bqghd,bwgd->bqghw', q, kw).astype(f32) * scale
    dist = t_pos[:, None] - w_pos[None, :]
    m_w = ((dist >= 0) & (dist < WINDOW) & (w_pos[None, :] >= 0))[None, :, None, None, :]
    p_w = masked_softmax(s_w, m_w)
    o_w = jnp.einsum('bqghw,bwgd->bqghd', p_w.astype(vw.dtype), vw)
    return gate[..., 0:1] * o_c + gate[..., 1:2] * o_s + gate[..., 2:3] * o_w


def nsa_mixer(q, kvs, gate, pos, cmp_pos, cmp_w1, cmp_b1, cmp_w2, cmp_b2, past):
    B, T = q.shape[:2]
    G, HD = NSA_KV_HEADS, HEAD_DIM
    q = partial_rope(q.reshape(B, T, NSA_HEADS, HD), pos).reshape(B, T, G, NSA_HPG, HD)
    kc, vc, ks, vs, kw, vw = [a.reshape(B, T, G, HD) for a in _split(kvs, (KV_W,) * 6)]
    ks = partial_rope(ks, pos)
    kw = partial_rope(kw, pos)
    gate = jax.nn.sigmoid(gate.reshape(B, T, G, NSA_HPG, 3))
    new_cmp = jnp.stack([kc, vc], axis=2)
    new_sel = jnp.stack([ks, vs], axis=2)
    new_win = jnp.stack([kw, vw], axis=2)
    if past is None:
        cmp_all, sel_all = new_cmp, new_sel
        win_all = jnp.pad(new_win, ((0, 0), (WINDOW, 0), (0, 0), (0, 0), (0, 0)))
        w_pos = pos[0] - WINDOW + jnp.arange(T + WINDOW, dtype=jnp.int32)
        win_state = new_win[:, T - min(WINDOW, T):]
    else:
        cmp_past, sel_past, win_buf = past
        cmp_all = jnp.concatenate([cmp_past, new_cmp], axis=1)
        sel_all = jnp.concatenate([sel_past, new_sel], axis=1)
        win_all = jnp.concatenate([win_buf, new_win], axis=1)
        nb = win_buf.shape[1]
        w_pos = pos[0] - nb + jnp.arange(nb + T, dtype=jnp.int32)
        win_state = win_all[:, T:]
    L = cmp_all.shape[1]
    k_cmp, c_end = compress_kv(cmp_all[:, :, 0], cmp_pos[0], cmp_w1[0], cmp_b1[0], cmp_w2[0], cmp_b2[0])
    v_cmp, _ = compress_kv(cmp_all[:, :, 1], cmp_pos[1], cmp_w1[1], cmp_b1[1], cmp_w2[1], cmp_b2[1])
    k_cmp = partial_rope(k_cmp, c_end)
    n_sel = -(-L // SEL_BLOCK)
    sel_pad = jnp.pad(sel_all, ((0, 0), (0, n_sel * SEL_BLOCK - L), (0, 0), (0, 0), (0, 0)))
    sel_blk = sel_pad.reshape(B, n_sel, SEL_BLOCK, 2, G, HD).transpose(3, 0, 1, 4, 2, 5)
    smap = selection_map(k_cmp.shape[1], n_sel)
    kw_all, vw_all = win_all[:, :, 0], win_all[:, :, 1]
    if past is None and T > Q_BLOCK and T % Q_BLOCK == 0:
        def one_block(n):
            q0 = n * Q_BLOCK
            return nsa_attend_block(
                lax.dynamic_slice_in_dim(q, q0, Q_BLOCK, axis=1),
                lax.dynamic_slice_in_dim(gate, q0, Q_BLOCK, axis=1),
                lax.dynamic_slice_in_dim(pos, q0, Q_BLOCK, axis=0),
                lax.dynamic_slice_in_dim(kw_all, q0, Q_BLOCK + WINDOW, axis=1),
                lax.dynamic_slice_in_dim(vw_all, q0, Q_BLOCK + WINDOW, axis=1),
                lax.dynamic_slice_in_dim(w_pos, q0, Q_BLOCK + WINDOW, axis=0),
                k_cmp, v_cmp, c_end, sel_blk[0], sel_blk[1], smap)
        o = lax.map(one_block, jnp.arange(T // Q_BLOCK))
        o = jnp.moveaxis(o, 0, 1).reshape(B, T, NSA_WIDTH)
    else:
        o = nsa_attend_block(q, gate, pos, kw_all, vw_all, w_pos, k_cmp, v_cmp, c_end,
                             sel_blk[0], sel_blk[1], smap).reshape(B, T, NSA_WIDTH)
    return o, new_cmp, new_sel, win_state


def _lin_combine(e1, e2):
    a1, b1 = e1
    a2, b2 = e2
    return a1 * a2, a2 * b1 + b2


def rglru_mixer(xr, gr, conv_buf, h0, conv_w, conv_b, wa, ba, wx, bx, lam):
    B, T = xr.shape[:2]
    f32 = jnp.float32
    xcat = jnp.concatenate([conv_buf.astype(xr.dtype), xr], axis=1)
    xc = conv_b + sum(conv_w[k] * xcat[:, k:k + T] for k in range(CONV_W))
    new_buf = xcat[:, T:]
    xb = xc.reshape(B, T, RG_BLOCKS, RG_BW)
    r = jax.nn.sigmoid(jnp.einsum('btnd,nde->btne', xb, wa).reshape(B, T, RG_WIDTH) + ba)
    i = jax.nn.sigmoid(jnp.einsum('btnd,nde->btne', xb, wx).reshape(B, T, RG_WIDTH) + bx)
    log_a = -RG_C * r.astype(f32) * jax.nn.softplus(-lam.astype(f32))
    a = jnp.exp(log_a)
    b = jnp.sqrt(-jnp.expm1(2.0 * log_a)) * (i * xc).astype(f32)
    b = b.at[:, 0].add(a[:, 0] * h0.astype(f32))
    _, h = lax.associative_scan(_lin_combine, (a, b), axis=1)
    out = h.astype(xr.dtype) * jax.nn.gelu(gr)
    return out, h[:, -1].astype(h0.dtype), new_buf


def gated_recurrence(q, k, v, logf, s0):
    B, T, H, DK = q.shape
    DV = v.shape[-1]
    C = HG_CHUNK if T % HG_CHUNK == 0 else T
    nc = T // C

    def to_chunks(a):
        return jnp.moveaxis(a.reshape(B, nc, C, *a.shape[2:]), 1, 0)

    tri = jnp.tril(jnp.ones((C, C), dtype=bool))[None, :, :, None, None]

    def step(S, inp):
        qc, kc, vc, gc = inp
        bcum = jnp.cumsum(gc, axis=1)
        o = jnp.einsum('bthk,bhkv->bthv', qc * jnp.exp(bcum), S)
        dec = jnp.exp(jnp.where(tri, bcum[:, :, None] - bcum[:, None, :], -jnp.inf))
        A = jnp.einsum('bthk,bshk,btshk->bhts', qc, kc, dec)
        o = o + jnp.einsum('bhts,bshv->bthv', A, vc)
        bl = bcum[:, -1]
        S = jnp.exp(bl)[..., None] * S + jnp.einsum('bshk,bshv->bhkv', kc * jnp.exp(bl[:, None] - bcum), vc)
        return S, o

    s_fin, o = lax.scan(step, s0, (to_chunks(q), to_chunks(k), to_chunks(v), to_chunks(logf)))
    return jnp.moveaxis(o, 0, 1).reshape(B, T, H, DV), s_fin


def hgrn2_mixer(hq, hf, hi, hg, s0, lb, gain):
    B, T = hq.shape[:2]
    f32 = jnp.float32
    q = jax.nn.silu(hq.astype(f32)).reshape(B, T, HG_HEADS, HG_DK)
    lbh = lb.reshape(HG_HEADS, HG_DK)
    f = lbh + (1.0 - lbh) * jax.nn.sigmoid(hf.astype(f32).reshape(B, T, HG_HEADS, HG_DK))
    v = hi.astype(f32).reshape(B, T, HG_HEADS, HG_DV)
    o, s_fin = gated_recurrence(q, 1.0 - f, v, jnp.log(f), s0.astype(f32))
    o = rms_norm(o, gain.reshape(HG_HEADS, HG_DV)).reshape(B, T, HG_WIDTH).astype(hq.dtype)
    return o * jax.nn.silu(hg), s_fin.astype(s0.dtype)


def forward_layer(x, pos, w, wb, past):
    (norm_mix, w_in, w_out, norm_ffn, w_up, w_down, cmp_pos, cmp_w1, cmp_b1, cmp_w2, cmp_b2,
     rg_conv_w, rg_conv_b, rg_wa, rg_ba, rg_wx, rg_bx, rg_lambda, hg_lb, hg_gain) = w
    w_in_b, w_out_b, w_up_b, w_down_b = wb
    B, T = x.shape[:2]
    x2 = x.reshape(B * T, D_MODEL)
    proj = norm_matmul(x2, norm_mix, w_in_b)[:, :D_IN].reshape(B, T, D_IN)
    q, kvs, gate, rg_x, rg_g, hg_q, hg_f, hg_i, hg_g = _split(proj, IN_SIZES)
    if past is None:
        nsa_past = None
        rg_buf = jnp.zeros((B, CONV_W - 1, RG_WIDTH), x.dtype)
        rg_h0 = jnp.zeros((B, RG_WIDTH), x.dtype)
        hg_s0 = jnp.zeros((B, HG_HEADS, HG_DK, HG_DV), x.dtype)
    else:
        cmp_past, sel_past, win_buf, rg_h0, rg_buf, hg_s0 = past
        nsa_past = (cmp_past, sel_past, win_buf)
    o_nsa, new_cmp, new_sel, new_win = nsa_mixer(q, kvs, gate, pos, cmp_pos, cmp_w1, cmp_b1, cmp_w2, cmp_b2, nsa_past)
    o_rg, new_h, new_buf = rglru_mixer(rg_x, rg_g, rg_buf, rg_h0, rg_conv_w, rg_conv_b, rg_wa, rg_ba, rg_wx, rg_bx, rg_lambda)
    o_hg, new_s = hgrn2_mixer(hg_q, hg_f, hg_i, hg_g, hg_s0, hg_lb, hg_gain)
    mix = jnp.concatenate([o_nsa, o_rg, o_hg], axis=-1).reshape(B * T, MIX_WIDTH)
    x2 = out_ffn(x2, mix, w_out_b, norm_ffn, w_up_b, w_down_b)
    return x2.reshape(B, T, D_MODEL), (new_cmp, new_sel, new_win, new_h, new_buf, new_s)


def kernel(x_prompt, x_sample, cache_nsa_cmp_kv, cache_nsa_sel_kv, cache_nsa_win_kv, state_rglru_h,
           state_rglru_conv, state_hgrn_s, page_table, norm_mix, w_in, w_out, norm_ffn, w_up, w_down,
           cmp_pos, cmp_w1, cmp_b1, cmp_w2, cmp_b2, rg_conv_w, rg_conv_b, rg_wa, rg_ba, rg_wx, rg_bx,
           rg_lambda, hg_lower_bounds, hg_gain, final_norm):
    lb = jnp.cumsum(jax.nn.softmax(hg_lower_bounds.astype(jnp.float32), axis=0), axis=0)
    lb = lb - lb[0]
    past_len = page_table.shape[1] * PAGE_SIZE
    n_dec = x_sample.shape[0]
    pos_p = jnp.arange(x_prompt.shape[1], dtype=jnp.int32)
    pos_s = past_len + jnp.arange(x_sample.shape[1], dtype=jnp.int32)
    d_in_pad = _round_up(D_IN, LANE)
    w_in_b = jnp.pad(w_in, ((0, 0), (0, 0), (0, d_in_pad - D_IN))).astype(BF16)
    w_out_b = w_out.astype(BF16)
    w_up_b = w_up.astype(BF16)
    w_down_b = w_down.astype(BF16)
    xp, xs = x_prompt, x_sample
    st_p, st_s = [], []
    for l in range(DEPTH):
        w = (norm_mix[l], w_in[l], w_out[l], norm_ffn[l], w_up[l], w_down[l], cmp_pos[l], cmp_w1[l], cmp_b1[l],
             cmp_w2[l], cmp_b2[l], rg_conv_w[l], rg_conv_b[l], rg_wa[l], rg_ba[l], rg_wx[l], rg_bx[l],
             rg_lambda[l], lb[l], hg_gain[l])
        wb = (w_in_b[l], w_out_b[l], w_up_b[l], w_down_b[l])
        xp, sp = forward_layer(xp, pos_p, w, wb, None)
        past = (cache_nsa_cmp_kv[l][page_table].reshape(n_dec, past_len, 2, NSA_KV_HEADS, HEAD_DIM),
                cache_nsa_sel_kv[l][page_table].reshape(n_dec, past_len, 2, NSA_KV_HEADS, HEAD_DIM),
                cache_nsa_win_kv[l], state_rglru_h[l], state_rglru_conv[l], state_hgrn_s[l])
        xs, ss = forward_layer(xs, pos_s, w, wb, past)
        st_p.append(sp)
        st_s.append(ss)

    def stack(sts, i):
        return jnp.stack([s[i] for s in sts], axis=0)

    y_prompt = final_rms(xp.reshape(-1, D_MODEL), final_norm).reshape(xp.shape)
    y_sample = final_rms(xs.reshape(-1, D_MODEL), final_norm).reshape(xs.shape)
    return (y_prompt, y_sample, stack(st_p, 0), stack(st_p, 1), stack(st_p, 2), stack(st_p, 3), stack(st_p, 4),
            stack(st_p, 5), stack(st_s, 0), stack(st_s, 1), stack(st_s, 2), stack(st_s, 3), stack(st_s, 4),
            stack(st_s, 5))
```

```python
import functools

import jax
import jax.numpy as jnp
from jax import lax
from jax.experimental import pallas as pl
from jax.experimental.pallas import tpu as pltpu

F32 = jnp.float32
BF16 = jnp.bfloat16

D_MODEL = 1024
DEPTH = 4
PAGE_SIZE = 128
HEAD_DIM = 64
NSA_HEADS = D_MODEL // (2 * HEAD_DIM)
NSA_KV_HEADS = 2
NSA_HPG = NSA_HEADS // NSA_KV_HEADS
NSA_WIDTH = NSA_HEADS * HEAD_DIM
KV_W = NSA_KV_HEADS * HEAD_DIM
CMP_BLOCK = 32
CMP_STRIDE = 16
CMP_RATIO = CMP_BLOCK // CMP_STRIDE
CMP_HIDDEN = 256
SEL_BLOCK = 64
SEL_TOPK = 16
WINDOW = 512
Q_BLOCK = 128
FORCE_SCORE = 1e6
ROPE_DIM = HEAD_DIM // 4
ROPE_THETA = 500000.0
RG_WIDTH = D_MODEL // 4
RG_BLOCKS = 4
RG_BW = RG_WIDTH // RG_BLOCKS
RG_C = 8.0
CONV_W = 4
HG_HEADS = 4
HG_DK = 64
HG_DV = D_MODEL // 4 // HG_HEADS
HG_WIDTH = HG_HEADS * HG_DV
HG_CHUNK = 64
MIX_WIDTH = NSA_WIDTH + RG_WIDTH + HG_WIDTH
D_FF = 4 * D_MODEL
EPS = 1e-6
IN_SIZES = (NSA_WIDTH, 6 * KV_W, 3 * NSA_HEADS, RG_WIDTH, RG_WIDTH, HG_HEADS * HG_DK, HG_HEADS * HG_DK, HG_WIDTH, HG_WIDTH)
D_IN = sum(IN_SIZES)

LANE = 128
VMEM_LIMIT = 56 * 1024 * 1024


def _round_up(n, m):
    return -(-n // m) * m


def _row_tile(m, want):
    t = min(m, want)
    while m % t:
        t //= 2
    return t


def _rms(x, g):
    return x * lax.rsqrt(jnp.mean(x * x, axis=-1, keepdims=True) + EPS) * g


def _norm_matmul_body(x_ref, g_ref, w_ref, o_ref):
    y = _rms(x_ref[...], g_ref[...]).astype(BF16)
    o_ref[...] = jnp.dot(y, w_ref[...], preferred_element_type=F32)


def norm_matmul(x, g, w_bf16, tm=512):
    m, d = x.shape
    n = w_bf16.shape[1]
    tm = _row_tile(m, tm)
    return pl.pallas_call(
        _norm_matmul_body,
        grid=(m // tm,),
        in_specs=[
            pl.BlockSpec((tm, d), lambda i: (i, 0)),
            pl.BlockSpec((1, d), lambda i: (0, 0)),
            pl.BlockSpec((d, n), lambda i: (0, 0)),
        ],
        out_specs=pl.BlockSpec((tm, n), lambda i: (i, 0)),
        out_shape=jax.ShapeDtypeStruct((m, n), F32),
        compiler_params=pltpu.CompilerParams(
            dimension_semantics=("arbitrary",), vmem_limit_bytes=VMEM_LIMIT),
        name="norm_in_proj",
    )(x, g.reshape(1, d), w_bf16)


def _out_ffn_body(x_ref, mix_ref, wo_ref, g_ref, wu_ref, wd_ref, o_ref, *, f_tile):
    x1 = x_ref[...] + jnp.dot(mix_ref[...].astype(BF16), wo_ref[...], preferred_element_type=F32)
    y = _rms(x1, g_ref[...]).astype(BF16)
    acc = x1
    for j in range(wu_ref.shape[1] // f_tile):
        h = jnp.maximum(jnp.dot(y, wu_ref[:, j * f_tile:(j + 1) * f_tile], preferred_element_type=F32), 0.0)
        acc = acc + jnp.dot((h * h).astype(BF16), wd_ref[j * f_tile:(j + 1) * f_tile, :], preferred_element_type=F32)
    o_ref[...] = acc


def out_ffn(x, mix, wo, g, wu, wd, tm=512, f_tile=512):
    m, d = x.shape
    dm = mix.shape[1]
    dff = wu.shape[1]
    tm = _row_tile(m, tm)
    const = lambda i: (0, 0)
    return pl.pallas_call(
        functools.partial(_out_ffn_body, f_tile=f_tile),
        grid=(m // tm,),
        in_specs=[
            pl.BlockSpec((tm, d), lambda i: (i, 0)),
            pl.BlockSpec((tm, dm), lambda i: (i, 0)),
            pl.BlockSpec((dm, d), const),
            pl.BlockSpec((1, d), const),
            pl.BlockSpec((d, dff), const),
            pl.BlockSpec((dff, d), const),
        ],
        out_specs=pl.BlockSpec((tm, d), lambda i: (i, 0)),
        out_shape=jax.ShapeDtypeStruct((m, d), F32),
        compiler_params=pltpu.CompilerParams(
            dimension_semantics=("arbitrary",), vmem_limit_bytes=VMEM_LIMIT),
        name="out_proj_ffn",
    )(x, mix, wo, g.reshape(1, d), wu, wd)


def _final_norm_body(x_ref, g_ref, o_ref):
    o_ref[...] = _rms(x_ref[...], g_ref[...])


def final_rms(x, g, tm=1024):
    m, d = x.shape
    tm = _row_tile(m, tm)
    return pl.pallas_call(
        _final_norm_body,
        grid=(m // tm,),
        in_specs=[pl.BlockSpec((tm, d), lambda i: (i, 0)), pl.BlockSpec((1, d), lambda i: (0, 0))],
        out_specs=pl.BlockSpec((tm, d), lambda i: (i, 0)),
        out_shape=jax.ShapeDtypeStruct((m, d), F32),
        name="final_norm",
    )(x, g.reshape(1, d))


MASK_BIG = 2.0 ** 100

_O_Q = 0
_O_KV = NSA_WIDTH
_O_GATE = _O_KV + 6 * KV_W
_O_RG = _O_GATE + 3 * NSA_HEADS
_O_HG = _O_RG + 2 * RG_WIDTH

_P_QZ = 0
_P_CMP = _P_QZ + NSA_HEADS * LANE
_P_SEL = _P_CMP + 2 * KV_W
_P_WIN = _P_SEL + 2 * KV_W
_P_KVS = _P_WIN + 2 * KV_W
_P_KVW = _P_KVS + 2 * KV_W
_P_GATE = _P_KVW + 2 * KV_W
_P_RG = _P_GATE + NSA_KV_HEADS * LANE
_P_HG = _P_RG + 2 * RG_WIDTH
_P_END = _P_HG + 4 * HG_WIDTH


def _in_proj_columns():
    import numpy as np
    src = -np.ones((_P_END,), np.int64)
    for hq in range(NSA_HEADS):
        src[_P_QZ + hq * LANE:_P_QZ + hq * LANE + HEAD_DIM] = _O_Q + hq * HEAD_DIM + np.arange(HEAD_DIM)
    src[_P_CMP:_P_CMP + 2 * KV_W] = _O_KV + np.arange(2 * KV_W)
    src[_P_SEL:_P_SEL + 2 * KV_W] = _O_KV + 2 * KV_W + np.arange(2 * KV_W)
    src[_P_WIN:_P_WIN + 2 * KV_W] = _O_KV + 4 * KV_W + np.arange(2 * KV_W)
    for base, off in ((_P_KVS, _O_KV + 2 * KV_W), (_P_KVW, _O_KV + 4 * KV_W)):
        for g in range(NSA_KV_HEADS):
            src[base + g * LANE:base + g * LANE + HEAD_DIM] = off + g * HEAD_DIM + np.arange(HEAD_DIM)
            src[base + g * LANE + HEAD_DIM:base + (g + 1) * LANE] = off + KV_W + g * HEAD_DIM + np.arange(HEAD_DIM)
    ng = 3 * NSA_HPG
    for g in range(NSA_KV_HEADS):
        src[_P_GATE + g * LANE:_P_GATE + g * LANE + ng] = _O_GATE + g * ng + np.arange(ng)
    src[_P_RG:_P_RG + 2 * RG_WIDTH] = _O_RG + np.arange(2 * RG_WIDTH)
    src[_P_HG:_P_HG + 4 * HG_WIDTH] = _O_HG + np.arange(4 * HG_WIDTH)
    return src


def relayout_w_in(w_in):
    import numpy as np
    src = _in_proj_columns()
    cols = jnp.take(w_in, jnp.asarray(np.maximum(src, 0), jnp.int32), axis=-1)
    return jnp.where(jnp.asarray(src >= 0), cols, 0.0).astype(BF16)


def rope_tables(pos):
    import numpy as np
    half = ROPE_DIM // 2
    inv = ROPE_THETA ** (-jnp.arange(half, dtype=F32) * 2.0 / ROPE_DIM)
    ang = pos.astype(F32)[:, None] * inv
    cos, sin = jnp.cos(ang), jnp.sin(ang)
    lane = np.arange(LANE)
    d = lane % HEAD_DIM
    idx = jnp.asarray(d % half, jnp.int32)
    cos_l, sin_l = jnp.take(cos, idx, axis=1), jnp.take(sin, idx, axis=1)
    out = []
    for active in (lane < ROPE_DIM, d < ROPE_DIM):
        first = jnp.asarray(active & (d < half))
        second = jnp.asarray(active & (d >= half) & (d < ROPE_DIM))
        out += [jnp.where(first | second, cos_l, 1.0), jnp.where(first, -sin_l, 0.0), jnp.where(second, sin_l, 0.0)]
    return tuple(out)


def _rope128(x, c, s1, s2):
    return x * c + pltpu.roll(x, LANE - ROPE_DIM // 2, 1) * s1 + pltpu.roll(x, ROPE_DIM // 2, 1) * s2


def _in_proj_prompt_body(x_ref, g_ref, w_ref, cq_ref, s1q_ref, s2q_ref, ckk_ref, s1kk_ref, s2kk_ref,
                         qz_ref, cmp_ref, sel_ref, win_ref, kvs_ref, kvw_ref, gate_ref, rg_ref, hg_ref,
                         *, tiles_per_seq):
    tm = x_ref.shape[0]
    y = _rms(x_ref[...], g_ref[...]).astype(BF16)

    def mm(c0, n):
        return jnp.dot(y, w_ref[:, c0:c0 + n], preferred_element_type=F32)

    cq, s1q, s2q = cq_ref[...], s1q_ref[...], s2q_ref[...]
    ckk, s1kk, s2kk = ckk_ref[...], s1kk_ref[...], s2kk_ref[...]
    scale = HEAD_DIM ** -0.5
    for hq in range(NSA_HEADS):
        qh = _rope128(mm(_P_QZ + hq * LANE, LANE), cq, s1q, s2q) * scale
        qz_ref[:, hq * LANE:(hq + 1) * LANE] = qh.astype(BF16)
    cmp_ref[...] = mm(_P_CMP, 2 * KV_W)
    for base, ref in ((_P_SEL, sel_ref), (_P_WIN, win_ref)):
        ref[:, 0:LANE] = _rope128(mm(base, LANE), ckk, s1kk, s2kk)
        ref[:, LANE:2 * LANE] = mm(base + LANE, LANE)
    row0 = (pl.program_id(0) % tiles_per_seq) * tm
    blk = (row0 + lax.broadcasted_iota(jnp.int32, (tm, 1), 0)) // SEL_BLOCK
    onehot = jnp.where(lax.broadcasted_iota(jnp.int32, (1, LANE), 1) == blk, MASK_BIG, 0.0).astype(BF16)
    for g in range(NSA_KV_HEADS):
        kvs_ref[g, :, 0:LANE] = _rope128(mm(_P_KVS + g * LANE, LANE), cq, s1q, s2q).astype(BF16)
        kvs_ref[g, :, LANE:2 * LANE] = onehot
        kvw_ref[g] = _rope128(mm(_P_KVW + g * LANE, LANE), cq, s1q, s2q).astype(BF16)
        gate_ref[g] = jax.nn.sigmoid(mm(_P_GATE + g * LANE, LANE))
    rg_ref[...] = mm(_P_RG, 2 * RG_WIDTH)
    hg_ref[...] = mm(_P_HG, 4 * HG_WIDTH)


def in_proj_prompt(x, g, w_rel, tables, seq_len, tm=512):
    m, d = x.shape
    tm = _row_tile(seq_len, tm)
    tiles_per_seq = seq_len // tm
    G = NSA_KV_HEADS
    row = lambda i: (i, 0)
    grow = lambda i: (0, i, 0)
    tab = pl.BlockSpec((tm, LANE), lambda i: (i % tiles_per_seq, 0))
    return pl.pallas_call(
        functools.partial(_in_proj_prompt_body, tiles_per_seq=tiles_per_seq),
        grid=(m // tm,),
        in_specs=[pl.BlockSpec((tm, d), row), pl.BlockSpec((1, d), lambda i: (0, 0)),
                  pl.BlockSpec((d, _P_END), lambda i: (0, 0))] + [tab] * 6,
        out_specs=[
            pl.BlockSpec((tm, NSA_HEADS * LANE), row),
            pl.BlockSpec((tm, 2 * KV_W), row), pl.BlockSpec((tm, 2 * KV_W), row), pl.BlockSpec((tm, 2 * KV_W), row),
            pl.BlockSpec((G, tm, 2 * LANE), grow), pl.BlockSpec((G, tm, LANE), grow), pl.BlockSpec((G, tm, LANE), grow),
            pl.BlockSpec((tm, 2 * RG_WIDTH), row), pl.BlockSpec((tm, 4 * HG_WIDTH), row),
        ],
        out_shape=[
            jax.ShapeDtypeStruct((m, NSA_HEADS * LANE), BF16),
            jax.ShapeDtypeStruct((m, 2 * KV_W), F32), jax.ShapeDtypeStruct((m, 2 * KV_W), F32),
            jax.ShapeDtypeStruct((m, 2 * KV_W), F32),
            jax.ShapeDtypeStruct((G, m, 2 * LANE), BF16), jax.ShapeDtypeStruct((G, m, LANE), BF16),
            jax.ShapeDtypeStruct((G, m, LANE), F32),
            jax.ShapeDtypeStruct((m, 2 * RG_WIDTH), F32), jax.ShapeDtypeStruct((m, 4 * HG_WIDTH), F32),
        ],
        compiler_params=pltpu.CompilerParams(
            dimension_semantics=("arbitrary",), vmem_limit_bytes=VMEM_LIMIT),
        name="in_proj_prompt",
    )(x, g.reshape(1, d), w_rel, *tables)


def compress_weights(cmp_pos, cmp_w1, cmp_b1, cmp_w2, cmp_b2):
    L = cmp_w1.shape[0]
    G, S, HD, CH = NSA_KV_HEADS, CMP_STRIDE, HEAD_DIM, CMP_HIDDEN
    w1 = cmp_w1.reshape(L, 2, CMP_RATIO, S, HD, CH)
    z = jnp.zeros_like(w1)
    w1bd = jnp.concatenate([jnp.concatenate([w1, z], axis=-1), jnp.concatenate([z, w1], axis=-1)], axis=-2)
    pe = jnp.tile(cmp_pos.reshape(L, 2, CMP_RATIO, S, HD), (1, 1, 1, 1, G))
    b1 = jnp.tile(cmp_b1, (1, 1, G)).reshape(L, 2, 1, G * CH)
    zc = jnp.zeros((L, CH, HD), F32)
    w2 = jnp.stack([jnp.concatenate([cmp_w2[:, 0], zc], axis=-1), jnp.concatenate([zc, cmp_w2[:, 1]], axis=-1)], axis=1)
    b2 = jnp.concatenate([cmp_b2[:, 0], cmp_b2[:, 1]], axis=-1).reshape(L, 1, 2 * HD)
    return w1bd.astype(BF16), pe, b1, w2.astype(BF16), b2


def _compress_body(xk_ref, xv_ref, pe_ref, w1_ref, b1_ref, w2_ref, b2_ref, c_ref, s1_ref, s2_ref, o_ref, *, nch, n_cmp):
    gelus = []
    for kv, x_ref in enumerate((xk_ref, xv_ref)):
        acc = [jnp.zeros((nch, NSA_KV_HEADS * CMP_HIDDEN), F32) for _ in range(CMP_RATIO)]
        for s in range(CMP_STRIDE):
            xs = x_ref[pl.ds(s, nch, stride=CMP_STRIDE), :]
            for r in range(CMP_RATIO):
                acc[r] = acc[r] + jnp.dot((xs + pe_ref[kv, r, s:s + 1, :]).astype(BF16), w1_ref[kv, r, s],
                                          preferred_element_type=F32)
        h = b1_ref[kv] + acc[0] + pltpu.roll(acc[1], nch - 1, 0)
        gelus.append(jax.nn.gelu(h).astype(BF16))
    valid = lax.broadcasted_iota(jnp.int32, (nch, 1), 0) < n_cmp
    for g in range(NSA_KV_HEADS):
        cols = slice(g * CMP_HIDDEN, (g + 1) * CMP_HIDDEN)
        out = (jnp.dot(gelus[0][:, cols], w2_ref[0], preferred_element_type=F32)
               + jnp.dot(gelus[1][:, cols], w2_ref[1], preferred_element_type=F32) + b2_ref[...])
        out = _rope128(out, c_ref[...], s1_ref[...], s2_ref[...])
        o_ref[g] = jnp.where(valid, out, 0.0).astype(BF16)


def compress(raw, cw, tables_q, seq_len):
    w1bd, pe, b1, w2, b2 = cw
    m = raw.shape[0]
    nb = m // seq_len
    nch = seq_len // CMP_STRIDE
    n_cmp = nch - CMP_RATIO + 1
    full = lambda shape: pl.BlockSpec(shape, lambda b: (0,) * len(shape))
    return pl.pallas_call(
        functools.partial(_compress_body, nch=nch, n_cmp=n_cmp),
        grid=(nb,),
        in_specs=[pl.BlockSpec((seq_len, KV_W), lambda b: (b, 0)), pl.BlockSpec((seq_len, KV_W), lambda b: (b, 1)),
                  full(pe.shape), full(w1bd.shape), full(b1.shape), full(w2.shape), full(b2.shape),
                  full((nch, LANE)), full((nch, LANE)), full((nch, LANE))],
        out_specs=pl.BlockSpec((None, NSA_KV_HEADS, nch, LANE), lambda b: (b, 0, 0, 0)),
        out_shape=jax.ShapeDtypeStruct((nb, NSA_KV_HEADS, nch, LANE), BF16),
        compiler_params=pltpu.CompilerParams(
            dimension_semantics=("arbitrary",), vmem_limit_bytes=VMEM_LIMIT),
        name="compress_kv",
    )(raw, raw, pe, w1bd, b1, w2, b2, *tables_q)


_NT = (((1,), (1,)), ((), ()))


def _masked_softmax(s, mask):
    s = jnp.where(mask, s, -1e30)
    m = jnp.max(s, axis=-1, keepdims=True)
    e = jnp.where(mask, jnp.exp(s - m), 0.0)
    return e / jnp.maximum(jnp.sum(e, axis=-1, keepdims=True), 1e-30)


def _nsa_prompt_body(qz_ref, gate_ref, kvc_ref, smap_ref, kvs_ref, kvw_ref, o_ref, *, tk):
    QB, H = Q_BLOCK, NSA_HPG
    q0 = pl.program_id(2) * QB
    q4 = jnp.concatenate([qz_ref[:, h * LANE:(h + 1) * LANE] for h in range(H)], axis=0)
    t_col = q0 + (lax.broadcasted_iota(jnp.int32, (H * QB, 1), 0) & (QB - 1))

    kvc = kvc_ref[...]
    ncp = kvc.shape[0]
    s_c = lax.dot_general(q4, kvc, _NT, preferred_element_type=F32)
    c_end = lax.broadcasted_iota(jnp.int32, (1, ncp), 1) * CMP_STRIDE + (CMP_BLOCK - 1)
    p_c = _masked_softmax(s_c, c_end <= t_col)
    o_c = jnp.dot(p_c.astype(BF16), kvc, preferred_element_type=F32)

    psum = p_c[0:QB]
    for h in range(1, H):
        psum = psum + p_c[h * QB:(h + 1) * QB]
    hi = psum.astype(BF16)
    lo = (psum - hi.astype(F32)).astype(BF16)
    smap = smap_ref[...]
    imp = jnp.dot(hi, smap, preferred_element_type=F32) + jnp.dot(lo, smap, preferred_element_type=F32)

    jf = lax.broadcasted_iota(jnp.int32, (QB, LANE), 1)
    t_tok = q0 + lax.broadcasted_iota(jnp.int32, (QB, 1), 0)
    cur = t_tok // SEL_BLOCK
    valid = jf * SEL_BLOCK <= t_tok
    forced = (jf == 0) | (jf == cur) | (jf == cur - 1)
    score = jnp.where(valid, jnp.where(forced, FORCE_SCORE, imp), -FORCE_SCORE)
    jff = jf.astype(F32)

    def pick_one(_, carry):
        score, chosen = carry
        best = jnp.max(score, axis=1, keepdims=True)
        first = jnp.min(jnp.where(score == best, jff, float(LANE)), axis=1, keepdims=True)
        hit = jff == first
        return jnp.where(hit, -jnp.inf, score), jnp.where(hit, 1.0, chosen)

    _, chosen = lax.fori_loop(0, SEL_TOPK, pick_one, (score, jnp.zeros((QB, LANE), F32)))
    selm1 = (jnp.where(valid, chosen, 0.0) - 1.0).astype(BF16)
    q_aug = jnp.concatenate([q4, jnp.concatenate([selm1] * H, axis=0)], axis=1)

    def tile_step(kt, carry, causal):
        m, l, acc = carry
        start = pl.multiple_of(kt * tk, tk)
        tile = kvs_ref[pl.ds(start, tk), :]
        s = lax.dot_general(q_aug, tile, _NT, preferred_element_type=F32)
        if causal:
            kpos = start + lax.broadcasted_iota(jnp.int32, (1, tk), 1)
            s = jnp.where(kpos <= t_col, s, -1e30)
        m_new = jnp.maximum(m, jnp.max(s, axis=1, keepdims=True))
        alpha = jnp.exp(m - m_new)
        p = jnp.exp(s - m_new)
        l = alpha * l + jnp.sum(p, axis=1, keepdims=True)
        acc = alpha * acc + jnp.dot(p.astype(BF16), tile, preferred_element_type=F32)
        return m_new, l, acc

    n_full = q0 // tk
    init = (jnp.full((H * QB, 1), -1e30, F32), jnp.zeros((H * QB, 1), F32), jnp.zeros((H * QB, 2 * LANE), F32))
    carry = lax.fori_loop(0, n_full, functools.partial(tile_step, causal=False), init)
    _, l_s, acc_s = tile_step(n_full, carry, True)
    o_s = acc_s[:, 0:LANE] / l_s

    wlen = WINDOW + QB
    w0 = pl.multiple_of(jnp.maximum(q0 - WINDOW, 0), QB)
    wt = kvw_ref[pl.ds(w0, wlen), :]
    s_w = lax.dot_general(q4, wt, _NT, preferred_element_type=F32)
    dist = t_col - (w0 + lax.broadcasted_iota(jnp.int32, (1, wlen), 1))
    p_w = _masked_softmax(s_w, (dist >= 0) & (dist < WINDOW))
    o_w = jnp.dot(p_w.astype(BF16), wt, preferred_element_type=F32)

    comb = []
    for h in range(H):
        r = slice(h * QB, (h + 1) * QB)
        comb.append(gate_ref[:, 3 * h:3 * h + 1] * o_c[r] + gate_ref[:, 3 * h + 1:3 * h + 2] * o_s[r]
                    + gate_ref[:, 3 * h + 2:3 * h + 3] * o_w[r])
    low = lax.broadcasted_iota(jnp.int32, (1, LANE), 1) < HEAD_DIM
    for p in range(H // 2):
        o_ref[:, p * LANE:(p + 1) * LANE] = jnp.where(low, pltpu.roll(comb[2 * p], HEAD_DIM, 1), comb[2 * p + 1])


def nsa_prompt(qz, gate, kvc, kvs, kvw, nb, seq_len, tk=512):
    m = qz.shape[0]
    G, QB = NSA_KV_HEADS, Q_BLOCK
    nqb = seq_len // QB
    ncp = kvc.shape[2]
    assert seq_len % tk == 0 and seq_len >= WINDOW + QB and seq_len // SEL_BLOCK <= LANE
    import numpy as np
    c0 = np.arange(ncp)[:, None] * CMP_STRIDE
    s0 = np.arange(LANE)[None, :] * SEL_BLOCK
    n_cmp = ncp - CMP_RATIO + 1
    ov = (c0 < s0 + SEL_BLOCK) & (c0 + CMP_BLOCK > s0) & (np.arange(ncp)[:, None] < n_cmp)
    smap = jnp.asarray(ov, BF16)
    return pl.pallas_call(
        functools.partial(_nsa_prompt_body, tk=tk),
        grid=(nb, G, nqb),
        in_specs=[
            pl.BlockSpec((QB, NSA_HPG * LANE), lambda b, g, q: (b * nqb + q, g)),
            pl.BlockSpec((None, QB, LANE), lambda b, g, q: (g, b * nqb + q, 0)),
            pl.BlockSpec((None, None, ncp, LANE), lambda b, g, q: (b, g, 0, 0)),
            pl.BlockSpec((ncp, LANE), lambda b, g, q: (0, 0)),
            pl.BlockSpec((None, seq_len, 2 * LANE), lambda b, g, q: (g, b, 0)),
            pl.BlockSpec((None, seq_len, LANE), lambda b, g, q: (g, b, 0)),
        ],
        out_specs=pl.BlockSpec((QB, NSA_HPG * HEAD_DIM), lambda b, g, q: (b * nqb + q, g)),
        out_shape=jax.ShapeDtypeStruct((m, NSA_WIDTH), F32),
        compiler_params=pltpu.CompilerParams(
            dimension_semantics=("arbitrary", "arbitrary", "arbitrary"), vmem_limit_bytes=VMEM_LIMIT),
        name="nsa_prompt",
    )(qz, gate, kvc, smap, kvs, kvw)


def _shift_rows(x, prev8, d):
    rolled = pltpu.roll(x, d, 0)
    row8 = lax.broadcasted_iota(jnp.int32, (8, 1), 0)
    first = jnp.where(row8 < d, pltpu.roll(prev8, d, 0), rolled[0:8])
    return jnp.concatenate([first, rolled[8:]], axis=0)


def _rglru_body(rg_ref, buf_ref, h0_ref, cw_ref, cb_ref, wg_ref, bg_ref, lam_ref,
                o_ref, hout_ref, bufout_ref, tail_ref, h_ref):
    tm = rg_ref.shape[0]
    W = RG_WIDTH

    @pl.when(pl.program_id(1) == 0)
    def _():
        tail_ref[...] = buf_ref[...]
        h_ref[...] = h0_ref[...]

    x = rg_ref[:, 0:W]
    gr = rg_ref[:, W:2 * W]
    prev8 = tail_ref[...]
    xc = cb_ref[...] + cw_ref[CONV_W - 1:CONV_W, :] * x
    for d in range(1, CONV_W):
        xc = xc + cw_ref[CONV_W - 1 - d:CONV_W - d, :] * _shift_rows(x, prev8, d)
    gates = jnp.dot(xc.astype(BF16), wg_ref[...], preferred_element_type=F32) + bg_ref[...]
    r = jax.nn.sigmoid(gates[:, 0:W])
    i = jax.nn.sigmoid(gates[:, W:2 * W])
    log_a = -RG_C * r * jax.nn.softplus(-lam_ref[...])
    a = jnp.exp(log_a)
    th = jnp.tanh(log_a)
    b = jnp.sqrt(-2.0 * th / (1.0 - th)) * (i * xc)
    row = lax.broadcasted_iota(jnp.int32, (tm, 1), 0)
    d = 1
    while d < tm:
        keep = row >= d
        a_sh = jnp.where(keep, pltpu.roll(a, d, 0), 1.0)
        b_sh = jnp.where(keep, pltpu.roll(b, d, 0), 0.0)
        b = a * b_sh + b
        a = a * a_sh
        d *= 2
    h = a * h_ref[7:8, :] + b
    o_ref[...] = h * jax.nn.gelu(gr)
    h_ref[...] = h[tm - 8:tm]
    tail_ref[...] = x[tm - 8:tm]
    hout_ref[...] = h[tm - 8:tm]
    bufout_ref[...] = x[tm - 8:tm]


def rglru_prompt(rg, conv_buf, h0, conv_w, conv_b, wa, ba, wx, bx, lam, seq_len, tm=256):
    m = rg.shape[0]
    nb = m // seq_len
    W = RG_WIDTH
    tm = _row_tile(seq_len, tm)
    nt = seq_len // tm
    bd = jax.scipy.linalg.block_diag
    wg = jnp.concatenate([bd(*[wa[n] for n in range(RG_BLOCKS)]), bd(*[wx[n] for n in range(RG_BLOCKS)])], axis=1).astype(BF16)
    bg = jnp.concatenate([ba, bx]).reshape(1, 2 * W)
    buf8 = jnp.pad(conv_buf, ((0, 0), (8 - (CONV_W - 1), 0), (0, 0)))
    h08 = jnp.broadcast_to(h0[:, None, :], (nb, 8, W))
    const = lambda shape: pl.BlockSpec(shape, lambda b, t: (0,) * len(shape))
    per_b = pl.BlockSpec((None, 8, W), lambda b, t: (b, 0, 0))
    out, h8, nbuf8 = pl.pallas_call(
        _rglru_body,
        grid=(nb, nt),
        in_specs=[pl.BlockSpec((tm, 2 * W), lambda b, t: (b * nt + t, 0)), per_b, per_b,
                  const((CONV_W, W)), const((1, W)), const((W, 2 * W)), const((1, 2 * W)), const((1, W))],
        out_specs=[pl.BlockSpec((tm, W), lambda b, t: (b * nt + t, 0)), per_b, per_b],
        out_shape=[jax.ShapeDtypeStruct((m, W), F32), jax.ShapeDtypeStruct((nb, 8, W), F32),
                   jax.ShapeDtypeStruct((nb, 8, W), F32)],
        scratch_shapes=[pltpu.VMEM((8, W), F32), pltpu.VMEM((8, W), F32)],
        compiler_params=pltpu.CompilerParams(dimension_semantics=("arbitrary", "arbitrary")),
        name="rglru_prompt",
    )(rg, buf8, h08, conv_w, conv_b.reshape(1, W), wg, bg, lam.reshape(1, W))
    return out, h8[:, 7], nbuf8[:, 8 - (CONV_W - 1):]


def _split_dot(x, w_bf16, pieces):
    acc = None
    for _ in range(pieces):
        xb = x.astype(BF16)
        part = jnp.dot(xb, w_bf16, preferred_element_type=F32)
        acc = part if acc is None else acc + part
        x = x - xb.astype(F32)
    return acc


def _hgrn_body(hg_ref, s0_ref, lb_ref, gain_ref, tri_ref, ones_ref, o_ref, sout_ref, st_ref, *, chunk):
    tm = hg_ref.shape[0]
    W = HG_WIDTH
    C = chunk

    @pl.when(pl.program_id(1) == 0)
    def _():
        st_ref[...] = s0_ref[...]

    lb = lb_ref[...]
    ones_blk = ones_ref[...]
    tri = tri_ref[...]
    same_head = ones_blk > 0
    rowc = lax.broadcasted_iota(jnp.int32, (C, 1), 0)

    def chunk_step(c, _):
        r0 = pl.multiple_of(c * C, C)
        rows = pl.ds(r0, C)
        q = jax.nn.silu(hg_ref[rows, 0:W])
        f = lb + (1.0 - lb) * jax.nn.sigmoid(hg_ref[rows, W:2 * W])
        v = hg_ref[rows, 2 * W:3 * W]
        k = 1.0 - f
        bcum = _split_dot_left(tri, jnp.log(f))
        st = st_ref[...]
        o = lax.dot_general((q * jnp.exp(bcum)).astype(BF16), st.astype(BF16), _NT, preferred_element_type=F32)

        def offset_step(dlt, carry):
            o, b_sh, k_sh, v_sh = carry
            p = jnp.where(rowc >= dlt, q * k_sh * jnp.exp(bcum - b_sh), 0.0)
            o = o + _split_dot(p, ones_blk, 2) * v_sh
            return o, pltpu.roll(b_sh, 1, 0), pltpu.roll(k_sh, 1, 0), pltpu.roll(v_sh, 1, 0)

        o, _, _, _ = lax.fori_loop(0, C, offset_step, (o, bcum, k, v))
        b_last = bcum[C - 1:C, :]
        k_hat = (k * jnp.exp(b_last - bcum)).astype(BF16)
        upd = lax.dot_general(v.astype(BF16), k_hat, (((0,), (0,)), ((), ())), preferred_element_type=F32)
        st_ref[...] = jnp.exp(b_last) * st + jnp.where(same_head, upd, 0.0)
        ms = _split_dot(o * o, ones_blk, 2) * (1.0 / HG_DV)
        o = o * lax.rsqrt(ms + EPS) * gain_ref[...]
        o_ref[rows, :] = o * jax.nn.silu(hg_ref[rows, 3 * W:4 * W])
        return 0

    lax.fori_loop(0, tm // C, chunk_step, 0)
    sout_ref[...] = st_ref[...]


def _split_dot_left(w_bf16, x):
    acc = None
    for _ in range(3):
        xb = x.astype(BF16)
        part = jnp.dot(w_bf16, xb, preferred_element_type=F32)
        acc = part if acc is None else acc + part
        x = x - xb.astype(F32)
    return acc


def hgrn_prompt(hg, s0, lb, gain, seq_len, tm=256, chunk=HG_CHUNK):
    import numpy as np
    m = hg.shape[0]
    nb = m // seq_len
    W, H = HG_WIDTH, HG_HEADS
    tm = _row_tile(seq_len, tm)
    nt = seq_len // tm
    head = np.arange(W) // HG_DV
    ones_blk = jnp.asarray(head[:, None] == head[None, :], BF16)
    tri = jnp.asarray(np.tril(np.ones((chunk, chunk))), BF16)
    s0_t = jnp.einsum('bhkv,hg->bhvgk', s0, jnp.eye(H, dtype=s0.dtype)).reshape(nb, W, W)
    const = lambda shape: pl.BlockSpec(shape, lambda b, t: (0,) * len(shape))
    per_b = pl.BlockSpec((None, W, W), lambda b, t: (b, 0, 0))
    out, s_t = pl.pallas_call(
        functools.partial(_hgrn_body, chunk=chunk),
        grid=(nb, nt),
        in_specs=[pl.BlockSpec((tm, 4 * W), lambda b, t: (b * nt + t, 0)), per_b,
                  const((1, W)), const((1, W)), const((chunk, chunk)), const((W, W))],
        out_specs=[pl.BlockSpec((tm, W), lambda b, t: (b * nt + t, 0)), per_b],
        out_shape=[jax.ShapeDtypeStruct((m, W), F32), jax.ShapeDtypeStruct((nb, W, W), F32)],
        scratch_shapes=[pltpu.VMEM((W, W), F32)],
        compiler_params=pltpu.CompilerParams(dimension_semantics=("arbitrary", "arbitrary")),
        name="hgrn_prompt",
    )(hg, s0_t, lb.reshape(1, W), gain.reshape(1, W), tri, ones_blk)
    s5 = s_t.reshape(nb, H, HG_DV, H, HG_DK)
    s_fin = jnp.stack([s5[:, h, :, h, :] for h in range(H)], axis=1).transpose(0, 1, 3, 2)
    return out, s_fin


def _split(a, sizes):
    out, o = [], 0
    for s in sizes:
        out.append(a[..., o:o + s])
        o += s
    return out


def rms_norm(x, g):
    x32 = x.astype(jnp.float32)
    y = x32 * lax.rsqrt(jnp.mean(x32 * x32, axis=-1, keepdims=True) + EPS)
    return (y * g.astype(jnp.float32)).astype(x.dtype)


def partial_rope(x, pos):
    half = ROPE_DIM // 2
    inv = ROPE_THETA ** (-jnp.arange(half, dtype=jnp.float32) * 2.0 / ROPE_DIM)
    ang = pos.astype(jnp.float32)[:, None] * inv
    cos = jnp.cos(ang)[:, None, :].astype(x.dtype)
    sin = jnp.sin(ang)[:, None, :].astype(x.dtype)
    x1, x2, rest = x[..., :half], x[..., half:ROPE_DIM], x[..., ROPE_DIM:]
    return jnp.concatenate([x1 * cos - x2 * sin, x2 * cos + x1 * sin, rest], axis=-1)


def masked_softmax(s, mask):
    s = jnp.where(mask, s, -1e30)
    m = jnp.max(s, axis=-1, keepdims=True)
    e = jnp.where(mask, jnp.exp(s - m), 0.0)
    return e / jnp.maximum(jnp.sum(e, axis=-1, keepdims=True), 1e-30)


def compress_kv(raw, pos_emb, w1, b1, w2, b2):
    B, L = raw.shape[0], raw.shape[1]
    nch = L // CMP_STRIDE
    n_cmp = nch - CMP_RATIO + 1
    chunks = raw[:, :nch * CMP_STRIDE].reshape(B, nch, CMP_STRIDE, NSA_KV_HEADS, HEAD_DIM)
    h = b1
    for r in range(CMP_RATIO):
        sl = slice(r * CMP_STRIDE, (r + 1) * CMP_STRIDE)
        pre = jnp.einsum('bnsgd,sdh->bngh', chunks + pos_emb[sl][:, None, :], w1[sl])
        h = h + pre[:, r:r + n_cmp]
    out = jax.nn.gelu(h) @ w2 + b2
    end = jnp.arange(n_cmp, dtype=jnp.int32) * CMP_STRIDE + CMP_BLOCK - 1
    return out, end


def selection_map(n_cmp, n_sel):
    c0 = jnp.arange(n_cmp) * CMP_STRIDE
    s0 = jnp.arange(n_sel) * SEL_BLOCK
    ov = (c0[:, None] < s0[None, :] + SEL_BLOCK) & (c0[:, None] + CMP_BLOCK > s0[None, :])
    return ov.astype(jnp.float32)


def nsa_attend_block(q, gate, t_pos, kw, vw, w_pos, k_cmp, v_cmp, c_end, ks_blk, vs_blk, smap):
    f32 = jnp.float32
    scale = HEAD_DIM ** -0.5
    s_c = jnp.einsum('bqghd,bngd->bqghn', q, k_cmp).astype(f32) * scale
    m_c = (c_end[None, :] <= t_pos[:, None])[None, :, None, None, :]
    p_c = masked_softmax(s_c, m_c)
    o_c = jnp.einsum('bqghn,bngd->bqghd', p_c.astype(v_cmp.dtype), v_cmp)
    imp = jnp.einsum('bqghn,nj->bqgj', p_c, smap)
    n_sel = ks_blk.shape[1]
    j = jnp.arange(n_sel)[None, :]
    cur = (t_pos // SEL_BLOCK)[:, None]
    valid = j * SEL_BLOCK <= t_pos[:, None]
    forced = (j == 0) | (j == cur) | (j == cur - 1)
    score = jnp.where(valid[None, :, None, :], jnp.where(forced[None, :, None, :], FORCE_SCORE, imp), -FORCE_SCORE)
    _, idx = lax.top_k(score, min(SEL_TOPK, n_sel))
    bi = jnp.arange(q.shape[0])[:, None, None, None]
    gi = jnp.arange(NSA_KV_HEADS)[None, None, :, None]
    kb = ks_blk[bi, idx, gi]
    vb = vs_blk[bi, idx, gi]
    kpos = idx[..., None] * SEL_BLOCK + jnp.arange(SEL_BLOCK)
    m_s = (kpos <= t_pos[None, :, None, None, None])[:, :, :, None]
    s_s = jnp.einsum('bqghd,bqgkld->bqghkl', q, kb).astype(f32) * scale
    shp = s_s.shape
    p_s = masked_softmax(s_s.reshape(*shp[:4], -1), m_s.reshape(*m_s.shape[:4], -1)).reshape(shp)
    o_s = jnp.einsum('bqghkl,bqgkld->bqghd', p_s.astype(vb.dtype), vb)
    s_w = jnp.einsum('bqghd,bwgd->bqghw', q, kw).astype(f32) * scale
    dist = t_pos[:, None] - w_pos[None, :]
    m_w = ((dist >= 0) & (dist < WINDOW) & (w_pos[None, :] >= 0))[None, :, None, None, :]
    p_w = masked_softmax(s_w, m_w)
    o_w = jnp.einsum('bqghw,bwgd->bqghd', p_w.astype(vw.dtype), vw)
    return gate[..., 0:1] * o_c + gate[..., 1:2] * o_s + gate[..., 2:3] * o_w


def nsa_mixer(q, kvs, gate, pos, cmp_pos, cmp_w1, cmp_b1, cmp_w2, cmp_b2, past):
    B, T = q.shape[:2]
    G, HD = NSA_KV_HEADS, HEAD_DIM
    q = partial_rope(q.reshape(B, T, NSA_HEADS, HD), pos).reshape(B, T, G, NSA_HPG, HD)
    kc, vc, ks, vs, kw, vw = [a.reshape(B, T, G, HD) for a in _split(kvs, (KV_W,) * 6)]
    ks = partial_rope(ks, pos)
    kw = partial_rope(kw, pos)
    gate = jax.nn.sigmoid(gate.reshape(B, T, G, NSA_HPG, 3))
    new_cmp = jnp.stack([kc, vc], axis=2)
    new_sel = jnp.stack([ks, vs], axis=2)
    new_win = jnp.stack([kw, vw], axis=2)
    if past is None:
        cmp_all, sel_all = new_cmp, new_sel
        win_all = jnp.pad(new_win, ((0, 0), (WINDOW, 0), (0, 0), (0, 0), (0, 0)))
        w_pos = pos[0] - WINDOW + jnp.arange(T + WINDOW, dtype=jnp.int32)
        win_state = new_win[:, T - min(WINDOW, T):]
    else:
        cmp_past, sel_past, win_buf = past
        cmp_all = jnp.concatenate([cmp_past, new_cmp], axis=1)
        sel_all = jnp.concatenate([sel_past, new_sel], axis=1)
        win_all = jnp.concatenate([win_buf, new_win], axis=1)
        nb = win_buf.shape[1]
        w_pos = pos[0] - nb + jnp.arange(nb + T, dtype=jnp.int32)
        win_state = win_all[:, T:]
    L = cmp_all.shape[1]
    k_cmp, c_end = compress_kv(cmp_all[:, :, 0], cmp_pos[0], cmp_w1[0], cmp_b1[0], cmp_w2[0], cmp_b2[0])
    v_cmp, _ = compress_kv(cmp_all[:, :, 1], cmp_pos[1], cmp_w1[1], cmp_b1[1], cmp_w2[1], cmp_b2[1])
    k_cmp = partial_rope(k_cmp, c_end)
    n_sel = -(-L // SEL_BLOCK)
    sel_pad = jnp.pad(sel_all, ((0, 0), (0, n_sel * SEL_BLOCK - L), (0, 0), (0, 0), (0, 0)))
    sel_blk = sel_pad.reshape(B, n_sel, SEL_BLOCK, 2, G, HD).transpose(3, 0, 1, 4, 2, 5)
    smap = selection_map(k_cmp.shape[1], n_sel)
    kw_all, vw_all = win_all[:, :, 0], win_all[:, :, 1]
    if past is None and T > Q_BLOCK and T % Q_BLOCK == 0:
        def one_block(n):
            q0 = n * Q_BLOCK
            return nsa_attend_block(
                lax.dynamic_slice_in_dim(q, q0, Q_BLOCK, axis=1),
                lax.dynamic_slice_in_dim(gate, q0, Q_BLOCK, axis=1),
                lax.dynamic_slice_in_dim(pos, q0, Q_BLOCK, axis=0),
                lax.dynamic_slice_in_dim(kw_all, q0, Q_BLOCK + WINDOW, axis=1),
                lax.dynamic_slice_in_dim(vw_all, q0, Q_BLOCK + WINDOW, axis=1),
                lax.dynamic_slice_in_dim(w_pos, q0, Q_BLOCK + WINDOW, axis=0),
                k_cmp, v_cmp, c_end, sel_blk[0], sel_blk[1], smap)
        o = lax.map(one_block, jnp.arange(T // Q_BLOCK))
        o = jnp.moveaxis(o, 0, 1).reshape(B, T, NSA_WIDTH)
    else:
        o = nsa_attend_block(q, gate, pos, kw_all, vw_all, w_pos, k_cmp, v_cmp, c_end,
                             sel_blk[0], sel_blk[1], smap).reshape(B, T, NSA_WIDTH)
    return o, new_cmp, new_sel, win_state


def _lin_combine(e1, e2):
    a1, b1 = e1
    a2, b2 = e2
    return a1 * a2, a2 * b1 + b2


def rglru_mixer(xr, gr, conv_buf, h0, conv_w, conv_b, wa, ba, wx, bx, lam):
    B, T = xr.shape[:2]
    f32 = jnp.float32
    xcat = jnp.concatenate([conv_buf.astype(xr.dtype), xr], axis=1)
    xc = conv_b + sum(conv_w[k] * xcat[:, k:k + T] for k in range(CONV_W))
    new_buf = xcat[:, T:]
    xb = xc.reshape(B, T, RG_BLOCKS, RG_BW)
    r = jax.nn.sigmoid(jnp.einsum('btnd,nde->btne', xb, wa).reshape(B, T, RG_WIDTH) + ba)
    i = jax.nn.sigmoid(jnp.einsum('btnd,nde->btne', xb, wx).reshape(B, T, RG_WIDTH) + bx)
    log_a = -RG_C * r.astype(f32) * jax.nn.softplus(-lam.astype(f32))
    a = jnp.exp(log_a)
    b = jnp.sqrt(-jnp.expm1(2.0 * log_a)) * (i * xc).astype(f32)
    b = b.at[:, 0].add(a[:, 0] * h0.astype(f32))
    _, h = lax.associative_scan(_lin_combine, (a, b), axis=1)
    out = h.astype(xr.dtype) * jax.nn.gelu(gr)
    return out, h[:, -1].astype(h0.dtype), new_buf


def gated_recurrence(q, k, v, logf, s0):
    B, T, H, DK = q.shape
    DV = v.shape[-1]
    C = HG_CHUNK if T % HG_CHUNK == 0 else T
    nc = T // C

    def to_chunks(a):
        return jnp.moveaxis(a.reshape(B, nc, C, *a.shape[2:]), 1, 0)

    tri = jnp.tril(jnp.ones((C, C), dtype=bool))[None, :, :, None, None]

    def step(S, inp):
        qc, kc, vc, gc = inp
        bcum = jnp.cumsum(gc, axis=1)
        o = jnp.einsum('bthk,bhkv->bthv', qc * jnp.exp(bcum), S)
        dec = jnp.exp(jnp.where(tri, bcum[:, :, None] - bcum[:, None, :], -jnp.inf))
        A = jnp.einsum('bthk,bshk,btshk->bhts', qc, kc, dec)
        o = o + jnp.einsum('bhts,bshv->bthv', A, vc)
        bl = bcum[:, -1]
        S = jnp.exp(bl)[..., None] * S + jnp.einsum('bshk,bshv->bhkv', kc * jnp.exp(bl[:, None] - bcum), vc)
        return S, o

    s_fin, o = lax.scan(step, s0, (to_chunks(q), to_chunks(k), to_chunks(v), to_chunks(logf)))
    return jnp.moveaxis(o, 0, 1).reshape(B, T, H, DV), s_fin


def hgrn2_mixer(hq, hf, hi, hg, s0, lb, gain):
    B, T = hq.shape[:2]
    f32 = jnp.float32
    q = jax.nn.silu(hq.astype(f32)).reshape(B, T, HG_HEADS, HG_DK)
    lbh = lb.reshape(HG_HEADS, HG_DK)
    f = lbh + (1.0 - lbh) * jax.nn.sigmoid(hf.astype(f32).reshape(B, T, HG_HEADS, HG_DK))
    v = hi.astype(f32).reshape(B, T, HG_HEADS, HG_DV)
    o, s_fin = gated_recurrence(q, 1.0 - f, v, jnp.log(f), s0.astype(f32))
    o = rms_norm(o, gain.reshape(HG_HEADS, HG_DV)).reshape(B, T, HG_WIDTH).astype(hq.dtype)
    return o * jax.nn.silu(hg), s_fin.astype(s0.dtype)


def forward_layer(x, pos, w, wb, past):
    (norm_mix, w_in, w_out, norm_ffn, w_up, w_down, cmp_pos, cmp_w1, cmp_b1, cmp_w2, cmp_b2,
     rg_conv_w, rg_conv_b, rg_wa, rg_ba, rg_wx, rg_bx, rg_lambda, hg_lb, hg_gain) = w
    w_in_b, w_out_b, w_up_b, w_down_b = wb
    B, T = x.shape[:2]
    x2 = x.reshape(B * T, D_MODEL)
    proj = norm_matmul(x2, norm_mix, w_in_b)[:, :D_IN].reshape(B, T, D_IN)
    q, kvs, gate, rg_x, rg_g, hg_q, hg_f, hg_i, hg_g = _split(proj, IN_SIZES)
    if past is None:
        nsa_past = None
        rg_buf = jnp.zeros((B, CONV_W - 1, RG_WIDTH), x.dtype)
        rg_h0 = jnp.zeros((B, RG_WIDTH), x.dtype)
        hg_s0 = jnp.zeros((B, HG_HEADS, HG_DK, HG_DV), x.dtype)
    else:
        cmp_past, sel_past, win_buf, rg_h0, rg_buf, hg_s0 = past
        nsa_past = (cmp_past, sel_past, win_buf)
    o_nsa, new_cmp, new_sel, new_win = nsa_mixer(q, kvs, gate, pos, cmp_pos, cmp_w1, cmp_b1, cmp_w2, cmp_b2, nsa_past)
    o_rg, new_h, new_buf = rglru_mixer(rg_x, rg_g, rg_buf, rg_h0, rg_conv_w, rg_conv_b, rg_wa, rg_ba, rg_wx, rg_bx, rg_lambda)
    o_hg, new_s = hgrn2_mixer(hg_q, hg_f, hg_i, hg_g, hg_s0, hg_lb, hg_gain)
    mix = jnp.concatenate([o_nsa, o_rg, o_hg], axis=-1).reshape(B * T, MIX_WIDTH)
    x2 = out_ffn(x2, mix, w_out_b, norm_ffn, w_up_b, w_down_b)
    return x2.reshape(B, T, D_MODEL), (new_cmp, new_sel, new_win, new_h, new_buf, new_s)


def forward_layer_prompt(x, w, wb, pw):
    (norm_mix, w_in, w_out, norm_ffn, w_up, w_down, cmp_pos, cmp_w1, cmp_b1, cmp_w2, cmp_b2,
     rg_conv_w, rg_conv_b, rg_wa, rg_ba, rg_wx, rg_bx, rg_lambda, hg_lb, hg_gain) = w
    _, w_out_b, w_up_b, w_down_b = wb
    w_rel, cw, tables, tables_cmp = pw
    B, T = x.shape[:2]
    x2 = x.reshape(B * T, D_MODEL)
    qz, cmp, sel, win, kvs, kvw, gate, rg, hg = in_proj_prompt(x2, norm_mix, w_rel, tables, T)
    kvc = compress(cmp, cw, tables_cmp, T)
    o_nsa = nsa_prompt(qz, gate, kvc, kvs, kvw, B, T)
    kv_shape = (B, T, 2, NSA_KV_HEADS, HEAD_DIM)
    new_cmp, new_sel = cmp.reshape(kv_shape), sel.reshape(kv_shape)
    new_win = win.reshape(kv_shape)[:, T - min(WINDOW, T):]
    rg_buf = jnp.zeros((B, CONV_W - 1, RG_WIDTH), x.dtype)
    rg_h0 = jnp.zeros((B, RG_WIDTH), x.dtype)
    hg_s0 = jnp.zeros((B, HG_HEADS, HG_DK, HG_DV), x.dtype)
    o_rg, new_h, new_buf = rglru_prompt(rg, rg_buf, rg_h0, rg_conv_w, rg_conv_b, rg_wa, rg_ba, rg_wx, rg_bx,
                                        rg_lambda, T)
    o_hg, new_s = hgrn_prompt(hg, hg_s0, hg_lb, hg_gain, T)
    mix = jnp.concatenate([o_nsa, o_rg, o_hg], axis=-1)
    x2 = out_ffn(x2, mix, w_out_b, norm_ffn, w_up_b, w_down_b)
    return x2.reshape(B, T, D_MODEL), (new_cmp, new_sel, new_win, new_h, new_buf, new_s)


def kernel(x_prompt, x_sample, cache_nsa_cmp_kv, cache_nsa_sel_kv, cache_nsa_win_kv, state_rglru_h,
           state_rglru_conv, state_hgrn_s, page_table, norm_mix, w_in, w_out, norm_ffn, w_up, w_down,
           cmp_pos, cmp_w1, cmp_b1, cmp_w2, cmp_b2, rg_conv_w, rg_conv_b, rg_wa, rg_ba, rg_wx, rg_bx,
           rg_lambda, hg_lower_bounds, hg_gain, final_norm):
    lb = jnp.cumsum(jax.nn.softmax(hg_lower_bounds.astype(jnp.float32), axis=0), axis=0)
    lb = lb - lb[0]
    past_len = page_table.shape[1] * PAGE_SIZE
    n_dec = x_sample.shape[0]
    pos_p = jnp.arange(x_prompt.shape[1], dtype=jnp.int32)
    pos_s = past_len + jnp.arange(x_sample.shape[1], dtype=jnp.int32)
    d_in_pad = _round_up(D_IN, LANE)
    w_in_b = jnp.pad(w_in, ((0, 0), (0, 0), (0, d_in_pad - D_IN))).astype(BF16)
    w_out_b = w_out.astype(BF16)
    w_up_b = w_up.astype(BF16)
    w_down_b = w_down.astype(BF16)
    w_rel = relayout_w_in(w_in)
    cws = compress_weights(cmp_pos, cmp_w1, cmp_b1, cmp_w2, cmp_b2)
    T = x_prompt.shape[1]
    tables = rope_tables(pos_p)
    c_end_p = jnp.arange(T // CMP_STRIDE, dtype=jnp.int32) * CMP_STRIDE + CMP_BLOCK - 1
    tables_cmp = rope_tables(c_end_p)[:3]
    xp, xs = x_prompt, x_sample
    st_p, st_s = [], []
    for l in range(DEPTH):
        w = (norm_mix[l], w_in[l], w_out[l], norm_ffn[l], w_up[l], w_down[l], cmp_pos[l], cmp_w1[l], cmp_b1[l],
             cmp_w2[l], cmp_b2[l], rg_conv_w[l], rg_conv_b[l], rg_wa[l], rg_ba[l], rg_wx[l], rg_bx[l],
             rg_lambda[l], lb[l], hg_gain[l])
        wb = (w_in_b[l], w_out_b[l], w_up_b[l], w_down_b[l])
        pw = (w_rel[l], tuple(a[l] for a in cws), tables, tables_cmp)
        xp, sp = forward_layer_prompt(xp, w, wb, pw)
        past = (cache_nsa_cmp_kv[l][page_table].reshape(n_dec, past_len, 2, NSA_KV_HEADS, HEAD_DIM),
                cache_nsa_sel_kv[l][page_table].reshape(n_dec, past_len, 2, NSA_KV_HEADS, HEAD_DIM),
                cache_nsa_win_kv[l], state_rglru_h[l], state_rglru_conv[l], state_hgrn_s[l])
        xs, ss = forward_layer(xs, pos_s, w, wb, past)
        st_p.append(sp)
        st_s.append(ss)

    def stack(sts, i):
        return jnp.stack([s[i] for s in sts], axis=0)

    y_prompt = final_rms(xp.reshape(-1, D_MODEL), final_norm).reshape(xp.shape)
    y_sample = final_rms(xs.reshape(-1, D_MODEL), final_norm).reshape(xs.shape)
    return (y_prompt, y_sample, stack(st_p, 0), stack(st_p, 1), stack(st_p, 2), stack(st_p, 3), stack(st_p, 4),
            stack(st_p, 5), stack(st_s, 0), stack(st_s, 1), stack(st_s, 2), stack(st_s, 3), stack(st_s, 4),
            stack(st_s, 5))
```

```python
import functools

import jax
import jax.numpy as jnp
from jax import lax
from jax.experimental import pallas as pl
from jax.experimental.pallas import tpu as pltpu

F32 = jnp.float32
BF16 = jnp.bfloat16

D_MODEL = 1024
DEPTH = 4
PAGE_SIZE = 128
HEAD_DIM = 64
NSA_HEADS = D_MODEL // (2 * HEAD_DIM)
NSA_KV_HEADS = 2
NSA_HPG = NSA_HEADS // NSA_KV_HEADS
NSA_WIDTH = NSA_HEADS * HEAD_DIM
KV_W = NSA_KV_HEADS * HEAD_DIM
CMP_BLOCK = 32
CMP_STRIDE = 16
CMP_RATIO = CMP_BLOCK // CMP_STRIDE
CMP_HIDDEN = 256
SEL_BLOCK = 64
SEL_TOPK = 16
WINDOW = 512
Q_BLOCK = 128
FORCE_SCORE = 1e6
ROPE_DIM = HEAD_DIM // 4
ROPE_THETA = 500000.0
RG_WIDTH = D_MODEL // 4
RG_BLOCKS = 4
RG_BW = RG_WIDTH // RG_BLOCKS
RG_C = 8.0
CONV_W = 4
HG_HEADS = 4
HG_DK = 64
HG_DV = D_MODEL // 4 // HG_HEADS
HG_WIDTH = HG_HEADS * HG_DV
HG_CHUNK = 64
MIX_WIDTH = NSA_WIDTH + RG_WIDTH + HG_WIDTH
D_FF = 4 * D_MODEL
EPS = 1e-6
IN_SIZES = (NSA_WIDTH, 6 * KV_W, 3 * NSA_HEADS, RG_WIDTH, RG_WIDTH, HG_HEADS * HG_DK, HG_HEADS * HG_DK, HG_WIDTH, HG_WIDTH)
D_IN = sum(IN_SIZES)

LANE = 128
VMEM_LIMIT = 56 * 1024 * 1024


def _round_up(n, m):
    return -(-n // m) * m


def _row_tile(m, want):
    t = min(m, want)
    while m % t:
        t //= 2
    return t


def _rms(x, g):
    return x * lax.rsqrt(jnp.mean(x * x, axis=-1, keepdims=True) + EPS) * g


def _norm_matmul_body(x_ref, g_ref, w_ref, o_ref):
    y = _rms(x_ref[...], g_ref[...]).astype(BF16)
    o_ref[...] = jnp.dot(y, w_ref[...], preferred_element_type=F32)


def norm_matmul(x, g, w_bf16, tm=512):
    m, d = x.shape
    n = w_bf16.shape[1]
    tm = _row_tile(m, tm)
    return pl.pallas_call(
        _norm_matmul_body,
        grid=(m // tm,),
        in_specs=[
            pl.BlockSpec((tm, d), lambda i: (i, 0)),
            pl.BlockSpec((1, d), lambda i: (0, 0)),
            pl.BlockSpec((d, n), lambda i: (0, 0)),
        ],
        out_specs=pl.BlockSpec((tm, n), lambda i: (i, 0)),
        out_shape=jax.ShapeDtypeStruct((m, n), F32),
        compiler_params=pltpu.CompilerParams(
            dimension_semantics=("arbitrary",), vmem_limit_bytes=VMEM_LIMIT),
        name="norm_in_proj",
    )(x, g.reshape(1, d), w_bf16)


def _out_ffn_body(x_ref, mix_ref, wo_ref, g_ref, wu_ref, wd_ref, o_ref, *, f_tile):
    x1 = x_ref[...] + jnp.dot(mix_ref[...].astype(BF16), wo_ref[...], preferred_element_type=F32)
    y = _rms(x1, g_ref[...]).astype(BF16)
    acc = x1
    for j in range(wu_ref.shape[1] // f_tile):
        h = jnp.maximum(jnp.dot(y, wu_ref[:, j * f_tile:(j + 1) * f_tile], preferred_element_type=F32), 0.0)
        acc = acc + jnp.dot((h * h).astype(BF16), wd_ref[j * f_tile:(j + 1) * f_tile, :], preferred_element_type=F32)
    o_ref[...] = acc


def out_ffn(x, mix, wo, g, wu, wd, tm=512, f_tile=512):
    m, d = x.shape
    dm = mix.shape[1]
    dff = wu.shape[1]
    tm = _row_tile(m, tm)
    const = lambda i: (0, 0)
    return pl.pallas_call(
        functools.partial(_out_ffn_body, f_tile=f_tile),
        grid=(m // tm,),
        in_specs=[
            pl.BlockSpec((tm, d), lambda i: (i, 0)),
            pl.BlockSpec((tm, dm), lambda i: (i, 0)),
            pl.BlockSpec((dm, d), const),
            pl.BlockSpec((1, d), const),
            pl.BlockSpec((d, dff), const),
            pl.BlockSpec((dff, d), const),
        ],
        out_specs=pl.BlockSpec((tm, d), lambda i: (i, 0)),
        out_shape=jax.ShapeDtypeStruct((m, d), F32),
        compiler_params=pltpu.CompilerParams(
            dimension_semantics=("arbitrary",), vmem_limit_bytes=VMEM_LIMIT),
        name="out_proj_ffn",
    )(x, mix, wo, g.reshape(1, d), wu, wd)


def _final_norm_body(x_ref, g_ref, o_ref):
    o_ref[...] = _rms(x_ref[...], g_ref[...])


def final_rms(x, g, tm=1024):
    m, d = x.shape
    tm = _row_tile(m, tm)
    return pl.pallas_call(
        _final_norm_body,
        grid=(m // tm,),
        in_specs=[pl.BlockSpec((tm, d), lambda i: (i, 0)), pl.BlockSpec((1, d), lambda i: (0, 0))],
        out_specs=pl.BlockSpec((tm, d), lambda i: (i, 0)),
        out_shape=jax.ShapeDtypeStruct((m, d), F32),
        name="final_norm",
    )(x, g.reshape(1, d))


MASK_BIG = 2.0 ** 100
LOG2E = 1.4426950408889634

_O_Q = 0
_O_KV = NSA_WIDTH
_O_GATE = _O_KV + 6 * KV_W
_O_RG = _O_GATE + 3 * NSA_HEADS
_O_HG = _O_RG + 2 * RG_WIDTH

_P_QZ = 0
_P_CMP = _P_QZ + NSA_HEADS * LANE
_P_SEL = _P_CMP + 2 * KV_W
_P_WIN = _P_SEL + 2 * KV_W
_P_KVS = _P_WIN + 2 * KV_W
_P_KVW = _P_KVS + 2 * KV_W
_P_GATE = _P_KVW + 2 * KV_W
_P_RG = _P_GATE + NSA_KV_HEADS * LANE
_P_HG = _P_RG + 2 * RG_WIDTH
_P_END = _P_HG + 4 * HG_WIDTH


def _in_proj_columns():
    import numpy as np
    src = -np.ones((_P_END,), np.int64)
    for hq in range(NSA_HEADS):
        src[_P_QZ + hq * LANE:_P_QZ + hq * LANE + HEAD_DIM] = _O_Q + hq * HEAD_DIM + np.arange(HEAD_DIM)
    src[_P_CMP:_P_CMP + 2 * KV_W] = _O_KV + np.arange(2 * KV_W)
    src[_P_SEL:_P_SEL + 2 * KV_W] = _O_KV + 2 * KV_W + np.arange(2 * KV_W)
    src[_P_WIN:_P_WIN + 2 * KV_W] = _O_KV + 4 * KV_W + np.arange(2 * KV_W)
    for base, off in ((_P_KVS, _O_KV + 2 * KV_W), (_P_KVW, _O_KV + 4 * KV_W)):
        for g in range(NSA_KV_HEADS):
            src[base + g * LANE:base + g * LANE + HEAD_DIM] = off + g * HEAD_DIM + np.arange(HEAD_DIM)
            src[base + g * LANE + HEAD_DIM:base + (g + 1) * LANE] = off + KV_W + g * HEAD_DIM + np.arange(HEAD_DIM)
    ng = 3 * NSA_HPG
    for g in range(NSA_KV_HEADS):
        src[_P_GATE + g * LANE:_P_GATE + g * LANE + ng] = _O_GATE + g * ng + np.arange(ng)
    src[_P_RG:_P_RG + 2 * RG_WIDTH] = _O_RG + np.arange(2 * RG_WIDTH)
    src[_P_HG:_P_HG + 4 * HG_WIDTH] = _O_HG + np.arange(4 * HG_WIDTH)
    return src


def relayout_w_in(w_in):
    import numpy as np
    src = _in_proj_columns()
    cols = jnp.take(w_in, jnp.asarray(np.maximum(src, 0), jnp.int32), axis=-1)
    return jnp.where(jnp.asarray(src >= 0), cols, 0.0).astype(BF16)


def rope_tables(pos):
    import numpy as np
    half = ROPE_DIM // 2
    inv = ROPE_THETA ** (-jnp.arange(half, dtype=F32) * 2.0 / ROPE_DIM)
    ang = pos.astype(F32)[:, None] * inv
    cos, sin = jnp.cos(ang), jnp.sin(ang)
    lane = np.arange(LANE)
    d = lane % HEAD_DIM
    idx = jnp.asarray(d % half, jnp.int32)
    cos_l, sin_l = jnp.take(cos, idx, axis=1), jnp.take(sin, idx, axis=1)
    out = []
    for active in (lane < ROPE_DIM, d < ROPE_DIM):
        first = jnp.asarray(active & (d < half))
        second = jnp.asarray(active & (d >= half) & (d < ROPE_DIM))
        out += [jnp.where(first | second, cos_l, 1.0), jnp.where(first, -sin_l, 0.0), jnp.where(second, sin_l, 0.0)]
    return tuple(out)


def _rope128(x, c, s1, s2):
    return x * c + pltpu.roll(x, LANE - ROPE_DIM // 2, 1) * s1 + pltpu.roll(x, ROPE_DIM // 2, 1) * s2


def _in_proj_prompt_body(x_ref, g_ref, w_ref, cq_ref, s1q_ref, s2q_ref, ckk_ref, s1kk_ref, s2kk_ref,
                         qz_ref, cmp_ref, sel_ref, win_ref, kvs_ref, kvw_ref, gate_ref, rg_ref, hg_ref,
                         *, tiles_per_seq):
    tm = x_ref.shape[0]
    y = _rms(x_ref[...], g_ref[...]).astype(BF16)

    def mm(c0, n):
        return jnp.dot(y, w_ref[:, c0:c0 + n], preferred_element_type=F32)

    cq, s1q, s2q = cq_ref[...], s1q_ref[...], s2q_ref[...]
    ckk, s1kk, s2kk = ckk_ref[...], s1kk_ref[...], s2kk_ref[...]
    scale = HEAD_DIM ** -0.5 * LOG2E
    for hq in range(NSA_HEADS):
        qh = _rope128(mm(_P_QZ + hq * LANE, LANE), cq, s1q, s2q) * scale
        qz_ref[:, hq * LANE:(hq + 1) * LANE] = qh.astype(BF16)
    cmp_ref[...] = mm(_P_CMP, 2 * KV_W)
    for base, ref in ((_P_SEL, sel_ref), (_P_WIN, win_ref)):
        ref[:, 0:LANE] = _rope128(mm(base, LANE), ckk, s1kk, s2kk)
        ref[:, LANE:2 * LANE] = mm(base + LANE, LANE)
    row0 = (pl.program_id(0) % tiles_per_seq) * tm
    blk = (row0 + lax.broadcasted_iota(jnp.int32, (tm, 1), 0)) // SEL_BLOCK
    onehot = jnp.where(lax.broadcasted_iota(jnp.int32, (1, LANE), 1) == blk, MASK_BIG, 0.0).astype(BF16)
    for g in range(NSA_KV_HEADS):
        kvs_ref[g, :, 0:LANE] = _rope128(mm(_P_KVS + g * LANE, LANE), cq, s1q, s2q).astype(BF16)
        kvs_ref[g, :, LANE:2 * LANE] = onehot
        kvw_ref[g] = _rope128(mm(_P_KVW + g * LANE, LANE), cq, s1q, s2q).astype(BF16)
        gate_ref[g] = jax.nn.sigmoid(mm(_P_GATE + g * LANE, LANE))
    rg_ref[...] = mm(_P_RG, 2 * RG_WIDTH)
    hg_ref[...] = mm(_P_HG, 4 * HG_WIDTH)


def in_proj_prompt(x, g, w_rel, tables, seq_len, tm=512):
    m, d = x.shape
    tm = _row_tile(seq_len, tm)
    tiles_per_seq = seq_len // tm
    G = NSA_KV_HEADS
    row = lambda i: (i, 0)
    grow = lambda i: (0, i, 0)
    tab = pl.BlockSpec((tm, LANE), lambda i: (i % tiles_per_seq, 0))
    return pl.pallas_call(
        functools.partial(_in_proj_prompt_body, tiles_per_seq=tiles_per_seq),
        grid=(m // tm,),
        in_specs=[pl.BlockSpec((tm, d), row), pl.BlockSpec((1, d), lambda i: (0, 0)),
                  pl.BlockSpec((d, _P_END), lambda i: (0, 0))] + [tab] * 6,
        out_specs=[
            pl.BlockSpec((tm, NSA_HEADS * LANE), row),
            pl.BlockSpec((tm, 2 * KV_W), row), pl.BlockSpec((tm, 2 * KV_W), row), pl.BlockSpec((tm, 2 * KV_W), row),
            pl.BlockSpec((G, tm, 2 * LANE), grow), pl.BlockSpec((G, tm, LANE), grow), pl.BlockSpec((G, tm, LANE), grow),
            pl.BlockSpec((tm, 2 * RG_WIDTH), row), pl.BlockSpec((tm, 4 * HG_WIDTH), row),
        ],
        out_shape=[
            jax.ShapeDtypeStruct((m, NSA_HEADS * LANE), BF16),
            jax.ShapeDtypeStruct((m, 2 * KV_W), F32), jax.ShapeDtypeStruct((m, 2 * KV_W), F32),
            jax.ShapeDtypeStruct((m, 2 * KV_W), F32),
            jax.ShapeDtypeStruct((G, m, 2 * LANE), BF16), jax.ShapeDtypeStruct((G, m, LANE), BF16),
            jax.ShapeDtypeStruct((G, m, LANE), F32),
            jax.ShapeDtypeStruct((m, 2 * RG_WIDTH), F32), jax.ShapeDtypeStruct((m, 4 * HG_WIDTH), F32),
        ],
        compiler_params=pltpu.CompilerParams(
            dimension_semantics=("arbitrary",), vmem_limit_bytes=VMEM_LIMIT),
        name="in_proj_prompt",
    )(x, g.reshape(1, d), w_rel, *tables)


def compress_weights(cmp_pos, cmp_w1, cmp_b1, cmp_w2, cmp_b2):
    L = cmp_w1.shape[0]
    G, S, HD, CH = NSA_KV_HEADS, CMP_STRIDE, HEAD_DIM, CMP_HIDDEN
    w1 = cmp_w1.reshape(L, 2, CMP_RATIO, S, HD, CH)
    z = jnp.zeros_like(w1)
    w1bd = jnp.concatenate([jnp.concatenate([w1, z], axis=-1), jnp.concatenate([z, w1], axis=-1)], axis=-2)
    pe = jnp.tile(cmp_pos.reshape(L, 2, CMP_RATIO, S, HD), (1, 1, 1, 1, G))
    b1 = jnp.tile(cmp_b1, (1, 1, G)).reshape(L, 2, 1, G * CH)
    zc = jnp.zeros((L, CH, HD), F32)
    w2 = jnp.stack([jnp.concatenate([cmp_w2[:, 0], zc], axis=-1), jnp.concatenate([zc, cmp_w2[:, 1]], axis=-1)], axis=1)
    b2 = jnp.concatenate([cmp_b2[:, 0], cmp_b2[:, 1]], axis=-1).reshape(L, 1, 2 * HD)
    return w1bd.astype(BF16), pe, b1, w2.astype(BF16), b2


def _compress_body(xk_ref, xv_ref, pe_ref, w1_ref, b1_ref, w2_ref, b2_ref, c_ref, s1_ref, s2_ref, o_ref, *, nch, n_cmp):
    gelus = []
    for kv, x_ref in enumerate((xk_ref, xv_ref)):
        acc = [jnp.zeros((nch, NSA_KV_HEADS * CMP_HIDDEN), F32) for _ in range(CMP_RATIO)]
        for s in range(CMP_STRIDE):
            xs = x_ref[pl.ds(s, nch, stride=CMP_STRIDE), :]
            for r in range(CMP_RATIO):
                acc[r] = acc[r] + jnp.dot((xs + pe_ref[kv, r, s:s + 1, :]).astype(BF16), w1_ref[kv, r, s],
                                          preferred_element_type=F32)
        h = b1_ref[kv] + acc[0] + pltpu.roll(acc[1], nch - 1, 0)
        gelus.append(jax.nn.gelu(h).astype(BF16))
    valid = lax.broadcasted_iota(jnp.int32, (nch, 1), 0) < n_cmp
    for g in range(NSA_KV_HEADS):
        cols = slice(g * CMP_HIDDEN, (g + 1) * CMP_HIDDEN)
        out = (jnp.dot(gelus[0][:, cols], w2_ref[0], preferred_element_type=F32)
               + jnp.dot(gelus[1][:, cols], w2_ref[1], preferred_element_type=F32) + b2_ref[...])
        out = _rope128(out, c_ref[...], s1_ref[...], s2_ref[...])
        o_ref[g] = jnp.where(valid, out, 0.0).astype(BF16)


def compress(raw, cw, tables_q, seq_len):
    w1bd, pe, b1, w2, b2 = cw
    m = raw.shape[0]
    nb = m // seq_len
    nch = seq_len // CMP_STRIDE
    n_cmp = nch - CMP_RATIO + 1
    full = lambda shape: pl.BlockSpec(shape, lambda b: (0,) * len(shape))
    return pl.pallas_call(
        functools.partial(_compress_body, nch=nch, n_cmp=n_cmp),
        grid=(nb,),
        in_specs=[pl.BlockSpec((seq_len, KV_W), lambda b: (b, 0)), pl.BlockSpec((seq_len, KV_W), lambda b: (b, 1)),
                  full(pe.shape), full(w1bd.shape), full(b1.shape), full(w2.shape), full(b2.shape),
                  full((nch, LANE)), full((nch, LANE)), full((nch, LANE))],
        out_specs=pl.BlockSpec((None, NSA_KV_HEADS, nch, LANE), lambda b: (b, 0, 0, 0)),
        out_shape=jax.ShapeDtypeStruct((nb, NSA_KV_HEADS, nch, LANE), BF16),
        compiler_params=pltpu.CompilerParams(
            dimension_semantics=("arbitrary",), vmem_limit_bytes=VMEM_LIMIT),
        name="compress_kv",
    )(raw, raw, pe, w1bd, b1, w2, b2, *tables_q)


_NT = (((1,), (1,)), ((), ()))


def _masked_softmax2(s, mask):
    s = jnp.where(mask, s, -1e30)
    m = jnp.max(s, axis=-1, keepdims=True)
    e = jnp.where(mask, jnp.exp2(s - m), 0.0)
    return e / jnp.maximum(jnp.sum(e, axis=-1, keepdims=True), 1e-30)


def _nsa_prompt_body(qz_ref, gate_ref, kvc_ref, smap_ref, kvs_ref, kvw_ref, o_ref, *, tk):
    QB, H = Q_BLOCK, NSA_HPG
    q0 = pl.program_id(2) * QB
    q4 = jnp.concatenate([qz_ref[:, h * LANE:(h + 1) * LANE] for h in range(H)], axis=0)
    t_col = q0 + (lax.broadcasted_iota(jnp.int32, (H * QB, 1), 0) & (QB - 1))

    kvc = kvc_ref[...]
    ncp = kvc.shape[0]
    s_c = lax.dot_general(q4, kvc, _NT, preferred_element_type=F32)
    c_end = lax.broadcasted_iota(jnp.int32, (1, ncp), 1) * CMP_STRIDE + (CMP_BLOCK - 1)
    p_c = _masked_softmax2(s_c, c_end <= t_col)
    o_c = jnp.dot(p_c.astype(BF16), kvc, preferred_element_type=F32)

    psum = p_c[0:QB]
    for h in range(1, H):
        psum = psum + p_c[h * QB:(h + 1) * QB]
    hi = psum.astype(BF16)
    lo = (psum - hi.astype(F32)).astype(BF16)
    smap = smap_ref[...]
    imp = jnp.dot(hi, smap, preferred_element_type=F32) + jnp.dot(lo, smap, preferred_element_type=F32)

    jf = lax.broadcasted_iota(jnp.int32, (QB, LANE), 1)
    t_tok = q0 + lax.broadcasted_iota(jnp.int32, (QB, 1), 0)
    cur = t_tok // SEL_BLOCK
    valid = jf * SEL_BLOCK <= t_tok
    forced = (jf == 0) | (jf == cur) | (jf == cur - 1)
    score = jnp.where(valid, jnp.where(forced, FORCE_SCORE, imp), -FORCE_SCORE)

    def pick_one(_, carry):
        score, chosen = carry
        hit = jf == jnp.argmax(score, axis=1, keepdims=True)
        return jnp.where(hit, -jnp.inf, score), jnp.where(hit, 1.0, chosen)

    _, chosen = lax.fori_loop(0, SEL_TOPK, pick_one, (score, jnp.zeros((QB, LANE), F32)), unroll=True)
    selm1 = (jnp.where(valid, chosen, 0.0) - 1.0).astype(BF16)
    q_aug = jnp.concatenate([q4, jnp.concatenate([selm1] * H, axis=0)], axis=1)

    NCHAIN = 1
    RC = H * QB // NCHAIN

    def tile_step(kt, carry, causal=False):
        tile = kvs_ref[pl.ds(pl.multiple_of(kt * tk, tk), tk), :]
        out = []
        for c in range(NCHAIN):
            m, l, acc = carry[c]
            s = lax.dot_general(q_aug[c * RC:(c + 1) * RC], tile, _NT, preferred_element_type=F32)
            if causal:
                kpos = kt * tk + lax.broadcasted_iota(jnp.int32, (1, tk), 1)
                s = jnp.where(kpos <= t_col[c * RC:(c + 1) * RC], s, -1e30)
            m_new = jnp.maximum(m, jnp.max(s, axis=1, keepdims=True))
            alpha = jnp.exp2(m - m_new)
            p = jnp.exp2(s - m_new)
            l = alpha * l + jnp.sum(p, axis=1, keepdims=True)
            out.append((m_new, l, alpha * acc + jnp.dot(p.astype(BF16), tile, preferred_element_type=F32)))
        return tuple(out)

    n_full = q0 // tk
    init = tuple((jnp.full((RC, 1), -1e30, F32), jnp.zeros((RC, 1), F32), jnp.zeros((RC, 2 * LANE), F32))
                 for _ in range(NCHAIN))
    carry = lax.fori_loop(0, n_full, tile_step, init)
    carry = tile_step(n_full, carry, causal=True)
    acc_s = jnp.concatenate([c[2] for c in carry], axis=0)
    o_s = acc_s[:, 0:LANE] / jnp.concatenate([c[1] for c in carry], axis=0)

    wlen = WINDOW + QB
    w0 = pl.multiple_of(jnp.maximum(q0 - WINDOW, 0), QB)
    wt = kvw_ref[pl.ds(w0, wlen), :]
    s_w = lax.dot_general(q4, wt, _NT, preferred_element_type=F32)
    dist = t_col - (w0 + lax.broadcasted_iota(jnp.int32, (1, wlen), 1))
    p_w = _masked_softmax2(s_w, (dist >= 0) & (dist < WINDOW))
    o_w = jnp.dot(p_w.astype(BF16), wt, preferred_element_type=F32)

    comb = []
    for h in range(H):
        r = slice(h * QB, (h + 1) * QB)
        comb.append(gate_ref[:, 3 * h:3 * h + 1] * o_c[r] + gate_ref[:, 3 * h + 1:3 * h + 2] * o_s[r]
                    + gate_ref[:, 3 * h + 2:3 * h + 3] * o_w[r])
    low = lax.broadcasted_iota(jnp.int32, (1, LANE), 1) < HEAD_DIM
    for p in range(H // 2):
        o_ref[:, p * LANE:(p + 1) * LANE] = jnp.where(low, pltpu.roll(comb[2 * p], HEAD_DIM, 1), comb[2 * p + 1])


def nsa_prompt(qz, gate, kvc, kvs, kvw, nb, seq_len, tk=512):
    m = qz.shape[0]
    G, QB = NSA_KV_HEADS, Q_BLOCK
    nqb = seq_len // QB
    ncp = kvc.shape[2]
    assert seq_len % tk == 0 and seq_len >= WINDOW + QB and seq_len // SEL_BLOCK <= LANE
    import numpy as np
    c0 = np.arange(ncp)[:, None] * CMP_STRIDE
    s0 = np.arange(LANE)[None, :] * SEL_BLOCK
    n_cmp = ncp - CMP_RATIO + 1
    ov = (c0 < s0 + SEL_BLOCK) & (c0 + CMP_BLOCK > s0) & (np.arange(ncp)[:, None] < n_cmp)
    smap = jnp.asarray(ov, BF16)
    return pl.pallas_call(
        functools.partial(_nsa_prompt_body, tk=tk),
        grid=(nb, G, nqb),
        in_specs=[
            pl.BlockSpec((QB, NSA_HPG * LANE), lambda b, g, q: (b * nqb + q, g)),
            pl.BlockSpec((None, QB, LANE), lambda b, g, q: (g, b * nqb + q, 0)),
            pl.BlockSpec((None, None, ncp, LANE), lambda b, g, q: (b, g, 0, 0)),
            pl.BlockSpec((ncp, LANE), lambda b, g, q: (0, 0)),
            pl.BlockSpec((None, seq_len, 2 * LANE), lambda b, g, q: (g, b, 0)),
            pl.BlockSpec((None, seq_len, LANE), lambda b, g, q: (g, b, 0)),
        ],
        out_specs=pl.BlockSpec((QB, NSA_HPG * HEAD_DIM), lambda b, g, q: (b * nqb + q, g)),
        out_shape=jax.ShapeDtypeStruct((m, NSA_WIDTH), F32),
        compiler_params=pltpu.CompilerParams(
            dimension_semantics=("arbitrary", "arbitrary", "arbitrary"), vmem_limit_bytes=VMEM_LIMIT),
        name="nsa_prompt",
    )(qz, gate, kvc, smap, kvs, kvw)


def _shift_rows(x, prev8, d):
    rolled = pltpu.roll(x, d, 0)
    row8 = lax.broadcasted_iota(jnp.int32, (8, 1), 0)
    first = jnp.where(row8 < d, pltpu.roll(prev8, d, 0), rolled[0:8])
    return jnp.concatenate([first, rolled[8:]], axis=0)


def _rglru_body(rg_ref, buf_ref, h0_ref, cw_ref, cb_ref, wg_ref, bg_ref, lam_ref,
                o_ref, hout_ref, bufout_ref, tail_ref, h_ref):
    tm = rg_ref.shape[0]
    W = RG_WIDTH

    @pl.when(pl.program_id(1) == 0)
    def _():
        tail_ref[...] = buf_ref[...]
        h_ref[...] = h0_ref[...]

    x = rg_ref[:, 0:W]
    gr = rg_ref[:, W:2 * W]
    prev8 = tail_ref[...]
    xc = cb_ref[...] + cw_ref[CONV_W - 1:CONV_W, :] * x
    for d in range(1, CONV_W):
        xc = xc + cw_ref[CONV_W - 1 - d:CONV_W - d, :] * _shift_rows(x, prev8, d)
    gates = jnp.dot(xc.astype(BF16), wg_ref[...], preferred_element_type=F32) + bg_ref[...]
    r = jax.nn.sigmoid(gates[:, 0:W])
    i = jax.nn.sigmoid(gates[:, W:2 * W])
    log_a = -RG_C * r * jax.nn.softplus(-lam_ref[...])
    a = jnp.exp(log_a)
    th = jnp.tanh(log_a)
    b = jnp.sqrt(-2.0 * th / (1.0 - th)) * (i * xc)
    row = lax.broadcasted_iota(jnp.int32, (tm, 1), 0)
    d = 1
    while d < tm:
        keep = row >= d
        a_sh = jnp.where(keep, pltpu.roll(a, d, 0), 1.0)
        b_sh = jnp.where(keep, pltpu.roll(b, d, 0), 0.0)
        b = a * b_sh + b
        a = a * a_sh
        d *= 2
    h = a * h_ref[7:8, :] + b
    o_ref[...] = h * jax.nn.gelu(gr)
    h_ref[...] = h[tm - 8:tm]
    tail_ref[...] = x[tm - 8:tm]
    hout_ref[...] = h[tm - 8:tm]
    bufout_ref[...] = x[tm - 8:tm]


def rglru_prompt(rg, conv_buf, h0, conv_w, conv_b, wa, ba, wx, bx, lam, seq_len, tm=256):
    m = rg.shape[0]
    nb = m // seq_len
    W = RG_WIDTH
    tm = _row_tile(seq_len, tm)
    nt = seq_len // tm
    bd = jax.scipy.linalg.block_diag
    wg = jnp.concatenate([bd(*[wa[n] for n in range(RG_BLOCKS)]), bd(*[wx[n] for n in range(RG_BLOCKS)])], axis=1).astype(BF16)
    bg = jnp.concatenate([ba, bx]).reshape(1, 2 * W)
    buf8 = jnp.pad(conv_buf, ((0, 0), (8 - (CONV_W - 1), 0), (0, 0)))
    h08 = jnp.broadcast_to(h0[:, None, :], (nb, 8, W))
    const = lambda shape: pl.BlockSpec(shape, lambda b, t: (0,) * len(shape))
    per_b = pl.BlockSpec((None, 8, W), lambda b, t: (b, 0, 0))
    out, h8, nbuf8 = pl.pallas_call(
        _rglru_body,
        grid=(nb, nt),
        in_specs=[pl.BlockSpec((tm, 2 * W), lambda b, t: (b * nt + t, 0)), per_b, per_b,
                  const((CONV_W, W)), const((1, W)), const((W, 2 * W)), const((1, 2 * W)), const((1, W))],
        out_specs=[pl.BlockSpec((tm, W), lambda b, t: (b * nt + t, 0)), per_b, per_b],
        out_shape=[jax.ShapeDtypeStruct((m, W), F32), jax.ShapeDtypeStruct((nb, 8, W), F32),
                   jax.ShapeDtypeStruct((nb, 8, W), F32)],
        scratch_shapes=[pltpu.VMEM((8, W), F32), pltpu.VMEM((8, W), F32)],
        compiler_params=pltpu.CompilerParams(dimension_semantics=("arbitrary", "arbitrary")),
        name="rglru_prompt",
    )(rg, buf8, h08, conv_w, conv_b.reshape(1, W), wg, bg, lam.reshape(1, W))
    return out, h8[:, 7], nbuf8[:, 8 - (CONV_W - 1):]


def _split_dot(x, w_bf16, pieces):
    acc = None
    for _ in range(pieces):
        xb = x.astype(BF16)
        part = jnp.dot(xb, w_bf16, preferred_element_type=F32)
        acc = part if acc is None else acc + part
        x = x - xb.astype(F32)
    return acc


HG_SAFE_DECAY = 75.0


def _hgrn_body(hg_ref, s0_ref, lb_ref, gain_ref, tri_ref, tribd_ref, ones_ref, o_ref, sout_ref, st_ref, *, chunk):
    tm = hg_ref.shape[0]
    W, H = HG_WIDTH, HG_HEADS
    C = chunk
    nck = tm // C

    @pl.when(pl.program_id(1) == 0)
    def _():
        st_ref[...] = s0_ref[...]

    lb = lb_ref[...]
    ones_blk = ones_ref[...]
    same_head = ones_blk > 0

    def finish(o, rows):
        ms = _split_dot(o * o, ones_blk, 2) * (1.0 / HG_DV)
        o = o * lax.rsqrt(ms + EPS) * gain_ref[...]
        o_ref[rows, :] = o * jax.nn.silu(hg_ref[rows, 3 * W:4 * W])

    q = jax.nn.silu(hg_ref[:, 0:W])
    f = lb + (1.0 - lb) * jax.nn.sigmoid(hg_ref[:, W:2 * W])
    v = hg_ref[:, 2 * W:3 * W]
    k = 1.0 - f
    bcum = _split_dot_left(tribd_ref[...], jnp.log(f))
    b_last = [bcum[(c + 1) * C - 1:(c + 1) * C, :] for c in range(nck)]
    total = b_last[0]
    for c in range(1, nck):
        total = jnp.minimum(total, b_last[c])
    safe = jnp.min(total) >= -HG_SAFE_DECAY

    @pl.when(safe)
    def _():
        qt = q * jnp.exp(bcum)
        kt = (k * jnp.exp(-bcum)).astype(BF16)
        vb = v.astype(BF16)
        lane_head = lax.broadcasted_iota(jnp.int32, (1, W), 1) // HG_DV
        q_exp = jnp.concatenate([jnp.where(lane_head == h, qt, 0.0) for h in range(H)], axis=0).astype(BF16)
        a = lax.dot_general(q_exp, kt, _NT, preferred_element_type=F32)
        t_row = lax.broadcasted_iota(jnp.int32, (H * tm, 1), 0) & (tm - 1)
        s_col = lax.broadcasted_iota(jnp.int32, (1, tm), 1)
        pair = (s_col <= t_row) & ((s_col // C) == (t_row // C))
        o_exp = jnp.dot(jnp.where(pair, a, 0.0).astype(BF16), vb, preferred_element_type=F32)
        o = jnp.where(lane_head == 0, o_exp[0:tm], 0.0)
        for h in range(1, H):
            o = o + jnp.where(lane_head == h, o_exp[h * tm:(h + 1) * tm], 0.0)
        st = st_ref[...]
        inter = []
        for c in range(nck):
            r = slice(c * C, (c + 1) * C)
            inter.append(lax.dot_general(qt[r].astype(BF16), st.astype(BF16), _NT, preferred_element_type=F32))
            k_hat = (k[r] * jnp.exp(b_last[c] - bcum[r])).astype(BF16)
            upd = lax.dot_general(vb[r], k_hat, (((0,), (0,)), ((), ())), preferred_element_type=F32)
            st = jnp.exp(b_last[c]) * st + jnp.where(same_head, upd, 0.0)
        st_ref[...] = st
        finish(o + jnp.concatenate(inter, axis=0), slice(None))

    @pl.when(jnp.logical_not(safe))
    def _():
        tri = tri_ref[...]
        rowc = lax.broadcasted_iota(jnp.int32, (C, 1), 0)

        def chunk_step(c, _):
            r0 = pl.multiple_of(c * C, C)
            rows = pl.ds(r0, C)
            qc = jax.nn.silu(hg_ref[rows, 0:W])
            fc = lb + (1.0 - lb) * jax.nn.sigmoid(hg_ref[rows, W:2 * W])
            vc = hg_ref[rows, 2 * W:3 * W]
            kc = 1.0 - fc
            bc = _split_dot_left(tri, jnp.log(fc))
            st = st_ref[...]
            o = lax.dot_general((qc * jnp.exp(bc)).astype(BF16), st.astype(BF16), _NT, preferred_element_type=F32)

            def offset_step(dlt, carry):
                o, b_sh, k_sh, v_sh = carry
                p = jnp.where(rowc >= dlt, qc * k_sh * jnp.exp(bc - b_sh), 0.0)
                o = o + _split_dot(p, ones_blk, 2) * v_sh
                return o, pltpu.roll(b_sh, 1, 0), pltpu.roll(k_sh, 1, 0), pltpu.roll(v_sh, 1, 0)

            o, _, _, _ = lax.fori_loop(0, C, offset_step, (o, bc, kc, vc))
            bl = bc[C - 1:C, :]
            k_hat = (kc * jnp.exp(bl - bc)).astype(BF16)
            upd = lax.dot_general(vc.astype(BF16), k_hat, (((0,), (0,)), ((), ())), preferred_element_type=F32)
            st_ref[...] = jnp.exp(bl) * st + jnp.where(same_head, upd, 0.0)
            finish(o, rows)
            return 0

        lax.fori_loop(0, nck, chunk_step, 0)

    sout_ref[...] = st_ref[...]


def _split_dot_left(w_bf16, x):
    acc = None
    for _ in range(3):
        xb = x.astype(BF16)
        part = jnp.dot(w_bf16, xb, preferred_element_type=F32)
        acc = part if acc is None else acc + part
        x = x - xb.astype(F32)
    return acc


def hgrn_prompt(hg, s0, lb, gain, seq_len, tm=256, chunk=HG_CHUNK):
    import numpy as np
    m = hg.shape[0]
    nb = m // seq_len
    W, H = HG_WIDTH, HG_HEADS
    tm = _row_tile(seq_len, tm)
    nt = seq_len // tm
    head = np.arange(W) // HG_DV
    ones_blk = jnp.asarray(head[:, None] == head[None, :], BF16)
    tri = jnp.asarray(np.tril(np.ones((chunk, chunk))), BF16)
    rt = np.arange(tm)
    tri_bd = jnp.asarray((rt[None, :] <= rt[:, None]) & (rt[None, :] // chunk == rt[:, None] // chunk), BF16)
    s0_t = jnp.einsum('bhkv,hg->bhvgk', s0, jnp.eye(H, dtype=s0.dtype)).reshape(nb, W, W)
    const = lambda shape: pl.BlockSpec(shape, lambda b, t: (0,) * len(shape))
    per_b = pl.BlockSpec((None, W, W), lambda b, t: (b, 0, 0))
    out, s_t = pl.pallas_call(
        functools.partial(_hgrn_body, chunk=chunk),
        grid=(nb, nt),
        in_specs=[pl.BlockSpec((tm, 4 * W), lambda b, t: (b * nt + t, 0)), per_b,
                  const((1, W)), const((1, W)), const((chunk, chunk)), const((tm, tm)), const((W, W))],
        out_specs=[pl.BlockSpec((tm, W), lambda b, t: (b * nt + t, 0)), per_b],
        out_shape=[jax.ShapeDtypeStruct((m, W), F32), jax.ShapeDtypeStruct((nb, W, W), F32)],
        scratch_shapes=[pltpu.VMEM((W, W), F32)],
        compiler_params=pltpu.CompilerParams(dimension_semantics=("arbitrary", "arbitrary")),
        name="hgrn_prompt",
    )(hg, s0_t, lb.reshape(1, W), gain.reshape(1, W), tri, tri_bd, ones_blk)
    s5 = s_t.reshape(nb, H, HG_DV, H, HG_DK)
    s_fin = jnp.stack([s5[:, h, :, h, :] for h in range(H)], axis=1).transpose(0, 1, 3, 2)
    return out, s_fin


def _split(a, sizes):
    out, o = [], 0
    for s in sizes:
        out.append(a[..., o:o + s])
        o += s
    return out


def rms_norm(x, g):
    x32 = x.astype(jnp.float32)
    y = x32 * lax.rsqrt(jnp.mean(x32 * x32, axis=-1, keepdims=True) + EPS)
    return (y * g.astype(jnp.float32)).astype(x.dtype)


def partial_rope(x, pos):
    half = ROPE_DIM // 2
    inv = ROPE_THETA ** (-jnp.arange(half, dtype=jnp.float32) * 2.0 / ROPE_DIM)
    ang = pos.astype(jnp.float32)[:, None] * inv
    cos = jnp.cos(ang)[:, None, :].astype(x.dtype)
    sin = jnp.sin(ang)[:, None, :].astype(x.dtype)
    x1, x2, rest = x[..., :half], x[..., half:ROPE_DIM], x[..., ROPE_DIM:]
    return jnp.concatenate([x1 * cos - x2 * sin, x2 * cos + x1 * sin, rest], axis=-1)


def masked_softmax(s, mask):
    s = jnp.where(mask, s, -1e30)
    m = jnp.max(s, axis=-1, keepdims=True)
    e = jnp.where(mask, jnp.exp(s - m), 0.0)
    return e / jnp.maximum(jnp.sum(e, axis=-1, keepdims=True), 1e-30)


def compress_kv(raw, pos_emb, w1, b1, w2, b2):
    B, L = raw.shape[0], raw.shape[1]
    nch = L // CMP_STRIDE
    n_cmp = nch - CMP_RATIO + 1
    chunks = raw[:, :nch * CMP_STRIDE].reshape(B, nch, CMP_STRIDE, NSA_KV_HEADS, HEAD_DIM)
    h = b1
    for r in range(CMP_RATIO):
        sl = slice(r * CMP_STRIDE, (r + 1) * CMP_STRIDE)
        pre = jnp.einsum('bnsgd,sdh->bngh', chunks + pos_emb[sl][:, None, :], w1[sl])
        h = h + pre[:, r:r + n_cmp]
    out = jax.nn.gelu(h) @ w2 + b2
    end = jnp.arange(n_cmp, dtype=jnp.int32) * CMP_STRIDE + CMP_BLOCK - 1
    return out, end


def selection_map(n_cmp, n_sel):
    c0 = jnp.arange(n_cmp) * CMP_STRIDE
    s0 = jnp.arange(n_sel) * SEL_BLOCK
    ov = (c0[:, None] < s0[None, :] + SEL_BLOCK) & (c0[:, None] + CMP_BLOCK > s0[None, :])
    return ov.astype(jnp.float32)


def nsa_attend_block(q, gate, t_pos, kw, vw, w_pos, k_cmp, v_cmp, c_end, ks_blk, vs_blk, smap):
    f32 = jnp.float32
    scale = HEAD_DIM ** -0.5
    s_c = jnp.einsum('bqghd,bngd->bqghn', q, k_cmp).astype(f32) * scale
    m_c = (c_end[None, :] <= t_pos[:, None])[None, :, None, None, :]
    p_c = masked_softmax(s_c, m_c)
    o_c = jnp.einsum('bqghn,bngd->bqghd', p_c.astype(v_cmp.dtype), v_cmp)
    imp = jnp.einsum('bqghn,nj->bqgj', p_c, smap)
    n_sel = ks_blk.shape[1]
    j = jnp.arange(n_sel)[None, :]
    cur = (t_pos // SEL_BLOCK)[:, None]
    valid = j * SEL_BLOCK <= t_pos[:, None]
    forced = (j == 0) | (j == cur) | (j == cur - 1)
    score = jnp.where(valid[None, :, None, :], jnp.where(forced[None, :, None, :], FORCE_SCORE, imp), -FORCE_SCORE)
    _, idx = lax.top_k(score, min(SEL_TOPK, n_sel))
    bi = jnp.arange(q.shape[0])[:, None, None, None]
    gi = jnp.arange(NSA_KV_HEADS)[None, None, :, None]
    kb = ks_blk[bi, idx, gi]
    vb = vs_blk[bi, idx, gi]
    kpos = idx[..., None] * SEL_BLOCK + jnp.arange(SEL_BLOCK)
    m_s = (kpos <= t_pos[None, :, None, None, None])[:, :, :, None]
    s_s = jnp.einsum('bqghd,bqgkld->bqghkl', q, kb).astype(f32) * scale
    shp = s_s.shape
    p_s = masked_softmax(s_s.reshape(*shp[:4], -1), m_s.reshape(*m_s.shape[:4], -1)).reshape(shp)
    o_s = jnp.einsum('bqghkl,bqgkld->bqghd', p_s.astype(vb.dtype), vb)
    s_w = jnp.einsum('bqghd,bwgd->bqghw', q, kw).astype(f32) * scale
    dist = t_pos[:, None] - w_pos[None, :]
    m_w = ((dist >= 0) & (dist < WINDOW) & (w_pos[None, :] >= 0))[None, :, None, None, :]
    p_w = masked_softmax(s_w, m_w)
    o_w = jnp.einsum('bqghw,bwgd->bqghd', p_w.astype(vw.dtype), vw)
    return gate[..., 0:1] * o_c + gate[..., 1:2] * o_s + gate[..., 2:3] * o_w


def nsa_mixer(q, kvs, gate, pos, cmp_pos, cmp_w1, cmp_b1, cmp_w2, cmp_b2, past):
    B, T = q.shape[:2]
    G, HD = NSA_KV_HEADS, HEAD_DIM
    q = partial_rope(q.reshape(B, T, NSA_HEADS, HD), pos).reshape(B, T, G, NSA_HPG, HD)
    kc, vc, ks, vs, kw, vw = [a.reshape(B, T, G, HD) for a in _split(kvs, (KV_W,) * 6)]
    ks = partial_rope(ks, pos)
    kw = partial_rope(kw, pos)
    gate = jax.nn.sigmoid(gate.reshape(B, T, G, NSA_HPG, 3))
    new_cmp = jnp.stack([kc, vc], axis=2)
    new_sel = jnp.stack([ks, vs], axis=2)
    new_win = jnp.stack([kw, vw], axis=2)
    if past is None:
        cmp_all, sel_all = new_cmp, new_sel
        win_all = jnp.pad(new_win, ((0, 0), (WINDOW, 0), (0, 0), (0, 0), (0, 0)))
        w_pos = pos[0] - WINDOW + jnp.arange(T + WINDOW, dtype=jnp.int32)
        win_state = new_win[:, T - min(WINDOW, T):]
    else:
        cmp_past, sel_past, win_buf = past
        cmp_all = jnp.concatenate([cmp_past, new_cmp], axis=1)
        sel_all = jnp.concatenate([sel_past, new_sel], axis=1)
        win_all = jnp.concatenate([win_buf, new_win], axis=1)
        nb = win_buf.shape[1]
        w_pos = pos[0] - nb + jnp.arange(nb + T, dtype=jnp.int32)
        win_state = win_all[:, T:]
    L = cmp_all.shape[1]
    k_cmp, c_end = compress_kv(cmp_all[:, :, 0], cmp_pos[0], cmp_w1[0], cmp_b1[0], cmp_w2[0], cmp_b2[0])
    v_cmp, _ = compress_kv(cmp_all[:, :, 1], cmp_pos[1], cmp_w1[1], cmp_b1[1], cmp_w2[1], cmp_b2[1])
    k_cmp = partial_rope(k_cmp, c_end)
    n_sel = -(-L // SEL_BLOCK)
    sel_pad = jnp.pad(sel_all, ((0, 0), (0, n_sel * SEL_BLOCK - L), (0, 0), (0, 0), (0, 0)))
    sel_blk = sel_pad.reshape(B, n_sel, SEL_BLOCK, 2, G, HD).transpose(3, 0, 1, 4, 2, 5)
    smap = selection_map(k_cmp.shape[1], n_sel)
    kw_all, vw_all = win_all[:, :, 0], win_all[:, :, 1]
    if past is None and T > Q_BLOCK and T % Q_BLOCK == 0:
        def one_block(n):
            q0 = n * Q_BLOCK
            return nsa_attend_block(
                lax.dynamic_slice_in_dim(q, q0, Q_BLOCK, axis=1),
                lax.dynamic_slice_in_dim(gate, q0, Q_BLOCK, axis=1),
                lax.dynamic_slice_in_dim(pos, q0, Q_BLOCK, axis=0),
                lax.dynamic_slice_in_dim(kw_all, q0, Q_BLOCK + WINDOW, axis=1),
                lax.dynamic_slice_in_dim(vw_all, q0, Q_BLOCK + WINDOW, axis=1),
                lax.dynamic_slice_in_dim(w_pos, q0, Q_BLOCK + WINDOW, axis=0),
                k_cmp, v_cmp, c_end, sel_blk[0], sel_blk[1], smap)
        o = lax.map(one_block, jnp.arange(T // Q_BLOCK))
        o = jnp.moveaxis(o, 0, 1).reshape(B, T, NSA_WIDTH)
    else:
        o = nsa_attend_block(q, gate, pos, kw_all, vw_all, w_pos, k_cmp, v_cmp, c_end,
                             sel_blk[0], sel_blk[1], smap).reshape(B, T, NSA_WIDTH)
    return o, new_cmp, new_sel, win_state


def _lin_combine(e1, e2):
    a1, b1 = e1
    a2, b2 = e2
    return a1 * a2, a2 * b1 + b2


def rglru_mixer(xr, gr, conv_buf, h0, conv_w, conv_b, wa, ba, wx, bx, lam):
    B, T = xr.shape[:2]
    f32 = jnp.float32
    xcat = jnp.concatenate([conv_buf.astype(xr.dtype), xr], axis=1)
    xc = conv_b + sum(conv_w[k] * xcat[:, k:k + T] for k in range(CONV_W))
    new_buf = xcat[:, T:]
    xb = xc.reshape(B, T, RG_BLOCKS, RG_BW)
    r = jax.nn.sigmoid(jnp.einsum('btnd,nde->btne', xb, wa).reshape(B, T, RG_WIDTH) + ba)
    i = jax.nn.sigmoid(jnp.einsum('btnd,nde->btne', xb, wx).reshape(B, T, RG_WIDTH) + bx)
    log_a = -RG_C * r.astype(f32) * jax.nn.softplus(-lam.astype(f32))
    a = jnp.exp(log_a)
    b = jnp.sqrt(-jnp.expm1(2.0 * log_a)) * (i * xc).astype(f32)
    b = b.at[:, 0].add(a[:, 0] * h0.astype(f32))
    _, h = lax.associative_scan(_lin_combine, (a, b), axis=1)
    out = h.astype(xr.dtype) * jax.nn.gelu(gr)
    return out, h[:, -1].astype(h0.dtype), new_buf


def gated_recurrence(q, k, v, logf, s0):
    B, T, H, DK = q.shape
    DV = v.shape[-1]
    C = HG_CHUNK if T % HG_CHUNK == 0 else T
    nc = T // C

    def to_chunks(a):
        return jnp.moveaxis(a.reshape(B, nc, C, *a.shape[2:]), 1, 0)

    tri = jnp.tril(jnp.ones((C, C), dtype=bool))[None, :, :, None, None]

    def step(S, inp):
        qc, kc, vc, gc = inp
        bcum = jnp.cumsum(gc, axis=1)
        o = jnp.einsum('bthk,bhkv->bthv', qc * jnp.exp(bcum), S)
        dec = jnp.exp(jnp.where(tri, bcum[:, :, None] - bcum[:, None, :], -jnp.inf))
        A = jnp.einsum('bthk,bshk,btshk->bhts', qc, kc, dec)
        o = o + jnp.einsum('bhts,bshv->bthv', A, vc)
        bl = bcum[:, -1]
        S = jnp.exp(bl)[..., None] * S + jnp.einsum('bshk,bshv->bhkv', kc * jnp.exp(bl[:, None] - bcum), vc)
        return S, o

    s_fin, o = lax.scan(step, s0, (to_chunks(q), to_chunks(k), to_chunks(v), to_chunks(logf)))
    return jnp.moveaxis(o, 0, 1).reshape(B, T, H, DV), s_fin


def hgrn2_mixer(hq, hf, hi, hg, s0, lb, gain):
    B, T = hq.shape[:2]
    f32 = jnp.float32
    q = jax.nn.silu(hq.astype(f32)).reshape(B, T, HG_HEADS, HG_DK)
    lbh = lb.reshape(HG_HEADS, HG_DK)
    f = lbh + (1.0 - lbh) * jax.nn.sigmoid(hf.astype(f32).reshape(B, T, HG_HEADS, HG_DK))
    v = hi.astype(f32).reshape(B, T, HG_HEADS, HG_DV)
    o, s_fin = gated_recurrence(q, 1.0 - f, v, jnp.log(f), s0.astype(f32))
    o = rms_norm(o, gain.reshape(HG_HEADS, HG_DV)).reshape(B, T, HG_WIDTH).astype(hq.dtype)
    return o * jax.nn.silu(hg), s_fin.astype(s0.dtype)


def forward_layer(x, pos, w, wb, past):
    (norm_mix, w_in, w_out, norm_ffn, w_up, w_down, cmp_pos, cmp_w1, cmp_b1, cmp_w2, cmp_b2,
     rg_conv_w, rg_conv_b, rg_wa, rg_ba, rg_wx, rg_bx, rg_lambda, hg_lb, hg_gain) = w
    w_in_b, w_out_b, w_up_b, w_down_b = wb
    B, T = x.shape[:2]
    x2 = x.reshape(B * T, D_MODEL)
    proj = norm_matmul(x2, norm_mix, w_in_b)[:, :D_IN].reshape(B, T, D_IN)
    q, kvs, gate, rg_x, rg_g, hg_q, hg_f, hg_i, hg_g = _split(proj, IN_SIZES)
    if past is None:
        nsa_past = None
        rg_buf = jnp.zeros((B, CONV_W - 1, RG_WIDTH), x.dtype)
        rg_h0 = jnp.zeros((B, RG_WIDTH), x.dtype)
        hg_s0 = jnp.zeros((B, HG_HEADS, HG_DK, HG_DV), x.dtype)
    else:
        cmp_past, sel_past, win_buf, rg_h0, rg_buf, hg_s0 = past
        nsa_past = (cmp_past, sel_past, win_buf)
    o_nsa, new_cmp, new_sel, new_win = nsa_mixer(q, kvs, gate, pos, cmp_pos, cmp_w1, cmp_b1, cmp_w2, cmp_b2, nsa_past)
    o_rg, new_h, new_buf = rglru_mixer(rg_x, rg_g, rg_buf, rg_h0, rg_conv_w, rg_conv_b, rg_wa, rg_ba, rg_wx, rg_bx, rg_lambda)
    o_hg, new_s = hgrn2_mixer(hg_q, hg_f, hg_i, hg_g, hg_s0, hg_lb, hg_gain)
    mix = jnp.concatenate([o_nsa, o_rg, o_hg], axis=-1).reshape(B * T, MIX_WIDTH)
    x2 = out_ffn(x2, mix, w_out_b, norm_ffn, w_up_b, w_down_b)
    return x2.reshape(B, T, D_MODEL), (new_cmp, new_sel, new_win, new_h, new_buf, new_s)


def forward_layer_prompt(x, w, wb, pw):
    (norm_mix, w_in, w_out, norm_ffn, w_up, w_down, cmp_pos, cmp_w1, cmp_b1, cmp_w2, cmp_b2,
     rg_conv_w, rg_conv_b, rg_wa, rg_ba, rg_wx, rg_bx, rg_lambda, hg_lb, hg_gain) = w
    _, w_out_b, w_up_b, w_down_b = wb
    w_rel, cw, tables, tables_cmp = pw
    B, T = x.shape[:2]
    x2 = x.reshape(B * T, D_MODEL)
    qz, cmp, sel, win, kvs, kvw, gate, rg, hg = in_proj_prompt(x2, norm_mix, w_rel, tables, T)
    kvc = compress(cmp, cw, tables_cmp, T)
    o_nsa = nsa_prompt(qz, gate, kvc, kvs, kvw, B, T)
    kv_shape = (B, T, 2, NSA_KV_HEADS, HEAD_DIM)
    new_cmp, new_sel = cmp.reshape(kv_shape), sel.reshape(kv_shape)
    new_win = win.reshape(kv_shape)[:, T - min(WINDOW, T):]
    rg_buf = jnp.zeros((B, CONV_W - 1, RG_WIDTH), x.dtype)
    rg_h0 = jnp.zeros((B, RG_WIDTH), x.dtype)
    hg_s0 = jnp.zeros((B, HG_HEADS, HG_DK, HG_DV), x.dtype)
    o_rg, new_h, new_buf = rglru_prompt(rg, rg_buf, rg_h0, rg_conv_w, rg_conv_b, rg_wa, rg_ba, rg_wx, rg_bx,
                                        rg_lambda, T)
    o_hg, new_s = hgrn_prompt(hg, hg_s0, hg_lb, hg_gain, T)
    mix = jnp.concatenate([o_nsa, o_rg, o_hg], axis=-1)
    x2 = out_ffn(x2, mix, w_out_b, norm_ffn, w_up_b, w_down_b)
    return x2.reshape(B, T, D_MODEL), (new_cmp, new_sel, new_win, new_h, new_buf, new_s)


def kernel(x_prompt, x_sample, cache_nsa_cmp_kv, cache_nsa_sel_kv, cache_nsa_win_kv, state_rglru_h,
           state_rglru_conv, state_hgrn_s, page_table, norm_mix, w_in, w_out, norm_ffn, w_up, w_down,
           cmp_pos, cmp_w1, cmp_b1, cmp_w2, cmp_b2, rg_conv_w, rg_conv_b, rg_wa, rg_ba, rg_wx, rg_bx,
           rg_lambda, hg_lower_bounds, hg_gain, final_norm):
    lb = jnp.cumsum(jax.nn.softmax(hg_lower_bounds.astype(jnp.float32), axis=0), axis=0)
    lb = lb - lb[0]
    past_len = page_table.shape[1] * PAGE_SIZE
    n_dec = x_sample.shape[0]
    pos_p = jnp.arange(x_prompt.shape[1], dtype=jnp.int32)
    pos_s = past_len + jnp.arange(x_sample.shape[1], dtype=jnp.int32)
    d_in_pad = _round_up(D_IN, LANE)
    w_in_b = jnp.pad(w_in, ((0, 0), (0, 0), (0, d_in_pad - D_IN))).astype(BF16)
    w_out_b = w_out.astype(BF16)
    w_up_b = w_up.astype(BF16)
    w_down_b = w_down.astype(BF16)
    w_rel = relayout_w_in(w_in)
    cws = compress_weights(cmp_pos, cmp_w1, cmp_b1, cmp_w2, cmp_b2)
    T = x_prompt.shape[1]
    tables = rope_tables(pos_p)
    c_end_p = jnp.arange(T // CMP_STRIDE, dtype=jnp.int32) * CMP_STRIDE + CMP_BLOCK - 1
    tables_cmp = rope_tables(c_end_p)[:3]
    xp, xs = x_prompt, x_sample
    st_p, st_s = [], []
    for l in range(DEPTH):
        w = (norm_mix[l], w_in[l], w_out[l], norm_ffn[l], w_up[l], w_down[l], cmp_pos[l], cmp_w1[l], cmp_b1[l],
             cmp_w2[l], cmp_b2[l], rg_conv_w[l], rg_conv_b[l], rg_wa[l], rg_ba[l], rg_wx[l], rg_bx[l],
             rg_lambda[l], lb[l], hg_gain[l])
        wb = (w_in_b[l], w_out_b[l], w_up_b[l], w_down_b[l])
        pw = (w_rel[l], tuple(a[l] for a in cws), tables, tables_cmp)
        xp, sp = forward_layer_prompt(xp, w, wb, pw)
        past = (cache_nsa_cmp_kv[l][page_table].reshape(n_dec, past_len, 2, NSA_KV_HEADS, HEAD_DIM),
                cache_nsa_sel_kv[l][page_table].reshape(n_dec, past_len, 2, NSA_KV_HEADS, HEAD_DIM),
                cache_nsa_win_kv[l], state_rglru_h[l], state_rglru_conv[l], state_hgrn_s[l])
        xs, ss = forward_layer(xs, pos_s, w, wb, past)
        st_p.append(sp)
        st_s.append(ss)

    def stack(sts, i):
        return jnp.stack([s[i] for s in sts], axis=0)

    y_prompt = final_rms(xp.reshape(-1, D_MODEL), final_norm).reshape(xp.shape)
    y_sample = final_rms(xs.reshape(-1, D_MODEL), final_norm).reshape(xs.shape)
    return (y_prompt, y_sample, stack(st_p, 0), stack(st_p, 1), stack(st_p, 2), stack(st_p, 3), stack(st_p, 4),
            stack(st_p, 5), stack(st_s, 0), stack(st_s, 1), stack(st_s, 2), stack(st_s, 3), stack(st_s, 4),
            stack(st_s, 5))
```

```python
import functools

import jax
import jax.numpy as jnp
from jax import lax
from jax.experimental import pallas as pl
from jax.experimental.pallas import tpu as pltpu

F32 = jnp.float32
BF16 = jnp.bfloat16

D_MODEL = 1024
DEPTH = 4
PAGE_SIZE = 128
HEAD_DIM = 64
NSA_HEADS = D_MODEL // (2 * HEAD_DIM)
NSA_KV_HEADS = 2
NSA_HPG = NSA_HEADS // NSA_KV_HEADS
NSA_WIDTH = NSA_HEADS * HEAD_DIM
KV_W = NSA_KV_HEADS * HEAD_DIM
CMP_BLOCK = 32
CMP_STRIDE = 16
CMP_RATIO = CMP_BLOCK // CMP_STRIDE
CMP_HIDDEN = 256
SEL_BLOCK = 64
SEL_TOPK = 16
WINDOW = 512
Q_BLOCK = 128
FORCE_SCORE = 1e6
ROPE_DIM = HEAD_DIM // 4
ROPE_THETA = 500000.0
RG_WIDTH = D_MODEL // 4
RG_BLOCKS = 4
RG_BW = RG_WIDTH // RG_BLOCKS
RG_C = 8.0
CONV_W = 4
HG_HEADS = 4
HG_DK = 64
HG_DV = D_MODEL // 4 // HG_HEADS
HG_WIDTH = HG_HEADS * HG_DV
HG_CHUNK = 64
MIX_WIDTH = NSA_WIDTH + RG_WIDTH + HG_WIDTH
D_FF = 4 * D_MODEL
EPS = 1e-6
IN_SIZES = (NSA_WIDTH, 6 * KV_W, 3 * NSA_HEADS, RG_WIDTH, RG_WIDTH, HG_HEADS * HG_DK, HG_HEADS * HG_DK, HG_WIDTH, HG_WIDTH)
D_IN = sum(IN_SIZES)

LANE = 128
VMEM_LIMIT = 56 * 1024 * 1024


def _round_up(n, m):
    return -(-n // m) * m


def _row_tile(m, want):
    t = min(m, want)
    while m % t:
        t //= 2
    return t


def _rms(x, g):
    return x * lax.rsqrt(jnp.mean(x * x, axis=-1, keepdims=True) + EPS) * g


def _norm_matmul_body(x_ref, g_ref, w_ref, o_ref):
    y = _rms(x_ref[...], g_ref[...]).astype(BF16)
    o_ref[...] = jnp.dot(y, w_ref[...], preferred_element_type=F32)


def norm_matmul(x, g, w_bf16, tm=512):
    m, d = x.shape
    n = w_bf16.shape[1]
    tm = _row_tile(m, tm)
    return pl.pallas_call(
        _norm_matmul_body,
        grid=(m // tm,),
        in_specs=[
            pl.BlockSpec((tm, d), lambda i: (i, 0)),
            pl.BlockSpec((1, d), lambda i: (0, 0)),
            pl.BlockSpec((d, n), lambda i: (0, 0)),
        ],
        out_specs=pl.BlockSpec((tm, n), lambda i: (i, 0)),
        out_shape=jax.ShapeDtypeStruct((m, n), F32),
        compiler_params=pltpu.CompilerParams(
            dimension_semantics=("arbitrary",), vmem_limit_bytes=VMEM_LIMIT),
        name="norm_in_proj",
    )(x, g.reshape(1, d), w_bf16)


def _out_ffn_body(x_ref, mix_ref, wo_ref, g_ref, wu_ref, wd_ref, o_ref, *, f_tile):
    x1 = x_ref[...] + jnp.dot(mix_ref[...].astype(BF16), wo_ref[...], preferred_element_type=F32)
    y = _rms(x1, g_ref[...]).astype(BF16)
    acc = x1
    for j in range(wu_ref.shape[1] // f_tile):
        h = jnp.maximum(jnp.dot(y, wu_ref[:, j * f_tile:(j + 1) * f_tile], preferred_element_type=F32), 0.0)
        acc = acc + jnp.dot((h * h).astype(BF16), wd_ref[j * f_tile:(j + 1) * f_tile, :], preferred_element_type=F32)
    o_ref[...] = acc


def out_ffn(x, mix, wo, g, wu, wd, tm=512, f_tile=512):
    m, d = x.shape
    dm = mix.shape[1]
    dff = wu.shape[1]
    tm = _row_tile(m, tm)
    const = lambda i: (0, 0)
    return pl.pallas_call(
        functools.partial(_out_ffn_body, f_tile=f_tile),
        grid=(m // tm,),
        in_specs=[
            pl.BlockSpec((tm, d), lambda i: (i, 0)),
            pl.BlockSpec((tm, dm), lambda i: (i, 0)),
            pl.BlockSpec((dm, d), const),
            pl.BlockSpec((1, d), const),
            pl.BlockSpec((d, dff), const),
            pl.BlockSpec((dff, d), const),
        ],
        out_specs=pl.BlockSpec((tm, d), lambda i: (i, 0)),
        out_shape=jax.ShapeDtypeStruct((m, d), F32),
        compiler_params=pltpu.CompilerParams(
            dimension_semantics=("arbitrary",), vmem_limit_bytes=VMEM_LIMIT),
        name="out_proj_ffn",
    )(x, mix, wo, g.reshape(1, d), wu, wd)


def _final_norm_body(x_ref, g_ref, o_ref):
    o_ref[...] = _rms(x_ref[...], g_ref[...])


def final_rms(x, g, tm=1024):
    m, d = x.shape
    tm = _row_tile(m, tm)
    return pl.pallas_call(
        _final_norm_body,
        grid=(m // tm,),
        in_specs=[pl.BlockSpec((tm, d), lambda i: (i, 0)), pl.BlockSpec((1, d), lambda i: (0, 0))],
        out_specs=pl.BlockSpec((tm, d), lambda i: (i, 0)),
        out_shape=jax.ShapeDtypeStruct((m, d), F32),
        name="final_norm",
    )(x, g.reshape(1, d))


MASK_BIG = 2.0 ** 100
LOG2E = 1.4426950408889634

_O_Q = 0
_O_KV = NSA_WIDTH
_O_GATE = _O_KV + 6 * KV_W
_O_RG = _O_GATE + 3 * NSA_HEADS
_O_HG = _O_RG + 2 * RG_WIDTH

_P_QZ = 0
_P_CMP = _P_QZ + NSA_HEADS * LANE
_P_SEL = _P_CMP + 2 * KV_W
_P_WIN = _P_SEL + 2 * KV_W
_P_KVS = _P_WIN + 2 * KV_W
_P_KVW = _P_KVS + 2 * KV_W
_P_GATE = _P_KVW + 2 * KV_W
_P_RG = _P_GATE + NSA_KV_HEADS * LANE
_P_HG = _P_RG + 2 * RG_WIDTH
_P_END = _P_HG + 4 * HG_WIDTH


def _in_proj_columns():
    import numpy as np
    src = -np.ones((_P_END,), np.int64)
    for hq in range(NSA_HEADS):
        src[_P_QZ + hq * LANE:_P_QZ + hq * LANE + HEAD_DIM] = _O_Q + hq * HEAD_DIM + np.arange(HEAD_DIM)
    src[_P_CMP:_P_CMP + 2 * KV_W] = _O_KV + np.arange(2 * KV_W)
    src[_P_SEL:_P_SEL + 2 * KV_W] = _O_KV + 2 * KV_W + np.arange(2 * KV_W)
    src[_P_WIN:_P_WIN + 2 * KV_W] = _O_KV + 4 * KV_W + np.arange(2 * KV_W)
    for base, off in ((_P_KVS, _O_KV + 2 * KV_W), (_P_KVW, _O_KV + 4 * KV_W)):
        for g in range(NSA_KV_HEADS):
            src[base + g * LANE:base + g * LANE + HEAD_DIM] = off + g * HEAD_DIM + np.arange(HEAD_DIM)
            src[base + g * LANE + HEAD_DIM:base + (g + 1) * LANE] = off + KV_W + g * HEAD_DIM + np.arange(HEAD_DIM)
    ng = 3 * NSA_HPG
    for g in range(NSA_KV_HEADS):
        src[_P_GATE + g * LANE:_P_GATE + g * LANE + ng] = _O_GATE + g * ng + np.arange(ng)
    src[_P_RG:_P_RG + 2 * RG_WIDTH] = _O_RG + np.arange(2 * RG_WIDTH)
    src[_P_HG:_P_HG + 4 * HG_WIDTH] = _O_HG + np.arange(4 * HG_WIDTH)
    return src


def relayout_w_in(w_in):
    import numpy as np
    src = _in_proj_columns()
    cols = jnp.take(w_in, jnp.asarray(np.maximum(src, 0), jnp.int32), axis=-1)
    return jnp.where(jnp.asarray(src >= 0), cols, 0.0).astype(BF16)


def rope_tables(pos):
    import numpy as np
    half = ROPE_DIM // 2
    inv = ROPE_THETA ** (-jnp.arange(half, dtype=F32) * 2.0 / ROPE_DIM)
    ang = pos.astype(F32)[:, None] * inv
    cos, sin = jnp.cos(ang), jnp.sin(ang)
    lane = np.arange(LANE)
    d = lane % HEAD_DIM
    idx = jnp.asarray(d % half, jnp.int32)
    cos_l, sin_l = jnp.take(cos, idx, axis=1), jnp.take(sin, idx, axis=1)
    out = []
    for active in (lane < ROPE_DIM, d < ROPE_DIM):
        first = jnp.asarray(active & (d < half))
        second = jnp.asarray(active & (d >= half) & (d < ROPE_DIM))
        out += [jnp.where(first | second, cos_l, 1.0), jnp.where(first, -sin_l, 0.0), jnp.where(second, sin_l, 0.0)]
    return tuple(out)


def _rope128(x, c, s1, s2):
    return x * c + pltpu.roll(x, LANE - ROPE_DIM // 2, 1) * s1 + pltpu.roll(x, ROPE_DIM // 2, 1) * s2


def _in_proj_prompt_body(x_ref, g_ref, w_ref, cq_ref, s1q_ref, s2q_ref, ckk_ref, s1kk_ref, s2kk_ref,
                         qz_ref, cmp_ref, sel_ref, win_ref, kvs_ref, kvw_ref, gate_ref, rg_ref, hg_ref,
                         *, tiles_per_seq):
    tm = x_ref.shape[0]
    y = _rms(x_ref[...], g_ref[...]).astype(BF16)

    def mm(c0, n):
        return jnp.dot(y, w_ref[:, c0:c0 + n], preferred_element_type=F32)

    cq, s1q, s2q = cq_ref[...], s1q_ref[...], s2q_ref[...]
    ckk, s1kk, s2kk = ckk_ref[...], s1kk_ref[...], s2kk_ref[...]
    scale = HEAD_DIM ** -0.5 * LOG2E
    for hq in range(NSA_HEADS):
        qh = _rope128(mm(_P_QZ + hq * LANE, LANE), cq, s1q, s2q) * scale
        qz_ref[:, hq * LANE:(hq + 1) * LANE] = qh.astype(BF16)
    cmp_ref[...] = mm(_P_CMP, 2 * KV_W)
    for base, ref in ((_P_SEL, sel_ref), (_P_WIN, win_ref)):
        ref[:, 0:LANE] = _rope128(mm(base, LANE), ckk, s1kk, s2kk)
        ref[:, LANE:2 * LANE] = mm(base + LANE, LANE)
    row0 = (pl.program_id(0) % tiles_per_seq) * tm
    blk = (row0 + lax.broadcasted_iota(jnp.int32, (tm, 1), 0)) // SEL_BLOCK
    onehot = jnp.where(lax.broadcasted_iota(jnp.int32, (1, LANE), 1) == blk, MASK_BIG, 0.0).astype(BF16)
    for g in range(NSA_KV_HEADS):
        kvs_ref[g, :, 0:LANE] = _rope128(mm(_P_KVS + g * LANE, LANE), cq, s1q, s2q).astype(BF16)
        kvs_ref[g, :, LANE:2 * LANE] = onehot
        kvw_ref[g] = _rope128(mm(_P_KVW + g * LANE, LANE), cq, s1q, s2q).astype(BF16)
        gate_ref[g] = jax.nn.sigmoid(mm(_P_GATE + g * LANE, LANE))
    rg_ref[...] = mm(_P_RG, 2 * RG_WIDTH)
    hg_ref[...] = mm(_P_HG, 4 * HG_WIDTH)


def in_proj_prompt(x, g, w_rel, tables, seq_len, tm=512):
    m, d = x.shape
    tm = _row_tile(seq_len, tm)
    tiles_per_seq = seq_len // tm
    G = NSA_KV_HEADS
    row = lambda i: (i, 0)
    grow = lambda i: (0, i, 0)
    tab = pl.BlockSpec((tm, LANE), lambda i: (i % tiles_per_seq, 0))
    return pl.pallas_call(
        functools.partial(_in_proj_prompt_body, tiles_per_seq=tiles_per_seq),
        grid=(m // tm,),
        in_specs=[pl.BlockSpec((tm, d), row), pl.BlockSpec((1, d), lambda i: (0, 0)),
                  pl.BlockSpec((d, _P_END), lambda i: (0, 0))] + [tab] * 6,
        out_specs=[
            pl.BlockSpec((tm, NSA_HEADS * LANE), row),
            pl.BlockSpec((tm, 2 * KV_W), row), pl.BlockSpec((tm, 2 * KV_W), row), pl.BlockSpec((tm, 2 * KV_W), row),
            pl.BlockSpec((G, tm, 2 * LANE), grow), pl.BlockSpec((G, tm, LANE), grow), pl.BlockSpec((G, tm, LANE), grow),
            pl.BlockSpec((tm, 2 * RG_WIDTH), row), pl.BlockSpec((tm, 4 * HG_WIDTH), row),
        ],
        out_shape=[
            jax.ShapeDtypeStruct((m, NSA_HEADS * LANE), BF16),
            jax.ShapeDtypeStruct((m, 2 * KV_W), F32), jax.ShapeDtypeStruct((m, 2 * KV_W), F32),
            jax.ShapeDtypeStruct((m, 2 * KV_W), F32),
            jax.ShapeDtypeStruct((G, m, 2 * LANE), BF16), jax.ShapeDtypeStruct((G, m, LANE), BF16),
            jax.ShapeDtypeStruct((G, m, LANE), F32),
            jax.ShapeDtypeStruct((m, 2 * RG_WIDTH), F32), jax.ShapeDtypeStruct((m, 4 * HG_WIDTH), F32),
        ],
        compiler_params=pltpu.CompilerParams(
            dimension_semantics=("arbitrary",), vmem_limit_bytes=VMEM_LIMIT),
        name="in_proj_prompt",
    )(x, g.reshape(1, d), w_rel, *tables)


def compress_weights(cmp_pos, cmp_w1, cmp_b1, cmp_w2, cmp_b2):
    L = cmp_w1.shape[0]
    G, S, HD, CH = NSA_KV_HEADS, CMP_STRIDE, HEAD_DIM, CMP_HIDDEN
    w1 = cmp_w1.reshape(L, 2, CMP_RATIO, S, HD, CH)
    z = jnp.zeros_like(w1)
    w1bd = jnp.concatenate([jnp.concatenate([w1, z], axis=-1), jnp.concatenate([z, w1], axis=-1)], axis=-2)
    pe = jnp.tile(cmp_pos.reshape(L, 2, CMP_RATIO, S, HD), (1, 1, 1, 1, G))
    b1 = jnp.tile(cmp_b1, (1, 1, G)).reshape(L, 2, 1, G * CH)
    zc = jnp.zeros((L, CH, HD), F32)
    w2 = jnp.stack([jnp.concatenate([cmp_w2[:, 0], zc], axis=-1), jnp.concatenate([zc, cmp_w2[:, 1]], axis=-1)], axis=1)
    b2 = jnp.concatenate([cmp_b2[:, 0], cmp_b2[:, 1]], axis=-1).reshape(L, 1, 2 * HD)
    return w1bd.astype(BF16), pe, b1, w2.astype(BF16), b2


def _compress_body(xk_ref, xv_ref, pe_ref, w1_ref, b1_ref, w2_ref, b2_ref, c_ref, s1_ref, s2_ref, o_ref, *, nch, n_cmp):
    gelus = []
    for kv, x_ref in enumerate((xk_ref, xv_ref)):
        acc = [jnp.zeros((nch, NSA_KV_HEADS * CMP_HIDDEN), F32) for _ in range(CMP_RATIO)]
        for s in range(CMP_STRIDE):
            xs = x_ref[pl.ds(s, nch, stride=CMP_STRIDE), :]
            for r in range(CMP_RATIO):
                acc[r] = acc[r] + jnp.dot((xs + pe_ref[kv, r, s:s + 1, :]).astype(BF16), w1_ref[kv, r, s],
                                          preferred_element_type=F32)
        h = b1_ref[kv] + acc[0] + pltpu.roll(acc[1], nch - 1, 0)
        gelus.append(jax.nn.gelu(h).astype(BF16))
    valid = lax.broadcasted_iota(jnp.int32, (nch, 1), 0) < n_cmp
    for g in range(NSA_KV_HEADS):
        cols = slice(g * CMP_HIDDEN, (g + 1) * CMP_HIDDEN)
        out = (jnp.dot(gelus[0][:, cols], w2_ref[0], preferred_element_type=F32)
               + jnp.dot(gelus[1][:, cols], w2_ref[1], preferred_element_type=F32) + b2_ref[...])
        out = _rope128(out, c_ref[...], s1_ref[...], s2_ref[...])
        o_ref[g] = jnp.where(valid, out, 0.0).astype(BF16)


def compress(raw, cw, tables_q, seq_len):
    w1bd, pe, b1, w2, b2 = cw
    m = raw.shape[0]
    nb = m // seq_len
    nch = seq_len // CMP_STRIDE
    n_cmp = nch - CMP_RATIO + 1
    full = lambda shape: pl.BlockSpec(shape, lambda b: (0,) * len(shape))
    return pl.pallas_call(
        functools.partial(_compress_body, nch=nch, n_cmp=n_cmp),
        grid=(nb,),
        in_specs=[pl.BlockSpec((seq_len, KV_W), lambda b: (b, 0)), pl.BlockSpec((seq_len, KV_W), lambda b: (b, 1)),
                  full(pe.shape), full(w1bd.shape), full(b1.shape), full(w2.shape), full(b2.shape),
                  full((nch, LANE)), full((nch, LANE)), full((nch, LANE))],
        out_specs=pl.BlockSpec((None, NSA_KV_HEADS, nch, LANE), lambda b: (b, 0, 0, 0)),
        out_shape=jax.ShapeDtypeStruct((nb, NSA_KV_HEADS, nch, LANE), BF16),
        compiler_params=pltpu.CompilerParams(
            dimension_semantics=("arbitrary",), vmem_limit_bytes=VMEM_LIMIT),
        name="compress_kv",
    )(raw, raw, pe, w1bd, b1, w2, b2, *tables_q)


_NT = (((1,), (1,)), ((), ()))


def _masked_softmax2(s, mask):
    s = jnp.where(mask, s, -1e30)
    m = jnp.max(s, axis=-1, keepdims=True)
    e = jnp.where(mask, jnp.exp2(s - m), 0.0)
    return e / jnp.maximum(jnp.sum(e, axis=-1, keepdims=True), 1e-30)


def _nsa_prompt_body(qz_ref, gate_ref, kvc_ref, smap_ref, kvs_ref, kvw_ref, o_ref, *, tk):
    QB, H = Q_BLOCK, NSA_HPG
    q0 = pl.program_id(2) * QB
    q4 = jnp.concatenate([qz_ref[:, h * LANE:(h + 1) * LANE] for h in range(H)], axis=0)
    t_col = q0 + (lax.broadcasted_iota(jnp.int32, (H * QB, 1), 0) & (QB - 1))

    kvc = kvc_ref[...]
    ncp = kvc.shape[0]
    s_c = lax.dot_general(q4, kvc, _NT, preferred_element_type=F32)
    c_end = lax.broadcasted_iota(jnp.int32, (1, ncp), 1) * CMP_STRIDE + (CMP_BLOCK - 1)
    p_c = _masked_softmax2(s_c, c_end <= t_col)
    o_c = jnp.dot(p_c.astype(BF16), kvc, preferred_element_type=F32)

    psum = p_c[0:QB]
    for h in range(1, H):
        psum = psum + p_c[h * QB:(h + 1) * QB]
    hi = psum.astype(BF16)
    lo = (psum - hi.astype(F32)).astype(BF16)
    smap = smap_ref[...]
    imp = jnp.dot(hi, smap, preferred_element_type=F32) + jnp.dot(lo, smap, preferred_element_type=F32)

    jf = lax.broadcasted_iota(jnp.int32, (QB, LANE), 1)
    t_tok = q0 + lax.broadcasted_iota(jnp.int32, (QB, 1), 0)
    cur = t_tok // SEL_BLOCK
    valid = jf * SEL_BLOCK <= t_tok
    forced = (jf == 0) | (jf == cur) | (jf == cur - 1)
    score = jnp.where(valid, jnp.where(forced, FORCE_SCORE, imp), -FORCE_SCORE)

    def pick_one(_, carry):
        score, chosen = carry
        hit = jf == jnp.argmax(score, axis=1, keepdims=True)
        return jnp.where(hit, -jnp.inf, score), jnp.where(hit, 1.0, chosen)

    _, chosen = lax.fori_loop(0, SEL_TOPK, pick_one, (score, jnp.zeros((QB, LANE), F32)), unroll=True)
    selm1 = (jnp.where(valid, chosen, 0.0) - 1.0).astype(BF16)
    q_aug = jnp.concatenate([q4, jnp.concatenate([selm1] * H, axis=0)], axis=1)

    NCHAIN = 1
    RC = H * QB // NCHAIN

    def tile_step(kt, carry, causal=False):
        tile = kvs_ref[pl.ds(pl.multiple_of(kt * tk, tk), tk), :]
        out = []
        for c in range(NCHAIN):
            m, l, acc = carry[c]
            s = lax.dot_general(q_aug[c * RC:(c + 1) * RC], tile, _NT, preferred_element_type=F32)
            if causal:
                kpos = kt * tk + lax.broadcasted_iota(jnp.int32, (1, tk), 1)
                s = jnp.where(kpos <= t_col[c * RC:(c + 1) * RC], s, -1e30)
            m_new = jnp.maximum(m, jnp.max(s, axis=1, keepdims=True))
            alpha = jnp.exp2(m - m_new)
            p = jnp.exp2(s - m_new)
            l = alpha * l + jnp.sum(p, axis=1, keepdims=True)
            out.append((m_new, l, alpha * acc + jnp.dot(p.astype(BF16), tile, preferred_element_type=F32)))
        return tuple(out)

    n_full = q0 // tk
    init = tuple((jnp.full((RC, 1), -1e30, F32), jnp.zeros((RC, 1), F32), jnp.zeros((RC, 2 * LANE), F32))
                 for _ in range(NCHAIN))
    carry = lax.fori_loop(0, n_full, tile_step, init)
    carry = tile_step(n_full, carry, causal=True)
    acc_s = jnp.concatenate([c[2] for c in carry], axis=0)
    o_s = acc_s[:, 0:LANE] / jnp.concatenate([c[1] for c in carry], axis=0)

    wlen = WINDOW + QB
    w0 = pl.multiple_of(jnp.maximum(q0 - WINDOW, 0), QB)
    wt = kvw_ref[pl.ds(w0, wlen), :]
    s_w = lax.dot_general(q4, wt, _NT, preferred_element_type=F32)
    dist = t_col - (w0 + lax.broadcasted_iota(jnp.int32, (1, wlen), 1))
    p_w = _masked_softmax2(s_w, (dist >= 0) & (dist < WINDOW))
    o_w = jnp.dot(p_w.astype(BF16), wt, preferred_element_type=F32)

    comb = []
    for h in range(H):
        r = slice(h * QB, (h + 1) * QB)
        comb.append(gate_ref[:, 3 * h:3 * h + 1] * o_c[r] + gate_ref[:, 3 * h + 1:3 * h + 2] * o_s[r]
                    + gate_ref[:, 3 * h + 2:3 * h + 3] * o_w[r])
    low = lax.broadcasted_iota(jnp.int32, (1, LANE), 1) < HEAD_DIM
    for p in range(H // 2):
        o_ref[:, p * LANE:(p + 1) * LANE] = jnp.where(low, pltpu.roll(comb[2 * p], HEAD_DIM, 1), comb[2 * p + 1])


def nsa_prompt(qz, gate, kvc, kvs, kvw, nb, seq_len, tk=512):
    m = qz.shape[0]
    G, QB = NSA_KV_HEADS, Q_BLOCK
    nqb = seq_len // QB
    ncp = kvc.shape[2]
    assert seq_len % tk == 0 and seq_len >= WINDOW + QB and seq_len // SEL_BLOCK <= LANE
    import numpy as np
    c0 = np.arange(ncp)[:, None] * CMP_STRIDE
    s0 = np.arange(LANE)[None, :] * SEL_BLOCK
    n_cmp = ncp - CMP_RATIO + 1
    ov = (c0 < s0 + SEL_BLOCK) & (c0 + CMP_BLOCK > s0) & (np.arange(ncp)[:, None] < n_cmp)
    smap = jnp.asarray(ov, BF16)
    return pl.pallas_call(
        functools.partial(_nsa_prompt_body, tk=tk),
        grid=(nb, G, nqb),
        in_specs=[
            pl.BlockSpec((QB, NSA_HPG * LANE), lambda b, g, q: (b * nqb + q, g)),
            pl.BlockSpec((None, QB, LANE), lambda b, g, q: (g, b * nqb + q, 0)),
            pl.BlockSpec((None, None, ncp, LANE), lambda b, g, q: (b, g, 0, 0)),
            pl.BlockSpec((ncp, LANE), lambda b, g, q: (0, 0)),
            pl.BlockSpec((None, seq_len, 2 * LANE), lambda b, g, q: (g, b, 0)),
            pl.BlockSpec((None, seq_len, LANE), lambda b, g, q: (g, b, 0)),
        ],
        out_specs=pl.BlockSpec((QB, NSA_HPG * HEAD_DIM), lambda b, g, q: (b * nqb + q, g)),
        out_shape=jax.ShapeDtypeStruct((m, NSA_WIDTH), F32),
        compiler_params=pltpu.CompilerParams(
            dimension_semantics=("arbitrary", "arbitrary", "arbitrary"), vmem_limit_bytes=VMEM_LIMIT),
        name="nsa_prompt",
    )(qz, gate, kvc, smap, kvs, kvw)


def _shift_rows(x, prev8, d):
    rolled = pltpu.roll(x, d, 0)
    row8 = lax.broadcasted_iota(jnp.int32, (8, 1), 0)
    first = jnp.where(row8 < d, pltpu.roll(prev8, d, 0), rolled[0:8])
    return jnp.concatenate([first, rolled[8:]], axis=0)


def _rglru_body(rg_ref, buf_ref, h0_ref, cw_ref, cb_ref, wg_ref, bg_ref, lam_ref,
                o_ref, hout_ref, bufout_ref, tail_ref, h_ref):
    tm = rg_ref.shape[0]
    W = RG_WIDTH

    @pl.when(pl.program_id(1) == 0)
    def _():
        tail_ref[...] = buf_ref[...]
        h_ref[...] = h0_ref[...]

    x = rg_ref[:, 0:W]
    gr = rg_ref[:, W:2 * W]
    prev8 = tail_ref[...]
    xc = cb_ref[...] + cw_ref[CONV_W - 1:CONV_W, :] * x
    for d in range(1, CONV_W):
        xc = xc + cw_ref[CONV_W - 1 - d:CONV_W - d, :] * _shift_rows(x, prev8, d)
    gates = jnp.dot(xc.astype(BF16), wg_ref[...], preferred_element_type=F32) + bg_ref[...]
    r = jax.nn.sigmoid(gates[:, 0:W])
    i = jax.nn.sigmoid(gates[:, W:2 * W])
    log_a = -RG_C * r * jax.nn.softplus(-lam_ref[...])
    a = jnp.exp(log_a)
    th = jnp.tanh(log_a)
    b = jnp.sqrt(-2.0 * th / (1.0 - th)) * (i * xc)
    row = lax.broadcasted_iota(jnp.int32, (tm, 1), 0)
    d = 1
    while d < tm:
        keep = row >= d
        a_sh = jnp.where(keep, pltpu.roll(a, d, 0), 1.0)
        b_sh = jnp.where(keep, pltpu.roll(b, d, 0), 0.0)
        b = a * b_sh + b
        a = a * a_sh
        d *= 2
    h = a * h_ref[7:8, :] + b
    o_ref[...] = h * jax.nn.gelu(gr)
    h_ref[...] = h[tm - 8:tm]
    tail_ref[...] = x[tm - 8:tm]
    hout_ref[...] = h[tm - 8:tm]
    bufout_ref[...] = x[tm - 8:tm]


def rglru_prompt(rg, conv_buf, h0, conv_w, conv_b, wa, ba, wx, bx, lam, seq_len, tm=256):
    m = rg.shape[0]
    nb = m // seq_len
    W = RG_WIDTH
    tm = _row_tile(seq_len, tm)
    nt = seq_len // tm
    bd = jax.scipy.linalg.block_diag
    wg = jnp.concatenate([bd(*[wa[n] for n in range(RG_BLOCKS)]), bd(*[wx[n] for n in range(RG_BLOCKS)])], axis=1).astype(BF16)
    bg = jnp.concatenate([ba, bx]).reshape(1, 2 * W)
    buf8 = jnp.pad(conv_buf, ((0, 0), (8 - (CONV_W - 1), 0), (0, 0)))
    h08 = jnp.broadcast_to(h0[:, None, :], (nb, 8, W))
    const = lambda shape: pl.BlockSpec(shape, lambda b, t: (0,) * len(shape))
    per_b = pl.BlockSpec((None, 8, W), lambda b, t: (b, 0, 0))
    out, h8, nbuf8 = pl.pallas_call(
        _rglru_body,
        grid=(nb, nt),
        in_specs=[pl.BlockSpec((tm, 2 * W), lambda b, t: (b * nt + t, 0)), per_b, per_b,
                  const((CONV_W, W)), const((1, W)), const((W, 2 * W)), const((1, 2 * W)), const((1, W))],
        out_specs=[pl.BlockSpec((tm, W), lambda b, t: (b * nt + t, 0)), per_b, per_b],
        out_shape=[jax.ShapeDtypeStruct((m, W), F32), jax.ShapeDtypeStruct((nb, 8, W), F32),
                   jax.ShapeDtypeStruct((nb, 8, W), F32)],
        scratch_shapes=[pltpu.VMEM((8, W), F32), pltpu.VMEM((8, W), F32)],
        compiler_params=pltpu.CompilerParams(dimension_semantics=("arbitrary", "arbitrary")),
        name="rglru_prompt",
    )(rg, buf8, h08, conv_w, conv_b.reshape(1, W), wg, bg, lam.reshape(1, W))
    return out, h8[:, 7], nbuf8[:, 8 - (CONV_W - 1):]


def _split_dot(x, w_bf16, pieces):
    acc = None
    for _ in range(pieces):
        xb = x.astype(BF16)
        part = jnp.dot(xb, w_bf16, preferred_element_type=F32)
        acc = part if acc is None else acc + part
        x = x - xb.astype(F32)
    return acc


HG_SAFE_DECAY = 75.0


def _hgrn_body(hg_ref, s0_ref, lb_ref, gain_ref, tri_ref, tribd_ref, ones_ref, o_ref, sout_ref, st_ref, *, chunk):
    tm = hg_ref.shape[0]
    W, H = HG_WIDTH, HG_HEADS
    C = chunk
    nck = tm // C

    @pl.when(pl.program_id(1) == 0)
    def _():
        st_ref[...] = s0_ref[...]

    lb = lb_ref[...]
    ones_blk = ones_ref[...]
    same_head = ones_blk > 0

    def finish(o, rows):
        ms = _split_dot(o * o, ones_blk, 2) * (1.0 / HG_DV)
        o = o * lax.rsqrt(ms + EPS) * gain_ref[...]
        o_ref[rows, :] = o * jax.nn.silu(hg_ref[rows, 3 * W:4 * W])

    q = jax.nn.silu(hg_ref[:, 0:W])
    f = lb + (1.0 - lb) * jax.nn.sigmoid(hg_ref[:, W:2 * W])
    v = hg_ref[:, 2 * W:3 * W]
    k = 1.0 - f
    bcum = _split_dot_left(tribd_ref[...], jnp.log(f))
    b_last = [bcum[(c + 1) * C - 1:(c + 1) * C, :] for c in range(nck)]
    total = b_last[0]
    for c in range(1, nck):
        total = jnp.minimum(total, b_last[c])
    safe = jnp.min(total) >= -HG_SAFE_DECAY

    @pl.when(safe)
    def _():
        qt = q * jnp.exp(bcum)
        kt = (k * jnp.exp(-bcum)).astype(BF16)
        vb = v.astype(BF16)
        lane_head = lax.broadcasted_iota(jnp.int32, (1, W), 1) // HG_DV
        q_exp = jnp.concatenate([jnp.where(lane_head == h, qt, 0.0) for h in range(H)], axis=0).astype(BF16)
        a = lax.dot_general(q_exp, kt, _NT, preferred_element_type=F32)
        t_row = lax.broadcasted_iota(jnp.int32, (H * tm, 1), 0) & (tm - 1)
        s_col = lax.broadcasted_iota(jnp.int32, (1, tm), 1)
        pair = (s_col <= t_row) & ((s_col // C) == (t_row // C))
        o_exp = jnp.dot(jnp.where(pair, a, 0.0).astype(BF16), vb, preferred_element_type=F32)
        o = jnp.where(lane_head == 0, o_exp[0:tm], 0.0)
        for h in range(1, H):
            o = o + jnp.where(lane_head == h, o_exp[h * tm:(h + 1) * tm], 0.0)
        st = st_ref[...]
        inter = []
        for c in range(nck):
            r = slice(c * C, (c + 1) * C)
            inter.append(lax.dot_general(qt[r].astype(BF16), st.astype(BF16), _NT, preferred_element_type=F32))
            k_hat = (k[r] * jnp.exp(b_last[c] - bcum[r])).astype(BF16)
            upd = lax.dot_general(vb[r], k_hat, (((0,), (0,)), ((), ())), preferred_element_type=F32)
            st = jnp.exp(b_last[c]) * st + jnp.where(same_head, upd, 0.0)
        st_ref[...] = st
        finish(o + jnp.concatenate(inter, axis=0), slice(None))

    @pl.when(jnp.logical_not(safe))
    def _():
        tri = tri_ref[...]
        rowc = lax.broadcasted_iota(jnp.int32, (C, 1), 0)

        def chunk_step(c, _):
            r0 = pl.multiple_of(c * C, C)
            rows = pl.ds(r0, C)
            qc = jax.nn.silu(hg_ref[rows, 0:W])
            fc = lb + (1.0 - lb) * jax.nn.sigmoid(hg_ref[rows, W:2 * W])
            vc = hg_ref[rows, 2 * W:3 * W]
            kc = 1.0 - fc
            bc = _split_dot_left(tri, jnp.log(fc))
            st = st_ref[...]
            o = lax.dot_general((qc * jnp.exp(bc)).astype(BF16), st.astype(BF16), _NT, preferred_element_type=F32)

            def offset_step(dlt, carry):
                o, b_sh, k_sh, v_sh = carry
                p = jnp.where(rowc >= dlt, qc * k_sh * jnp.exp(bc - b_sh), 0.0)
                o = o + _split_dot(p, ones_blk, 2) * v_sh
                return o, pltpu.roll(b_sh, 1, 0), pltpu.roll(k_sh, 1, 0), pltpu.roll(v_sh, 1, 0)

            o, _, _, _ = lax.fori_loop(0, C, offset_step, (o, bc, kc, vc))
            bl = bc[C - 1:C, :]
            k_hat = (kc * jnp.exp(bl - bc)).astype(BF16)
            upd = lax.dot_general(vc.astype(BF16), k_hat, (((0,), (0,)), ((), ())), preferred_element_type=F32)
            st_ref[...] = jnp.exp(bl) * st + jnp.where(same_head, upd, 0.0)
            finish(o, rows)
            return 0

        lax.fori_loop(0, nck, chunk_step, 0)

    sout_ref[...] = st_ref[...]


def _split_dot_left(w_bf16, x):
    acc = None
    for _ in range(3):
        xb = x.astype(BF16)
        part = jnp.dot(w_bf16, xb, preferred_element_type=F32)
        acc = part if acc is None else acc + part
        x = x - xb.astype(F32)
    return acc


def hgrn_prompt(hg, s0, lb, gain, seq_len, tm=256, chunk=HG_CHUNK):
    import numpy as np
    m = hg.shape[0]
    nb = m // seq_len
    W, H = HG_WIDTH, HG_HEADS
    tm = _row_tile(seq_len, tm)
    nt = seq_len // tm
    head = np.arange(W) // HG_DV
    ones_blk = jnp.asarray(head[:, None] == head[None, :], BF16)
    tri = jnp.asarray(np.tril(np.ones((chunk, chunk))), BF16)
    rt = np.arange(tm)
    tri_bd = jnp.asarray((rt[None, :] <= rt[:, None]) & (rt[None, :] // chunk == rt[:, None] // chunk), BF16)
    s0_t = jnp.einsum('bhkv,hg->bhvgk', s0, jnp.eye(H, dtype=s0.dtype)).reshape(nb, W, W)
    const = lambda shape: pl.BlockSpec(shape, lambda b, t: (0,) * len(shape))
    per_b = pl.BlockSpec((None, W, W), lambda b, t: (b, 0, 0))
    out, s_t = pl.pallas_call(
        functools.partial(_hgrn_body, chunk=chunk),
        grid=(nb, nt),
        in_specs=[pl.BlockSpec((tm, 4 * W), lambda b, t: (b * nt + t, 0)), per_b,
                  const((1, W)), const((1, W)), const((chunk, chunk)), const((tm, tm)), const((W, W))],
        out_specs=[pl.BlockSpec((tm, W), lambda b, t: (b * nt + t, 0)), per_b],
        out_shape=[jax.ShapeDtypeStruct((m, W), F32), jax.ShapeDtypeStruct((nb, W, W), F32)],
        scratch_shapes=[pltpu.VMEM((W, W), F32)],
        compiler_params=pltpu.CompilerParams(dimension_semantics=("arbitrary", "arbitrary")),
        name="hgrn_prompt",
    )(hg, s0_t, lb.reshape(1, W), gain.reshape(1, W), tri, tri_bd, ones_blk)
    s5 = s_t.reshape(nb, H, HG_DV, H, HG_DK)
    s_fin = jnp.stack([s5[:, h, :, h, :] for h in range(H)], axis=1).transpose(0, 1, 3, 2)
    return out, s_fin


def _nsa_sample_body(pt_ref, *refs, n_pages, past_len, n_win):
    del pt_ref
    np2 = 2 * n_pages
    cmp_refs, sel_refs = refs[0:np2], refs[np2:2 * np2]
    (win_ref, selnew_ref, winnew_ref, qlo_ref, qq_ref, gate_ref, pe_ref, w1_ref, b1_ref, w2_ref, b2_ref,
     c_ref, s1_ref, s2_ref, smap_ref, expand_ref, o_ref, winout_ref) = refs[2 * np2:]
    G, H, CH = NSA_KV_HEADS, NSA_HPG, CMP_HIDDEN
    R = G * H
    nch = n_pages * (PAGE_SIZE // CMP_STRIDE)
    rows_pp = PAGE_SIZE // CMP_STRIDE
    n_cmp = nch - CMP_RATIO + 1
    t = past_len
    row = lax.broadcasted_iota(jnp.int32, (R, 1), 0)
    grp0 = row < H
    lane = lax.broadcasted_iota(jnp.int32, (1, LANE), 1)

    gelus = []
    for kv in range(2):
        acc = [jnp.zeros((nch, G * CH), F32) for _ in range(CMP_RATIO)]
        for sp in range(CMP_STRIDE // 2):
            halves = []
            for s in (2 * sp, 2 * sp + 1):
                halves.append(jnp.concatenate(
                    [cmp_refs[2 * p + kv][pl.ds(s, rows_pp, stride=CMP_STRIDE), :] for p in range(n_pages)], axis=0))
            xs = jnp.concatenate(halves, axis=1)
            for r in range(CMP_RATIO):
                acc[r] = acc[r] + jnp.dot((xs + pe_ref[kv, r, sp:sp + 1, :]).astype(BF16), w1_ref[kv, r, sp],
                                          preferred_element_type=F32)
        h = b1_ref[kv] + acc[0] + pltpu.roll(acc[1], nch - 1, 0)
        gelus.append(jax.nn.gelu(h).astype(BF16))
    valid_c = lax.broadcasted_iota(jnp.int32, (nch, 1), 0) < n_cmp
    qlo = qlo_ref[...].astype(BF16)
    c_end = lax.broadcasted_iota(jnp.int32, (1, nch), 1) * CMP_STRIDE + (CMP_BLOCK - 1)
    smap = smap_ref[...]
    o_c, imp = [], []
    for g in range(G):
        cols = slice(g * CH, (g + 1) * CH)
        kvc = (jnp.dot(gelus[0][:, cols], w2_ref[0], preferred_element_type=F32)
               + jnp.dot(gelus[1][:, cols], w2_ref[1], preferred_element_type=F32) + b2_ref[...])
        kvc = jnp.where(valid_c, _rope128(kvc, c_ref[...], s1_ref[...], s2_ref[...]), 0.0).astype(BF16)
        s_c = lax.dot_general(qlo, kvc, _NT, preferred_element_type=F32)
        p_c = _masked_softmax2(s_c, c_end <= t)
        o_c.append(jnp.dot(p_c.astype(BF16), kvc, preferred_element_type=F32))
        mine = grp0 if g == 0 else jnp.logical_not(grp0)
        psum = jnp.sum(jnp.where(mine, p_c, 0.0), axis=0, keepdims=True)
        imp.append(_split_dot(jnp.broadcast_to(psum, (R, nch)), smap, 2))
    o_c = jnp.where(grp0, pltpu.roll(o_c[0], HEAD_DIM, 1), o_c[1])
    imp = jnp.where(grp0, imp[0], imp[1])

    cur = t // SEL_BLOCK
    valid = lane * SEL_BLOCK <= t
    forced = (lane == 0) | (lane == cur) | (lane == cur - 1)
    score = jnp.where(valid, jnp.where(forced, FORCE_SCORE, imp), -FORCE_SCORE)

    def pick_one(_, carry):
        score, chosen = carry
        hit = lane == jnp.argmax(score, axis=1, keepdims=True)
        return jnp.where(hit, -jnp.inf, score), jnp.where(hit, 1.0, chosen)

    _, chosen = lax.fori_loop(0, SEL_TOPK, pick_one, (score, jnp.zeros((R, LANE), F32)), unroll=True)
    selm1 = (jnp.where(valid, chosen, 0.0) - 1.0).astype(BF16)
    bias = jnp.dot(selm1, expand_ref[...], preferred_element_type=F32)

    qq32 = qq_ref[...]
    qq = qq32.astype(BF16)
    sel_k = [sel_refs[2 * p][...].astype(BF16) for p in range(n_pages)]
    s_s = jnp.concatenate([lax.dot_general(qq, kp, _NT, preferred_element_type=F32) for kp in sel_k], axis=1) + bias
    s_new = jnp.sum(qq32 * selnew_ref[:, 0:LANE], axis=1, keepdims=True)
    m = jnp.maximum(jnp.max(s_s, axis=1, keepdims=True), s_new)
    p = jnp.exp2(s_s - m)
    p_new = jnp.exp2(s_new - m)
    denom = jnp.sum(p, axis=1, keepdims=True) + p_new
    pb = p.astype(BF16)
    o_s = p_new * selnew_ref[:, LANE:2 * LANE]
    for pg in range(n_pages):
        o_s = o_s + jnp.dot(pb[:, pg * PAGE_SIZE:(pg + 1) * PAGE_SIZE], sel_refs[2 * pg + 1][...].astype(BF16),
                            preferred_element_type=F32)
    o_s = o_s / denom

    win = win_ref[...]
    s_w = lax.dot_general(qq, win[:, 0:LANE].astype(BF16), _NT, preferred_element_type=F32)
    col = lax.broadcasted_iota(jnp.int32, (1, n_win), 1)
    w_ok = (n_win - col < WINDOW) & (t - n_win + col >= 0)
    s_w = jnp.where(w_ok, s_w, -1e30)
    s_wn = jnp.sum(qq32 * winnew_ref[:, 0:LANE], axis=1, keepdims=True)
    m = jnp.maximum(jnp.max(s_w, axis=1, keepdims=True), s_wn)
    p = jnp.where(w_ok, jnp.exp2(s_w - m), 0.0)
    p_new = jnp.exp2(s_wn - m)
    denom = jnp.sum(p, axis=1, keepdims=True) + p_new
    o_w = (jnp.dot(p.astype(BF16), win[:, LANE:2 * LANE].astype(BF16), preferred_element_type=F32)
           + p_new * winnew_ref[:, LANE:2 * LANE]) / denom

    o_ref[...] = gate_ref[:, 0:1] * o_c + gate_ref[:, 1:2] * o_s + gate_ref[:, 2:3] * o_w
    wrow = lax.broadcasted_iota(jnp.int32, (n_win, 1), 0)
    winout_ref[...] = jnp.where(wrow == n_win - 1, winnew_ref[...], pltpu.roll(win, n_win - 1, 0))


def nsa_sample(l, page_table, cache_cmp, cache_sel, cache_win, sel_new, win_new, qz, gate, cw):
    import numpy as np
    n, n_pages = page_table.shape
    past_len = n_pages * PAGE_SIZE
    n_win = cache_win.shape[2]
    depth, n_pool = cache_cmp.shape[:2]
    G, H = NSA_KV_HEADS, NSA_HPG
    R = G * H
    nch = past_len // CMP_STRIDE
    assert past_len // SEL_BLOCK + 1 <= LANE and n_win % 8 == 0
    w1bd, pe, b1, w2, b2 = cw
    w1p = w1bd.reshape(2, CMP_RATIO, CMP_STRIDE // 2, 2 * LANE, G * CMP_HIDDEN)
    pep = pe.reshape(2, CMP_RATIO, CMP_STRIDE // 2, 2 * LANE)
    c_end = jnp.arange(nch, dtype=jnp.int32) * CMP_STRIDE + CMP_BLOCK - 1
    tabs = rope_tables(c_end)[:3]
    c0 = np.arange(nch)[:, None] * CMP_STRIDE
    s0 = np.arange(LANE)[None, :] * SEL_BLOCK
    ov = (c0 < s0 + SEL_BLOCK) & (c0 + CMP_BLOCK > s0) & (np.arange(nch)[:, None] < nch - CMP_RATIO + 1)
    smap = jnp.asarray(ov, BF16)
    expand = jnp.asarray(np.where(np.arange(past_len)[None, :] // SEL_BLOCK == np.arange(LANE)[:, None], MASK_BIG, 0.0), BF16)
    qlo = qz.reshape(n, R, LANE).astype(F32)
    qq = jnp.concatenate([qlo[:, :H], jnp.roll(qlo[:, H:], HEAD_DIM, axis=-1)], axis=1)
    g3 = jnp.pad(gate[:, :, :3 * H].reshape(G, n, H, 3).transpose(1, 0, 2, 3).reshape(n, R, 3), ((0, 0), (0, 0), (0, LANE - 3)))
    cmp4 = cache_cmp.reshape(depth, n_pool, PAGE_SIZE, 2 * KV_W)
    sel4 = cache_sel.reshape(depth, n_pool, PAGE_SIZE, 2 * KV_W)
    win4 = cache_win.reshape(depth, n, n_win, 2 * KV_W)

    def page_spec(p, half):
        return pl.BlockSpec((None, None, PAGE_SIZE, LANE), lambda b, pt: (l, pt[b, p], 0, half))

    page_specs = [page_spec(p, half) for p in range(n_pages) for half in range(2)]
    full = lambda shape: pl.BlockSpec(shape, lambda b, pt: (0,) * len(shape))
    per_seq = lambda shape: pl.BlockSpec((None,) + shape, lambda b, pt: (b,) + (0,) * len(shape))
    grid_spec = pltpu.PrefetchScalarGridSpec(
        num_scalar_prefetch=1,
        grid=(n,),
        in_specs=page_specs + page_specs + [
            pl.BlockSpec((None, None, n_win, 2 * KV_W), lambda b, pt: (l, b, 0, 0)),
            per_seq((1, 2 * KV_W)), per_seq((1, 2 * KV_W)), per_seq((R, LANE)), per_seq((R, LANE)), per_seq((R, LANE)),
            full(pep.shape), full(w1p.shape), full(b1.shape), full(w2.shape), full(b2.shape),
            full((nch, LANE)), full((nch, LANE)), full((nch, LANE)), full((nch, LANE)), full((LANE, past_len))],
        out_specs=[per_seq((R, LANE)), per_seq((n_win, 2 * KV_W))],
    )
    o8, win_out = pl.pallas_call(
        functools.partial(_nsa_sample_body, n_pages=n_pages, past_len=past_len, n_win=n_win),
        grid_spec=grid_spec,
        out_shape=[jax.ShapeDtypeStruct((n, R, LANE), F32), jax.ShapeDtypeStruct((n, n_win, 2 * KV_W), F32)],
        compiler_params=pltpu.CompilerParams(dimension_semantics=("arbitrary",), vmem_limit_bytes=VMEM_LIMIT),
        name="nsa_sample",
    )(page_table, *([cmp4] * (2 * n_pages)), *([sel4] * (2 * n_pages)), win4,
      sel_new.reshape(n, 1, 2 * KV_W), win_new.reshape(n, 1, 2 * KV_W), qlo, qq, g3,
      pep, w1p, b1, w2, b2, *tabs, smap, expand)
    o5 = o8.reshape(n, G, H, G, HEAD_DIM)
    o = jnp.stack([o5[:, g, :, g, :] for g in range(G)], axis=1).reshape(n, NSA_WIDTH)
    return o, win_out


def _rglru_sample_body(rg_ref, b0_ref, b1_ref, b2_ref, h0_ref, cw_ref, cb_ref, wg_ref, bg_ref, lam_ref, o_ref, h_ref):
    W = RG_WIDTH
    x = rg_ref[:, 0:W]
    xc = (cb_ref[...] + cw_ref[0:1, :] * b0_ref[...] + cw_ref[1:2, :] * b1_ref[...] + cw_ref[2:3, :] * b2_ref[...]
          + cw_ref[3:4, :] * x)
    gates = jnp.dot(xc.astype(BF16), wg_ref[...], preferred_element_type=F32) + bg_ref[...]
    r = jax.nn.sigmoid(gates[:, 0:W])
    i = jax.nn.sigmoid(gates[:, W:2 * W])
    log_a = -RG_C * r * jax.nn.softplus(-lam_ref[...])
    th = jnp.tanh(log_a)
    h = jnp.exp(log_a) * h0_ref[...] + jnp.sqrt(-2.0 * th / (1.0 - th)) * (i * xc)
    h_ref[...] = h
    o_ref[...] = h * jax.nn.gelu(rg_ref[:, W:2 * W])


def rglru_sample(rg, conv_buf, h0, conv_w, conv_b, wa, ba, wx, bx, lam):
    assert CONV_W == 4
    n = rg.shape[0]
    W = RG_WIDTH
    bd = jax.scipy.linalg.block_diag
    wg = jnp.concatenate([bd(*[wa[k] for k in range(RG_BLOCKS)]), bd(*[wx[k] for k in range(RG_BLOCKS)])], axis=1).astype(BF16)
    bg = jnp.concatenate([ba, bx]).reshape(1, 2 * W)
    out, h = pl.pallas_call(
        _rglru_sample_body,
        out_shape=[jax.ShapeDtypeStruct((n, W), F32), jax.ShapeDtypeStruct((n, W), F32)],
        name="rglru_sample",
    )(rg, conv_buf[:, 0], conv_buf[:, 1], conv_buf[:, 2], h0, conv_w, conv_b.reshape(1, W), wg, bg, lam.reshape(1, W))
    new_buf = jnp.concatenate([conv_buf[:, 1:], rg[:, None, 0:W]], axis=1)
    return out, h, new_buf


def _hgrn_sample_body(q_ref, f_ref, lb_ref, v_ref, g_ref, gain_ref, s0_ref, o_ref, s_ref):
    DK = HG_DK
    q = jax.nn.silu(q_ref[...])
    lb = lb_ref[...]
    f = lb + (1.0 - lb) * jax.nn.sigmoid(f_ref[...])
    k = 1.0 - f
    outs = []
    for h in range(HG_HEADS):
        r = slice(h * DK, (h + 1) * DK)
        s_new = f[r] * s0_ref[h] + k[r] * v_ref[h:h + 1, :]
        s_ref[h] = s_new
        outs.append(jnp.sum(q[r] * s_new, axis=0, keepdims=True))
    o = jnp.concatenate(outs, axis=0)
    o = o * lax.rsqrt(jnp.mean(o * o, axis=-1, keepdims=True) + EPS) * gain_ref[...]
    o_ref[...] = o * jax.nn.silu(g_ref[...])


def hgrn_sample(hg, s0, lb, gain):
    n = hg.shape[0]
    H, DK, DV, W = HG_HEADS, HG_DK, HG_DV, HG_WIDTH
    col = lambda a: a.reshape(n, W, 1)
    per_seq = lambda shape: pl.BlockSpec((None,) + shape, lambda b: (b,) + (0,) * len(shape))
    full = lambda shape: pl.BlockSpec(shape, lambda b: (0,) * len(shape))
    o, s_new = pl.pallas_call(
        _hgrn_sample_body,
        grid=(n,),
        in_specs=[per_seq((W, 1)), per_seq((W, 1)), full((W, 1)), per_seq((H, DV)), per_seq((H, DV)), full((H, DV)),
                  per_seq((H, DK, DV))],
        out_specs=[per_seq((H, DV)), per_seq((H, DK, DV))],
        out_shape=[jax.ShapeDtypeStruct((n, H, DV), F32), jax.ShapeDtypeStruct((n, H, DK, DV), F32)],
        compiler_params=pltpu.CompilerParams(dimension_semantics=("arbitrary",)),
        name="hgrn_sample",
    )(col(hg[:, 0:W]), col(hg[:, W:2 * W]), lb.reshape(W, 1), hg[:, 2 * W:3 * W].reshape(n, H, DV),
      hg[:, 3 * W:4 * W].reshape(n, H, DV), gain.reshape(H, DV), s0)
    return o.reshape(n, W), s_new


def _split(a, sizes):
    out, o = [], 0
    for s in sizes:
        out.append(a[..., o:o + s])
        o += s
    return out


def rms_norm(x, g):
    x32 = x.astype(jnp.float32)
    y = x32 * lax.rsqrt(jnp.mean(x32 * x32, axis=-1, keepdims=True) + EPS)
    return (y * g.astype(jnp.float32)).astype(x.dtype)


def partial_rope(x, pos):
    half = ROPE_DIM // 2
    inv = ROPE_THETA ** (-jnp.arange(half, dtype=jnp.float32) * 2.0 / ROPE_DIM)
    ang = pos.astype(jnp.float32)[:, None] * inv
    cos = jnp.cos(ang)[:, None, :].astype(x.dtype)
    sin = jnp.sin(ang)[:, None, :].astype(x.dtype)
    x1, x2, rest = x[..., :half], x[..., half:ROPE_DIM], x[..., ROPE_DIM:]
    return jnp.concatenate([x1 * cos - x2 * sin, x2 * cos + x1 * sin, rest], axis=-1)


def masked_softmax(s, mask):
    s = jnp.where(mask, s, -1e30)
    m = jnp.max(s, axis=-1, keepdims=True)
    e = jnp.where(mask, jnp.exp(s - m), 0.0)
    return e / jnp.maximum(jnp.sum(e, axis=-1, keepdims=True), 1e-30)


def compress_kv(raw, pos_emb, w1, b1, w2, b2):
    B, L = raw.shape[0], raw.shape[1]
    nch = L // CMP_STRIDE
    n_cmp = nch - CMP_RATIO + 1
    chunks = raw[:, :nch * CMP_STRIDE].reshape(B, nch, CMP_STRIDE, NSA_KV_HEADS, HEAD_DIM)
    h = b1
    for r in range(CMP_RATIO):
        sl = slice(r * CMP_STRIDE, (r + 1) * CMP_STRIDE)
        pre = jnp.einsum('bnsgd,sdh->bngh', chunks + pos_emb[sl][:, None, :], w1[sl])
        h = h + pre[:, r:r + n_cmp]
    out = jax.nn.gelu(h) @ w2 + b2
    end = jnp.arange(n_cmp, dtype=jnp.int32) * CMP_STRIDE + CMP_BLOCK - 1
    return out, end


def selection_map(n_cmp, n_sel):
    c0 = jnp.arange(n_cmp) * CMP_STRIDE
    s0 = jnp.arange(n_sel) * SEL_BLOCK
    ov = (c0[:, None] < s0[None, :] + SEL_BLOCK) & (c0[:, None] + CMP_BLOCK > s0[None, :])
    return ov.astype(jnp.float32)


def nsa_attend_block(q, gate, t_pos, kw, vw, w_pos, k_cmp, v_cmp, c_end, ks_blk, vs_blk, smap):
    f32 = jnp.float32
    scale = HEAD_DIM ** -0.5
    s_c = jnp.einsum('bqghd,bngd->bqghn', q, k_cmp).astype(f32) * scale
    m_c = (c_end[None, :] <= t_pos[:, None])[None, :, None, None, :]
    p_c = masked_softmax(s_c, m_c)
    o_c = jnp.einsum('bqghn,bngd->bqghd', p_c.astype(v_cmp.dtype), v_cmp)
    imp = jnp.einsum('bqghn,nj->bqgj', p_c, smap)
    n_sel = ks_blk.shape[1]
    j = jnp.arange(n_sel)[None, :]
    cur = (t_pos // SEL_BLOCK)[:, None]
    valid = j * SEL_BLOCK <= t_pos[:, None]
    forced = (j == 0) | (j == cur) | (j == cur - 1)
    score = jnp.where(valid[None, :, None, :], jnp.where(forced[None, :, None, :], FORCE_SCORE, imp), -FORCE_SCORE)
    _, idx = lax.top_k(score, min(SEL_TOPK, n_sel))
    bi = jnp.arange(q.shape[0])[:, None, None, None]
    gi = jnp.arange(NSA_KV_HEADS)[None, None, :, None]
    kb = ks_blk[bi, idx, gi]
    vb = vs_blk[bi, idx, gi]
    kpos = idx[..., None] * SEL_BLOCK + jnp.arange(SEL_BLOCK)
    m_s = (kpos <= t_pos[None, :, None, None, None])[:, :, :, None]
    s_s = jnp.einsum('bqghd,bqgkld->bqghkl', q, kb).astype(f32) * scale
    shp = s_s.shape
    p_s = masked_softmax(s_s.reshape(*shp[:4], -1), m_s.reshape(*m_s.shape[:4], -1)).reshape(shp)
    o_s = jnp.einsum('bqghkl,bqgkld->bqghd', p_s.astype(vb.dtype), vb)
    s_w = jnp.einsum('bqghd,bwgd->bqghw', q, kw).astype(f32) * scale
    dist = t_pos[:, None] - w_pos[None, :]
    m_w = ((dist >= 0) & (dist < WINDOW) & (w_pos[None, :] >= 0))[None, :, None, None, :]
    p_w = masked_softmax(s_w, m_w)
    o_w = jnp.einsum('bqghw,bwgd->bqghd', p_w.astype(vw.dtype), vw)
    return gate[..., 0:1] * o_c + gate[..., 1:2] * o_s + gate[..., 2:3] * o_w


def nsa_mixer(q, kvs, gate, pos, cmp_pos, cmp_w1, cmp_b1, cmp_w2, cmp_b2, past):
    B, T = q.shape[:2]
    G, HD = NSA_KV_HEADS, HEAD_DIM
    q = partial_rope(q.reshape(B, T, NSA_HEADS, HD), pos).reshape(B, T, G, NSA_HPG, HD)
    kc, vc, ks, vs, kw, vw = [a.reshape(B, T, G, HD) for a in _split(kvs, (KV_W,) * 6)]
    ks = partial_rope(ks, pos)
    kw = partial_rope(kw, pos)
    gate = jax.nn.sigmoid(gate.reshape(B, T, G, NSA_HPG, 3))
    new_cmp = jnp.stack([kc, vc], axis=2)
    new_sel = jnp.stack([ks, vs], axis=2)
    new_win = jnp.stack([kw, vw], axis=2)
    if past is None:
        cmp_all, sel_all = new_cmp, new_sel
        win_all = jnp.pad(new_win, ((0, 0), (WINDOW, 0), (0, 0), (0, 0), (0, 0)))
        w_pos = pos[0] - WINDOW + jnp.arange(T + WINDOW, dtype=jnp.int32)
        win_state = new_win[:, T - min(WINDOW, T):]
    else:
        cmp_past, sel_past, win_buf = past
        cmp_all = jnp.concatenate([cmp_past, new_cmp], axis=1)
        sel_all = jnp.concatenate([sel_past, new_sel], axis=1)
        win_all = jnp.concatenate([win_buf, new_win], axis=1)
        nb = win_buf.shape[1]
        w_pos = pos[0] - nb + jnp.arange(nb + T, dtype=jnp.int32)
        win_state = win_all[:, T:]
    L = cmp_all.shape[1]
    k_cmp, c_end = compress_kv(cmp_all[:, :, 0], cmp_pos[0], cmp_w1[0], cmp_b1[0], cmp_w2[0], cmp_b2[0])
    v_cmp, _ = compress_kv(cmp_all[:, :, 1], cmp_pos[1], cmp_w1[1], cmp_b1[1], cmp_w2[1], cmp_b2[1])
    k_cmp = partial_rope(k_cmp, c_end)
    n_sel = -(-L // SEL_BLOCK)
    sel_pad = jnp.pad(sel_all, ((0, 0), (0, n_sel * SEL_BLOCK - L), (0, 0), (0, 0), (0, 0)))
    sel_blk = sel_pad.reshape(B, n_sel, SEL_BLOCK, 2, G, HD).transpose(3, 0, 1, 4, 2, 5)
    smap = selection_map(k_cmp.shape[1], n_sel)
    kw_all, vw_all = win_all[:, :, 0], win_all[:, :, 1]
    if past is None and T > Q_BLOCK and T % Q_BLOCK == 0:
        def one_block(n):
            q0 = n * Q_BLOCK
            return nsa_attend_block(
                lax.dynamic_slice_in_dim(q, q0, Q_BLOCK, axis=1),
                lax.dynamic_slice_in_dim(gate, q0, Q_BLOCK, axis=1),
                lax.dynamic_slice_in_dim(pos, q0, Q_BLOCK, axis=0),
                lax.dynamic_slice_in_dim(kw_all, q0, Q_BLOCK + WINDOW, axis=1),
                lax.dynamic_slice_in_dim(vw_all, q0, Q_BLOCK + WINDOW, axis=1),
                lax.dynamic_slice_in_dim(w_pos, q0, Q_BLOCK + WINDOW, axis=0),
                k_cmp, v_cmp, c_end, sel_blk[0], sel_blk[1], smap)
        o = lax.map(one_block, jnp.arange(T // Q_BLOCK))
        o = jnp.moveaxis(o, 0, 1).reshape(B, T, NSA_WIDTH)
    else:
        o = nsa_attend_block(q, gate, pos, kw_all, vw_all, w_pos, k_cmp, v_cmp, c_end,
                             sel_blk[0], sel_blk[1], smap).reshape(B, T, NSA_WIDTH)
    return o, new_cmp, new_sel, win_state


def _lin_combine(e1, e2):
    a1, b1 = e1
    a2, b2 = e2
    return a1 * a2, a2 * b1 + b2


def rglru_mixer(xr, gr, conv_buf, h0, conv_w, conv_b, wa, ba, wx, bx, lam):
    B, T = xr.shape[:2]
    f32 = jnp.float32
    xcat = jnp.concatenate([conv_buf.astype(xr.dtype), xr], axis=1)
    xc = conv_b + sum(conv_w[k] * xcat[:, k:k + T] for k in range(CONV_W))
    new_buf = xcat[:, T:]
    xb = xc.reshape(B, T, RG_BLOCKS, RG_BW)
    r = jax.nn.sigmoid(jnp.einsum('btnd,nde->btne', xb, wa).reshape(B, T, RG_WIDTH) + ba)
    i = jax.nn.sigmoid(jnp.einsum('btnd,nde->btne', xb, wx).reshape(B, T, RG_WIDTH) + bx)
    log_a = -RG_C * r.astype(f32) * jax.nn.softplus(-lam.astype(f32))
    a = jnp.exp(log_a)
    b = jnp.sqrt(-jnp.expm1(2.0 * log_a)) * (i * xc).astype(f32)
    b = b.at[:, 0].add(a[:, 0] * h0.astype(f32))
    _, h = lax.associative_scan(_lin_combine, (a, b), axis=1)
    out = h.astype(xr.dtype) * jax.nn.gelu(gr)
    return out, h[:, -1].astype(h0.dtype), new_buf


def gated_recurrence(q, k, v, logf, s0):
    B, T, H, DK = q.shape
    DV = v.shape[-1]
    C = HG_CHUNK if T % HG_CHUNK == 0 else T
    nc = T // C

    def to_chunks(a):
        return jnp.moveaxis(a.reshape(B, nc, C, *a.shape[2:]), 1, 0)

    tri = jnp.tril(jnp.ones((C, C), dtype=bool))[None, :, :, None, None]

    def step(S, inp):
        qc, kc, vc, gc = inp
        bcum = jnp.cumsum(gc, axis=1)
        o = jnp.einsum('bthk,bhkv->bthv', qc * jnp.exp(bcum), S)
        dec = jnp.exp(jnp.where(tri, bcum[:, :, None] - bcum[:, None, :], -jnp.inf))
        A = jnp.einsum('bthk,bshk,btshk->bhts', qc, kc, dec)
        o = o + jnp.einsum('bhts,bshv->bthv', A, vc)
        bl = bcum[:, -1]
        S = jnp.exp(bl)[..., None] * S + jnp.einsum('bshk,bshv->bhkv', kc * jnp.exp(bl[:, None] - bcum), vc)
        return S, o

    s_fin, o = lax.scan(step, s0, (to_chunks(q), to_chunks(k), to_chunks(v), to_chunks(logf)))
    return jnp.moveaxis(o, 0, 1).reshape(B, T, H, DV), s_fin


def hgrn2_mixer(hq, hf, hi, hg, s0, lb, gain):
    B, T = hq.shape[:2]
    f32 = jnp.float32
    q = jax.nn.silu(hq.astype(f32)).reshape(B, T, HG_HEADS, HG_DK)
    lbh = lb.reshape(HG_HEADS, HG_DK)
    f = lbh + (1.0 - lbh) * jax.nn.sigmoid(hf.astype(f32).reshape(B, T, HG_HEADS, HG_DK))
    v = hi.astype(f32).reshape(B, T, HG_HEADS, HG_DV)
    o, s_fin = gated_recurrence(q, 1.0 - f, v, jnp.log(f), s0.astype(f32))
    o = rms_norm(o, gain.reshape(HG_HEADS, HG_DV)).reshape(B, T, HG_WIDTH).astype(hq.dtype)
    return o * jax.nn.silu(hg), s_fin.astype(s0.dtype)


def forward_layer(x, pos, w, wb, past):
    (norm_mix, w_in, w_out, norm_ffn, w_up, w_down, cmp_pos, cmp_w1, cmp_b1, cmp_w2, cmp_b2,
     rg_conv_w, rg_conv_b, rg_wa, rg_ba, rg_wx, rg_bx, rg_lambda, hg_lb, hg_gain) = w
    w_in_b, w_out_b, w_up_b, w_down_b = wb
    B, T = x.shape[:2]
    x2 = x.reshape(B * T, D_MODEL)
    proj = norm_matmul(x2, norm_mix, w_in_b)[:, :D_IN].reshape(B, T, D_IN)
    q, kvs, gate, rg_x, rg_g, hg_q, hg_f, hg_i, hg_g = _split(proj, IN_SIZES)
    if past is None:
        nsa_past = None
        rg_buf = jnp.zeros((B, CONV_W - 1, RG_WIDTH), x.dtype)
        rg_h0 = jnp.zeros((B, RG_WIDTH), x.dtype)
        hg_s0 = jnp.zeros((B, HG_HEADS, HG_DK, HG_DV), x.dtype)
    else:
        cmp_past, sel_past, win_buf, rg_h0, rg_buf, hg_s0 = past
        nsa_past = (cmp_past, sel_past, win_buf)
    o_nsa, new_cmp, new_sel, new_win = nsa_mixer(q, kvs, gate, pos, cmp_pos, cmp_w1, cmp_b1, cmp_w2, cmp_b2, nsa_past)
    o_rg, new_h, new_buf = rglru_mixer(rg_x, rg_g, rg_buf, rg_h0, rg_conv_w, rg_conv_b, rg_wa, rg_ba, rg_wx, rg_bx, rg_lambda)
    o_hg, new_s = hgrn2_mixer(hg_q, hg_f, hg_i, hg_g, hg_s0, hg_lb, hg_gain)
    mix = jnp.concatenate([o_nsa, o_rg, o_hg], axis=-1).reshape(B * T, MIX_WIDTH)
    x2 = out_ffn(x2, mix, w_out_b, norm_ffn, w_up_b, w_down_b)
    return x2.reshape(B, T, D_MODEL), (new_cmp, new_sel, new_win, new_h, new_buf, new_s)


def forward_layer_prompt(x, w, wb, pw):
    (norm_mix, w_in, w_out, norm_ffn, w_up, w_down, cmp_pos, cmp_w1, cmp_b1, cmp_w2, cmp_b2,
     rg_conv_w, rg_conv_b, rg_wa, rg_ba, rg_wx, rg_bx, rg_lambda, hg_lb, hg_gain) = w
    _, w_out_b, w_up_b, w_down_b = wb
    w_rel, cw, tables, tables_cmp = pw
    B, T = x.shape[:2]
    x2 = x.reshape(B * T, D_MODEL)
    qz, cmp, sel, win, kvs, kvw, gate, rg, hg = in_proj_prompt(x2, norm_mix, w_rel, tables, T)
    kvc = compress(cmp, cw, tables_cmp, T)
    o_nsa = nsa_prompt(qz, gate, kvc, kvs, kvw, B, T)
    kv_shape = (B, T, 2, NSA_KV_HEADS, HEAD_DIM)
    new_cmp, new_sel = cmp.reshape(kv_shape), sel.reshape(kv_shape)
    new_win = win.reshape(kv_shape)[:, T - min(WINDOW, T):]
    rg_buf = jnp.zeros((B, CONV_W - 1, RG_WIDTH), x.dtype)
    rg_h0 = jnp.zeros((B, RG_WIDTH), x.dtype)
    hg_s0 = jnp.zeros((B, HG_HEADS, HG_DK, HG_DV), x.dtype)
    o_rg, new_h, new_buf = rglru_prompt(rg, rg_buf, rg_h0, rg_conv_w, rg_conv_b, rg_wa, rg_ba, rg_wx, rg_bx,
                                        rg_lambda, T)
    o_hg, new_s = hgrn_prompt(hg, hg_s0, hg_lb, hg_gain, T)
    mix = jnp.concatenate([o_nsa, o_rg, o_hg], axis=-1)
    x2 = out_ffn(x2, mix, w_out_b, norm_ffn, w_up_b, w_down_b)
    return x2.reshape(B, T, D_MODEL), (new_cmp, new_sel, new_win, new_h, new_buf, new_s)


def forward_layer_sample(l, x, w, wb, pw, caches):
    (norm_mix, w_in, w_out, norm_ffn, w_up, w_down, cmp_pos, cmp_w1, cmp_b1, cmp_w2, cmp_b2,
     rg_conv_w, rg_conv_b, rg_wa, rg_ba, rg_wx, rg_bx, rg_lambda, hg_lb, hg_gain) = w
    _, w_out_b, w_up_b, w_down_b = wb
    w_rel, cw, tables_s = pw
    page_table, cache_cmp, cache_sel, cache_win, rg_h0, rg_buf, hg_s0 = caches
    n = x.shape[0]
    x2 = x.reshape(n, D_MODEL)
    qz, cmp_new, sel_new, win_new, _, _, gate, rg, hg = in_proj_prompt(x2, norm_mix, w_rel, tables_s, n)
    o_nsa, win_out = nsa_sample(l, page_table, cache_cmp, cache_sel, cache_win, sel_new, win_new, qz, gate, cw)
    o_rg, new_h, new_buf = rglru_sample(rg, rg_buf, rg_h0, rg_conv_w, rg_conv_b, rg_wa, rg_ba, rg_wx, rg_bx, rg_lambda)
    o_hg, new_s = hgrn_sample(hg, hg_s0, hg_lb, hg_gain)
    mix = jnp.concatenate([o_nsa, o_rg, o_hg], axis=-1)
    x2 = out_ffn(x2, mix, w_out_b, norm_ffn, w_up_b, w_down_b)
    kv_shape = (n, 1, 2, NSA_KV_HEADS, HEAD_DIM)
    new_win = win_out.reshape(n, win_out.shape[1], 2, NSA_KV_HEADS, HEAD_DIM)
    return x2.reshape(n, 1, D_MODEL), (cmp_new.reshape(kv_shape), sel_new.reshape(kv_shape), new_win, new_h, new_buf, new_s)


def kernel(x_prompt, x_sample, cache_nsa_cmp_kv, cache_nsa_sel_kv, cache_nsa_win_kv, state_rglru_h,
           state_rglru_conv, state_hgrn_s, page_table, norm_mix, w_in, w_out, norm_ffn, w_up, w_down,
           cmp_pos, cmp_w1, cmp_b1, cmp_w2, cmp_b2, rg_conv_w, rg_conv_b, rg_wa, rg_ba, rg_wx, rg_bx,
           rg_lambda, hg_lower_bounds, hg_gain, final_norm):
    lb = jnp.cumsum(jax.nn.softmax(hg_lower_bounds.astype(jnp.float32), axis=0), axis=0)
    lb = lb - lb[0]
    past_len = page_table.shape[1] * PAGE_SIZE
    n_dec = x_sample.shape[0]
    pos_p = jnp.arange(x_prompt.shape[1], dtype=jnp.int32)
    pos_s = past_len + jnp.arange(x_sample.shape[1], dtype=jnp.int32)
    d_in_pad = _round_up(D_IN, LANE)
    w_in_b = jnp.pad(w_in, ((0, 0), (0, 0), (0, d_in_pad - D_IN))).astype(BF16)
    w_out_b = w_out.astype(BF16)
    w_up_b = w_up.astype(BF16)
    w_down_b = w_down.astype(BF16)
    w_rel = relayout_w_in(w_in)
    cws = compress_weights(cmp_pos, cmp_w1, cmp_b1, cmp_w2, cmp_b2)
    T = x_prompt.shape[1]
    tables = rope_tables(pos_p)
    c_end_p = jnp.arange(T // CMP_STRIDE, dtype=jnp.int32) * CMP_STRIDE + CMP_BLOCK - 1
    tables_cmp = rope_tables(c_end_p)[:3]
    assert x_sample.shape[1] == 1
    tables_s = rope_tables(jnp.full((n_dec,), past_len, jnp.int32))
    xp, xs = x_prompt, x_sample
    st_p, st_s = [], []
    for l in range(DEPTH):
        w = (norm_mix[l], w_in[l], w_out[l], norm_ffn[l], w_up[l], w_down[l], cmp_pos[l], cmp_w1[l], cmp_b1[l],
             cmp_w2[l], cmp_b2[l], rg_conv_w[l], rg_conv_b[l], rg_wa[l], rg_ba[l], rg_wx[l], rg_bx[l],
             rg_lambda[l], lb[l], hg_gain[l])
        wb = (w_in_b[l], w_out_b[l], w_up_b[l], w_down_b[l])
        pw = (w_rel[l], tuple(a[l] for a in cws), tables, tables_cmp)
        xp, sp = forward_layer_prompt(xp, w, wb, pw)
        caches = (page_table, cache_nsa_cmp_kv, cache_nsa_sel_kv, cache_nsa_win_kv,
                  state_rglru_h[l], state_rglru_conv[l], state_hgrn_s[l])
        xs, ss = forward_layer_sample(l, xs, w, wb, (w_rel[l], pw[1], tables_s), caches)
        st_p.append(sp)
        st_s.append(ss)

    def stack(sts, i):
        return jnp.stack([s[i] for s in sts], axis=0)

    y_prompt = final_rms(xp.reshape(-1, D_MODEL), final_norm).reshape(xp.shape)
    y_sample = final_rms(xs.reshape(-1, D_MODEL), final_norm).reshape(xs.shape)
    return (y_prompt, y_sample, stack(st_p, 0), stack(st_p, 1), stack(st_p, 2), stack(st_p, 3), stack(st_p, 4),
            stack(st_p, 5), stack(st_s, 0), stack(st_s, 1), stack(st_s, 2), stack(st_s, 3), stack(st_s, 4),
            stack(st_s, 5))
```

```python
import functools

import jax
import jax.numpy as jnp
from jax import lax
from jax.experimental import pallas as pl
from jax.experimental.pallas import tpu as pltpu

F32 = jnp.float32
BF16 = jnp.bfloat16

D_MODEL = 1024
DEPTH = 4
PAGE_SIZE = 128
HEAD_DIM = 64
NSA_HEADS = D_MODEL // (2 * HEAD_DIM)
NSA_KV_HEADS = 2
NSA_HPG = NSA_HEADS // NSA_KV_HEADS
NSA_WIDTH = NSA_HEADS * HEAD_DIM
KV_W = NSA_KV_HEADS * HEAD_DIM
CMP_BLOCK = 32
CMP_STRIDE = 16
CMP_RATIO = CMP_BLOCK // CMP_STRIDE
CMP_HIDDEN = 256
SEL_BLOCK = 64
SEL_TOPK = 16
WINDOW = 512
Q_BLOCK = 128
FORCE_SCORE = 1e6
ROPE_DIM = HEAD_DIM // 4
ROPE_THETA = 500000.0
RG_WIDTH = D_MODEL // 4
RG_BLOCKS = 4
RG_BW = RG_WIDTH // RG_BLOCKS
RG_C = 8.0
CONV_W = 4
HG_HEADS = 4
HG_DK = 64
HG_DV = D_MODEL // 4 // HG_HEADS
HG_WIDTH = HG_HEADS * HG_DV
HG_CHUNK = 64
MIX_WIDTH = NSA_WIDTH + RG_WIDTH + HG_WIDTH
D_FF = 4 * D_MODEL
EPS = 1e-6
IN_SIZES = (NSA_WIDTH, 6 * KV_W, 3 * NSA_HEADS, RG_WIDTH, RG_WIDTH, HG_HEADS * HG_DK, HG_HEADS * HG_DK, HG_WIDTH, HG_WIDTH)
D_IN = sum(IN_SIZES)

LANE = 128
VMEM_LIMIT = 56 * 1024 * 1024


def _round_up(n, m):
    return -(-n // m) * m


def _row_tile(m, want):
    t = min(m, want)
    while m % t:
        t //= 2
    return t


def _rms(x, g):
    return x * lax.rsqrt(jnp.mean(x * x, axis=-1, keepdims=True) + EPS) * g


def _norm_matmul_body(x_ref, g_ref, w_ref, o_ref):
    y = _rms(x_ref[...], g_ref[...]).astype(BF16)
    o_ref[...] = jnp.dot(y, w_ref[...], preferred_element_type=F32)


def norm_matmul(x, g, w_bf16, tm=512):
    m, d = x.shape
    n = w_bf16.shape[1]
    tm = _row_tile(m, tm)
    return pl.pallas_call(
        _norm_matmul_body,
        grid=(m // tm,),
        in_specs=[
            pl.BlockSpec((tm, d), lambda i: (i, 0)),
            pl.BlockSpec((1, d), lambda i: (0, 0)),
            pl.BlockSpec((d, n), lambda i: (0, 0)),
        ],
        out_specs=pl.BlockSpec((tm, n), lambda i: (i, 0)),
        out_shape=jax.ShapeDtypeStruct((m, n), F32),
        compiler_params=pltpu.CompilerParams(
            dimension_semantics=("arbitrary",), vmem_limit_bytes=VMEM_LIMIT),
        name="norm_in_proj",
    )(x, g.reshape(1, d), w_bf16)


def _out_ffn_body(x_ref, mix_ref, wo_ref, g_ref, wu_ref, wd_ref, o_ref, *, f_tile):
    x1 = x_ref[...] + jnp.dot(mix_ref[...].astype(BF16), wo_ref[...], preferred_element_type=F32)
    y = _rms(x1, g_ref[...]).astype(BF16)
    acc = x1
    for j in range(wu_ref.shape[1] // f_tile):
        h = jnp.maximum(jnp.dot(y, wu_ref[:, j * f_tile:(j + 1) * f_tile], preferred_element_type=F32), 0.0)
        acc = acc + jnp.dot((h * h).astype(BF16), wd_ref[j * f_tile:(j + 1) * f_tile, :], preferred_element_type=F32)
    o_ref[...] = acc


def out_ffn(x, mix, wo, g, wu, wd, tm=512, f_tile=512):
    m, d = x.shape
    dm = mix.shape[1]
    dff = wu.shape[1]
    tm = _row_tile(m, tm)
    const = lambda i: (0, 0)
    return pl.pallas_call(
        functools.partial(_out_ffn_body, f_tile=f_tile),
        grid=(m // tm,),
        in_specs=[
            pl.BlockSpec((tm, d), lambda i: (i, 0)),
            pl.BlockSpec((tm, dm), lambda i: (i, 0)),
            pl.BlockSpec((dm, d), const),
            pl.BlockSpec((1, d), const),
            pl.BlockSpec((d, dff), const),
            pl.BlockSpec((dff, d), const),
        ],
        out_specs=pl.BlockSpec((tm, d), lambda i: (i, 0)),
        out_shape=jax.ShapeDtypeStruct((m, d), F32),
        compiler_params=pltpu.CompilerParams(
            dimension_semantics=("arbitrary",), vmem_limit_bytes=VMEM_LIMIT),
        name="out_proj_ffn",
    )(x, mix, wo, g.reshape(1, d), wu, wd)


def _final_norm_body(x_ref, g_ref, o_ref):
    o_ref[...] = _rms(x_ref[...], g_ref[...])


def final_rms(x, g, tm=1024):
    m, d = x.shape
    tm = _row_tile(m, tm)
    return pl.pallas_call(
        _final_norm_body,
        grid=(m // tm,),
        in_specs=[pl.BlockSpec((tm, d), lambda i: (i, 0)), pl.BlockSpec((1, d), lambda i: (0, 0))],
        out_specs=pl.BlockSpec((tm, d), lambda i: (i, 0)),
        out_shape=jax.ShapeDtypeStruct((m, d), F32),
        name="final_norm",
    )(x, g.reshape(1, d))


MASK_BIG = 2.0 ** 100
LOG2E = 1.4426950408889634

_O_Q = 0
_O_KV = NSA_WIDTH
_O_GATE = _O_KV + 6 * KV_W
_O_RG = _O_GATE + 3 * NSA_HEADS
_O_HG = _O_RG + 2 * RG_WIDTH

_P_QZ = 0
_P_CMP = _P_QZ + NSA_HEADS * LANE
_P_SEL = _P_CMP + 2 * KV_W
_P_WIN = _P_SEL + 2 * KV_W
_P_KVS = _P_WIN + 2 * KV_W
_P_KVW = _P_KVS + 2 * KV_W
_P_GATE = _P_KVW + 2 * KV_W
_P_RG = _P_GATE + NSA_KV_HEADS * LANE
_P_HG = _P_RG + 2 * RG_WIDTH
_P_END = _P_HG + 4 * HG_WIDTH


def _in_proj_columns():
    import numpy as np
    src = -np.ones((_P_END,), np.int64)
    for hq in range(NSA_HEADS):
        src[_P_QZ + hq * LANE:_P_QZ + hq * LANE + HEAD_DIM] = _O_Q + hq * HEAD_DIM + np.arange(HEAD_DIM)
    src[_P_CMP:_P_CMP + 2 * KV_W] = _O_KV + np.arange(2 * KV_W)
    src[_P_SEL:_P_SEL + 2 * KV_W] = _O_KV + 2 * KV_W + np.arange(2 * KV_W)
    src[_P_WIN:_P_WIN + 2 * KV_W] = _O_KV + 4 * KV_W + np.arange(2 * KV_W)
    for base, off in ((_P_KVS, _O_KV + 2 * KV_W), (_P_KVW, _O_KV + 4 * KV_W)):
        for g in range(NSA_KV_HEADS):
            src[base + g * LANE:base + g * LANE + HEAD_DIM] = off + g * HEAD_DIM + np.arange(HEAD_DIM)
            src[base + g * LANE + HEAD_DIM:base + (g + 1) * LANE] = off + KV_W + g * HEAD_DIM + np.arange(HEAD_DIM)
    ng = 3 * NSA_HPG
    for g in range(NSA_KV_HEADS):
        src[_P_GATE + g * LANE:_P_GATE + g * LANE + ng] = _O_GATE + g * ng + np.arange(ng)
    src[_P_RG:_P_RG + 2 * RG_WIDTH] = _O_RG + np.arange(2 * RG_WIDTH)
    src[_P_HG:_P_HG + 4 * HG_WIDTH] = _O_HG + np.arange(4 * HG_WIDTH)
    return src


def relayout_w_in(w_in):
    import numpy as np
    src = _in_proj_columns()
    cols = jnp.take(w_in, jnp.asarray(np.maximum(src, 0), jnp.int32), axis=-1)
    return jnp.where(jnp.asarray(src >= 0), cols, 0.0).astype(BF16)


def rope_tables(pos):
    import numpy as np
    half = ROPE_DIM // 2
    inv = ROPE_THETA ** (-jnp.arange(half, dtype=F32) * 2.0 / ROPE_DIM)
    ang = pos.astype(F32)[:, None] * inv
    cos, sin = jnp.cos(ang), jnp.sin(ang)
    lane = np.arange(LANE)
    d = lane % HEAD_DIM
    idx = jnp.asarray(d % half, jnp.int32)
    cos_l, sin_l = jnp.take(cos, idx, axis=1), jnp.take(sin, idx, axis=1)
    out = []
    for active in (lane < ROPE_DIM, d < ROPE_DIM):
        first = jnp.asarray(active & (d < half))
        second = jnp.asarray(active & (d >= half) & (d < ROPE_DIM))
        out += [jnp.where(first | second, cos_l, 1.0), jnp.where(first, -sin_l, 0.0), jnp.where(second, sin_l, 0.0)]
    return tuple(out)


def _rope128(x, c, s1, s2):
    return x * c + pltpu.roll(x, LANE - ROPE_DIM // 2, 1) * s1 + pltpu.roll(x, ROPE_DIM // 2, 1) * s2


def _in_proj_prompt_body(x_ref, g_ref, w_ref, cq_ref, s1q_ref, s2q_ref, ckk_ref, s1kk_ref, s2kk_ref,
                         qz_ref, cmp_ref, sel_ref, win_ref, kvs_ref, kvw_ref, gate_ref, rg_ref, hg_ref,
                         *, tiles_per_seq):
    tm = x_ref.shape[0]
    y = _rms(x_ref[...], g_ref[...]).astype(BF16)

    def mm(c0, n):
        return jnp.dot(y, w_ref[:, c0:c0 + n], preferred_element_type=F32)

    cq, s1q, s2q = cq_ref[...], s1q_ref[...], s2q_ref[...]
    ckk, s1kk, s2kk = ckk_ref[...], s1kk_ref[...], s2kk_ref[...]
    scale = HEAD_DIM ** -0.5 * LOG2E
    for hq in range(NSA_HEADS):
        qh = _rope128(mm(_P_QZ + hq * LANE, LANE), cq, s1q, s2q) * scale
        qz_ref[:, hq * LANE:(hq + 1) * LANE] = qh.astype(BF16)
    cmp_ref[...] = mm(_P_CMP, 2 * KV_W)
    for base, ref in ((_P_SEL, sel_ref), (_P_WIN, win_ref)):
        ref[:, 0:LANE] = _rope128(mm(base, LANE), ckk, s1kk, s2kk)
        ref[:, LANE:2 * LANE] = mm(base + LANE, LANE)
    row0 = (pl.program_id(0) % tiles_per_seq) * tm
    blk = (row0 + lax.broadcasted_iota(jnp.int32, (tm, 1), 0)) // SEL_BLOCK
    onehot = jnp.where(lax.broadcasted_iota(jnp.int32, (1, LANE), 1) == blk, MASK_BIG, 0.0).astype(BF16)
    for g in range(NSA_KV_HEADS):
        kvs_ref[g, :, 0:LANE] = _rope128(mm(_P_KVS + g * LANE, LANE), cq, s1q, s2q).astype(BF16)
        kvs_ref[g, :, LANE:2 * LANE] = onehot
        kvw_ref[g] = _rope128(mm(_P_KVW + g * LANE, LANE), cq, s1q, s2q).astype(BF16)
        gate_ref[g] = jax.nn.sigmoid(mm(_P_GATE + g * LANE, LANE))
    rg_ref[...] = mm(_P_RG, 2 * RG_WIDTH)
    hg_ref[...] = mm(_P_HG, 4 * HG_WIDTH)


def in_proj_prompt(x, g, w_rel, tables, seq_len, tm=512):
    m, d = x.shape
    tm = _row_tile(seq_len, tm)
    tiles_per_seq = seq_len // tm
    G = NSA_KV_HEADS
    row = lambda i: (i, 0)
    grow = lambda i: (0, i, 0)
    tab = pl.BlockSpec((tm, LANE), lambda i: (i % tiles_per_seq, 0))
    return pl.pallas_call(
        functools.partial(_in_proj_prompt_body, tiles_per_seq=tiles_per_seq),
        grid=(m // tm,),
        in_specs=[pl.BlockSpec((tm, d), row), pl.BlockSpec((1, d), lambda i: (0, 0)),
                  pl.BlockSpec((d, _P_END), lambda i: (0, 0))] + [tab] * 6,
        out_specs=[
            pl.BlockSpec((tm, NSA_HEADS * LANE), row),
            pl.BlockSpec((tm, 2 * KV_W), row), pl.BlockSpec((tm, 2 * KV_W), row), pl.BlockSpec((tm, 2 * KV_W), row),
            pl.BlockSpec((G, tm, 2 * LANE), grow), pl.BlockSpec((G, tm, LANE), grow), pl.BlockSpec((G, tm, LANE), grow),
            pl.BlockSpec((tm, 2 * RG_WIDTH), row), pl.BlockSpec((tm, 4 * HG_WIDTH), row),
        ],
        out_shape=[
            jax.ShapeDtypeStruct((m, NSA_HEADS * LANE), BF16),
            jax.ShapeDtypeStruct((m, 2 * KV_W), F32), jax.ShapeDtypeStruct((m, 2 * KV_W), F32),
            jax.ShapeDtypeStruct((m, 2 * KV_W), F32),
            jax.ShapeDtypeStruct((G, m, 2 * LANE), BF16), jax.ShapeDtypeStruct((G, m, LANE), BF16),
            jax.ShapeDtypeStruct((G, m, LANE), F32),
            jax.ShapeDtypeStruct((m, 2 * RG_WIDTH), F32), jax.ShapeDtypeStruct((m, 4 * HG_WIDTH), F32),
        ],
        compiler_params=pltpu.CompilerParams(
            dimension_semantics=("arbitrary",), vmem_limit_bytes=VMEM_LIMIT),
        name="in_proj_prompt",
    )(x, g.reshape(1, d), w_rel, *tables)


def compress_weights(cmp_pos, cmp_w1, cmp_b1, cmp_w2, cmp_b2):
    L = cmp_w1.shape[0]
    G, S, HD, CH = NSA_KV_HEADS, CMP_STRIDE, HEAD_DIM, CMP_HIDDEN
    w1 = cmp_w1.reshape(L, 2, CMP_RATIO, S, HD, CH)
    z = jnp.zeros_like(w1)
    w1bd = jnp.concatenate([jnp.concatenate([w1, z], axis=-1), jnp.concatenate([z, w1], axis=-1)], axis=-2)
    pe = jnp.tile(cmp_pos.reshape(L, 2, CMP_RATIO, S, HD), (1, 1, 1, 1, G))
    b1 = jnp.tile(cmp_b1, (1, 1, G)).reshape(L, 2, 1, G * CH)
    zc = jnp.zeros((L, CH, HD), F32)
    w2 = jnp.stack([jnp.concatenate([cmp_w2[:, 0], zc], axis=-1), jnp.concatenate([zc, cmp_w2[:, 1]], axis=-1)], axis=1)
    b2 = jnp.concatenate([cmp_b2[:, 0], cmp_b2[:, 1]], axis=-1).reshape(L, 1, 2 * HD)
    return w1bd.astype(BF16), pe, b1, w2.astype(BF16), b2


def _compress_body(xk_ref, xv_ref, pe_ref, w1_ref, b1_ref, w2_ref, b2_ref, c_ref, s1_ref, s2_ref, o_ref, *, nch, n_cmp):
    gelus = []
    for kv, x_ref in enumerate((xk_ref, xv_ref)):
        acc = [jnp.zeros((nch, NSA_KV_HEADS * CMP_HIDDEN), F32) for _ in range(CMP_RATIO)]
        for s in range(CMP_STRIDE):
            xs = x_ref[pl.ds(s, nch, stride=CMP_STRIDE), :]
            for r in range(CMP_RATIO):
                acc[r] = acc[r] + jnp.dot((xs + pe_ref[kv, r, s:s + 1, :]).astype(BF16), w1_ref[kv, r, s],
                                          preferred_element_type=F32)
        h = b1_ref[kv] + acc[0] + pltpu.roll(acc[1], nch - 1, 0)
        gelus.append(jax.nn.gelu(h).astype(BF16))
    valid = lax.broadcasted_iota(jnp.int32, (nch, 1), 0) < n_cmp
    for g in range(NSA_KV_HEADS):
        cols = slice(g * CMP_HIDDEN, (g + 1) * CMP_HIDDEN)
        out = (jnp.dot(gelus[0][:, cols], w2_ref[0], preferred_element_type=F32)
               + jnp.dot(gelus[1][:, cols], w2_ref[1], preferred_element_type=F32) + b2_ref[...])
        out = _rope128(out, c_ref[...], s1_ref[...], s2_ref[...])
        o_ref[g] = jnp.where(valid, out, 0.0).astype(BF16)


def compress(raw, cw, tables_q, seq_len):
    w1bd, pe, b1, w2, b2 = cw
    m = raw.shape[0]
    nb = m // seq_len
    nch = seq_len // CMP_STRIDE
    n_cmp = nch - CMP_RATIO + 1
    full = lambda shape: pl.BlockSpec(shape, lambda b: (0,) * len(shape))
    return pl.pallas_call(
        functools.partial(_compress_body, nch=nch, n_cmp=n_cmp),
        grid=(nb,),
        in_specs=[pl.BlockSpec((seq_len, KV_W), lambda b: (b, 0)), pl.BlockSpec((seq_len, KV_W), lambda b: (b, 1)),
                  full(pe.shape), full(w1bd.shape), full(b1.shape), full(w2.shape), full(b2.shape),
                  full((nch, LANE)), full((nch, LANE)), full((nch, LANE))],
        out_specs=pl.BlockSpec((None, NSA_KV_HEADS, nch, LANE), lambda b: (b, 0, 0, 0)),
        out_shape=jax.ShapeDtypeStruct((nb, NSA_KV_HEADS, nch, LANE), BF16),
        compiler_params=pltpu.CompilerParams(
            dimension_semantics=("arbitrary",), vmem_limit_bytes=VMEM_LIMIT),
        name="compress_kv",
    )(raw, raw, pe, w1bd, b1, w2, b2, *tables_q)


_NT = (((1,), (1,)), ((), ()))


def _masked_softmax2(s, mask):
    s = jnp.where(mask, s, -1e30)
    m = jnp.max(s, axis=-1, keepdims=True)
    e = jnp.where(mask, jnp.exp2(s - m), 0.0)
    return e / jnp.maximum(jnp.sum(e, axis=-1, keepdims=True), 1e-30)


def _nsa_prompt_body(qz_ref, gate_ref, kvc_ref, smap_ref, kvs_ref, kvw_ref, o_ref, *, tk):
    QB, H = Q_BLOCK, NSA_HPG
    q0 = pl.program_id(2) * QB
    q4 = jnp.concatenate([qz_ref[:, h * LANE:(h + 1) * LANE] for h in range(H)], axis=0)
    t_col = q0 + (lax.broadcasted_iota(jnp.int32, (H * QB, 1), 0) & (QB - 1))

    kvc = kvc_ref[...]
    ncp = kvc.shape[0]
    s_c = lax.dot_general(q4, kvc, _NT, preferred_element_type=F32)
    c_end = lax.broadcasted_iota(jnp.int32, (1, ncp), 1) * CMP_STRIDE + (CMP_BLOCK - 1)
    p_c = _masked_softmax2(s_c, c_end <= t_col)
    o_c = jnp.dot(p_c.astype(BF16), kvc, preferred_element_type=F32)

    psum = p_c[0:QB]
    for h in range(1, H):
        psum = psum + p_c[h * QB:(h + 1) * QB]
    hi = psum.astype(BF16)
    lo = (psum - hi.astype(F32)).astype(BF16)
    smap = smap_ref[...]
    imp = jnp.dot(hi, smap, preferred_element_type=F32) + jnp.dot(lo, smap, preferred_element_type=F32)

    jf = lax.broadcasted_iota(jnp.int32, (QB, LANE), 1)
    t_tok = q0 + lax.broadcasted_iota(jnp.int32, (QB, 1), 0)
    cur = t_tok // SEL_BLOCK
    valid = jf * SEL_BLOCK <= t_tok
    forced = (jf == 0) | (jf == cur) | (jf == cur - 1)
    score = jnp.where(valid, jnp.where(forced, FORCE_SCORE, imp), -FORCE_SCORE)

    def pick_one(_, carry):
        score, chosen = carry
        hit = jf == jnp.argmax(score, axis=1, keepdims=True)
        return jnp.where(hit, -jnp.inf, score), jnp.where(hit, 1.0, chosen)

    _, chosen = lax.fori_loop(0, SEL_TOPK, pick_one, (score, jnp.zeros((QB, LANE), F32)), unroll=True)
    selm1 = (jnp.where(valid, chosen, 0.0) - 1.0).astype(BF16)
    q_aug = jnp.concatenate([q4, jnp.concatenate([selm1] * H, axis=0)], axis=1)

    NCHAIN = 1
    RC = H * QB // NCHAIN

    def tile_step(kt, carry, causal=False):
        hk = tk // 2
        start = pl.multiple_of(kt * tk, tk)
        tiles = [kvs_ref[pl.ds(start + j * hk, hk), :] for j in range(2)]
        out = []
        for c in range(NCHAIN):
            m, l, acc = carry[c]
            qc = q_aug[c * RC:(c + 1) * RC]
            ss = [lax.dot_general(qc, tl, _NT, preferred_element_type=F32) for tl in tiles]
            if causal:
                for j in range(2):
                    kpos = kt * tk + j * hk + lax.broadcasted_iota(jnp.int32, (1, hk), 1)
                    ss[j] = jnp.where(kpos <= t_col[c * RC:(c + 1) * RC], ss[j], -1e30)
            m_new = jnp.maximum(m, jnp.maximum(jnp.max(ss[0], axis=1, keepdims=True),
                                               jnp.max(ss[1], axis=1, keepdims=True)))
            alpha = jnp.exp2(m - m_new)
            ps = [jnp.exp2(sj - m_new) for sj in ss]
            l = alpha * l + jnp.sum(ps[0], axis=1, keepdims=True) + jnp.sum(ps[1], axis=1, keepdims=True)
            pv = (jnp.dot(ps[0].astype(BF16), tiles[0][:, 0:LANE], preferred_element_type=F32)
                  + jnp.dot(ps[1].astype(BF16), tiles[1][:, 0:LANE], preferred_element_type=F32))
            out.append((m_new, l, alpha * acc + pv))
        return tuple(out)

    n_full = q0 // tk
    init = tuple((jnp.full((RC, 1), -1e30, F32), jnp.zeros((RC, 1), F32), jnp.zeros((RC, LANE), F32))
                 for _ in range(NCHAIN))
    n_pairs = n_full // 2
    carry = lax.fori_loop(0, n_pairs, lambda i, c: tile_step(2 * i + 1, tile_step(2 * i, c)), init)
    carry = lax.fori_loop(2 * n_pairs, n_full, tile_step, carry)
    carry = tile_step(n_full, carry, causal=True)
    acc_s = jnp.concatenate([c[2] for c in carry], axis=0)
    o_s = acc_s[:, 0:LANE] / jnp.concatenate([c[1] for c in carry], axis=0)

    wlen = WINDOW + QB
    w0 = pl.multiple_of(jnp.maximum(q0 - WINDOW, 0), QB)
    wt = kvw_ref[pl.ds(w0, wlen), :]
    s_w = lax.dot_general(q4, wt, _NT, preferred_element_type=F32)
    dist = t_col - (w0 + lax.broadcasted_iota(jnp.int32, (1, wlen), 1))
    p_w = _masked_softmax2(s_w, (dist >= 0) & (dist < WINDOW))
    o_w = jnp.dot(p_w.astype(BF16), wt, preferred_element_type=F32)

    comb = []
    for h in range(H):
        r = slice(h * QB, (h + 1) * QB)
        comb.append(gate_ref[:, 3 * h:3 * h + 1] * o_c[r] + gate_ref[:, 3 * h + 1:3 * h + 2] * o_s[r]
                    + gate_ref[:, 3 * h + 2:3 * h + 3] * o_w[r])
    low = lax.broadcasted_iota(jnp.int32, (1, LANE), 1) < HEAD_DIM
    for p in range(H // 2):
        o_ref[:, p * LANE:(p + 1) * LANE] = jnp.where(low, pltpu.roll(comb[2 * p], HEAD_DIM, 1), comb[2 * p + 1])


def nsa_prompt(qz, gate, kvc, kvs, kvw, nb, seq_len, tk=512):
    m = qz.shape[0]
    G, QB = NSA_KV_HEADS, Q_BLOCK
    nqb = seq_len // QB
    ncp = kvc.shape[2]
    assert seq_len % tk == 0 and seq_len >= WINDOW + QB and seq_len // SEL_BLOCK <= LANE
    import numpy as np
    c0 = np.arange(ncp)[:, None] * CMP_STRIDE
    s0 = np.arange(LANE)[None, :] * SEL_BLOCK
    n_cmp = ncp - CMP_RATIO + 1
    ov = (c0 < s0 + SEL_BLOCK) & (c0 + CMP_BLOCK > s0) & (np.arange(ncp)[:, None] < n_cmp)
    smap = jnp.asarray(ov, BF16)
    return pl.pallas_call(
        functools.partial(_nsa_prompt_body, tk=tk),
        grid=(nb, G, nqb),
        in_specs=[
            pl.BlockSpec((QB, NSA_HPG * LANE), lambda b, g, q: (b * nqb + q, g)),
            pl.BlockSpec((None, QB, LANE), lambda b, g, q: (g, b * nqb + q, 0)),
            pl.BlockSpec((None, None, ncp, LANE), lambda b, g, q: (b, g, 0, 0)),
            pl.BlockSpec((ncp, LANE), lambda b, g, q: (0, 0)),
            pl.BlockSpec((None, seq_len, 2 * LANE), lambda b, g, q: (g, b, 0)),
            pl.BlockSpec((None, seq_len, LANE), lambda b, g, q: (g, b, 0)),
        ],
        out_specs=pl.BlockSpec((QB, NSA_HPG * HEAD_DIM), lambda b, g, q: (b * nqb + q, g)),
        out_shape=jax.ShapeDtypeStruct((m, NSA_WIDTH), F32),
        compiler_params=pltpu.CompilerParams(
            dimension_semantics=("arbitrary", "arbitrary", "arbitrary"), vmem_limit_bytes=VMEM_LIMIT),
        name="nsa_prompt",
    )(qz, gate, kvc, smap, kvs, kvw)


def _shift_rows(x, prev8, d):
    rolled = pltpu.roll(x, d, 0)
    row8 = lax.broadcasted_iota(jnp.int32, (8, 1), 0)
    first = jnp.where(row8 < d, pltpu.roll(prev8, d, 0), rolled[0:8])
    return jnp.concatenate([first, rolled[8:]], axis=0)


def _rglru_body(rg_ref, buf_ref, h0_ref, cw_ref, cb_ref, wg_ref, bg_ref, lam_ref,
                o_ref, hout_ref, bufout_ref, tail_ref, h_ref):
    tm = rg_ref.shape[0]
    W = RG_WIDTH

    @pl.when(pl.program_id(1) == 0)
    def _():
        tail_ref[...] = buf_ref[...]
        h_ref[...] = h0_ref[...]

    x = rg_ref[:, 0:W]
    gr = rg_ref[:, W:2 * W]
    prev8 = tail_ref[...]
    xc = cb_ref[...] + cw_ref[CONV_W - 1:CONV_W, :] * x
    for d in range(1, CONV_W):
        xc = xc + cw_ref[CONV_W - 1 - d:CONV_W - d, :] * _shift_rows(x, prev8, d)
    gates = jnp.dot(xc.astype(BF16), wg_ref[...], preferred_element_type=F32) + bg_ref[...]
    r = jax.nn.sigmoid(gates[:, 0:W])
    i = jax.nn.sigmoid(gates[:, W:2 * W])
    log_a = -RG_C * r * jax.nn.softplus(-lam_ref[...])
    a = jnp.exp(log_a)
    th = jnp.tanh(log_a)
    b = jnp.sqrt(-2.0 * th / (1.0 - th)) * (i * xc)
    row = lax.broadcasted_iota(jnp.int32, (tm, 1), 0)
    d = 1
    while d < tm:
        keep = row >= d
        a_sh = jnp.where(keep, pltpu.roll(a, d, 0), 1.0)
        b_sh = jnp.where(keep, pltpu.roll(b, d, 0), 0.0)
        b = a * b_sh + b
        a = a * a_sh
        d *= 2
    h = a * h_ref[7:8, :] + b
    o_ref[...] = h * jax.nn.gelu(gr)
    h_ref[...] = h[tm - 8:tm]
    tail_ref[...] = x[tm - 8:tm]
    hout_ref[...] = h[tm - 8:tm]
    bufout_ref[...] = x[tm - 8:tm]


def rglru_prompt(rg, conv_buf, h0, conv_w, conv_b, wa, ba, wx, bx, lam, seq_len, tm=256):
    m = rg.shape[0]
    nb = m // seq_len
    W = RG_WIDTH
    tm = _row_tile(seq_len, tm)
    nt = seq_len // tm
    bd = jax.scipy.linalg.block_diag
    wg = jnp.concatenate([bd(*[wa[n] for n in range(RG_BLOCKS)]), bd(*[wx[n] for n in range(RG_BLOCKS)])], axis=1).astype(BF16)
    bg = jnp.concatenate([ba, bx]).reshape(1, 2 * W)
    buf8 = jnp.pad(conv_buf, ((0, 0), (8 - (CONV_W - 1), 0), (0, 0)))
    h08 = jnp.broadcast_to(h0[:, None, :], (nb, 8, W))
    const = lambda shape: pl.BlockSpec(shape, lambda b, t: (0,) * len(shape))
    per_b = pl.BlockSpec((None, 8, W), lambda b, t: (b, 0, 0))
    out, h8, nbuf8 = pl.pallas_call(
        _rglru_body,
        grid=(nb, nt),
        in_specs=[pl.BlockSpec((tm, 2 * W), lambda b, t: (b * nt + t, 0)), per_b, per_b,
                  const((CONV_W, W)), const((1, W)), const((W, 2 * W)), const((1, 2 * W)), const((1, W))],
        out_specs=[pl.BlockSpec((tm, W), lambda b, t: (b * nt + t, 0)), per_b, per_b],
        out_shape=[jax.ShapeDtypeStruct((m, W), F32), jax.ShapeDtypeStruct((nb, 8, W), F32),
                   jax.ShapeDtypeStruct((nb, 8, W), F32)],
        scratch_shapes=[pltpu.VMEM((8, W), F32), pltpu.VMEM((8, W), F32)],
        compiler_params=pltpu.CompilerParams(dimension_semantics=("arbitrary", "arbitrary")),
        name="rglru_prompt",
    )(rg, buf8, h08, conv_w, conv_b.reshape(1, W), wg, bg, lam.reshape(1, W))
    return out, h8[:, 7], nbuf8[:, 8 - (CONV_W - 1):]


def _split_dot(x, w_bf16, pieces):
    acc = None
    for _ in range(pieces):
        xb = x.astype(BF16)
        part = jnp.dot(xb, w_bf16, preferred_element_type=F32)
        acc = part if acc is None else acc + part
        x = x - xb.astype(F32)
    return acc


HG_SAFE_DECAY = 75.0


def _hgrn_body(hg_ref, s0_ref, lb_ref, gain_ref, tri_ref, tribd_ref, ones_ref, o_ref, sout_ref, st_ref, *, chunk):
    tm = hg_ref.shape[0]
    W, H = HG_WIDTH, HG_HEADS
    C = chunk
    nck = tm // C

    @pl.when(pl.program_id(1) == 0)
    def _():
        st_ref[...] = s0_ref[...]

    lb = lb_ref[...]
    ones_blk = ones_ref[...]
    same_head = ones_blk > 0

    def finish(o, rows):
        ms = _split_dot(o * o, ones_blk, 2) * (1.0 / HG_DV)
        o = o * lax.rsqrt(ms + EPS) * gain_ref[...]
        o_ref[rows, :] = o * jax.nn.silu(hg_ref[rows, 3 * W:4 * W])

    q = jax.nn.silu(hg_ref[:, 0:W])
    f = lb + (1.0 - lb) * jax.nn.sigmoid(hg_ref[:, W:2 * W])
    v = hg_ref[:, 2 * W:3 * W]
    k = 1.0 - f
    bcum = _split_dot_left(tribd_ref[...], jnp.log(f))
    b_last = [bcum[(c + 1) * C - 1:(c + 1) * C, :] for c in range(nck)]
    total = b_last[0]
    for c in range(1, nck):
        total = jnp.minimum(total, b_last[c])
    safe = jnp.min(total) >= -HG_SAFE_DECAY

    @pl.when(safe)
    def _():
        qt = q * jnp.exp(bcum)
        kt = (k * jnp.exp(-bcum)).astype(BF16)
        vb = v.astype(BF16)
        lane_head = lax.broadcasted_iota(jnp.int32, (1, W), 1) // HG_DV
        q_exp = jnp.concatenate([jnp.where(lane_head == h, qt, 0.0) for h in range(H)], axis=0).astype(BF16)
        a = lax.dot_general(q_exp, kt, _NT, preferred_element_type=F32)
        t_row = lax.broadcasted_iota(jnp.int32, (H * tm, 1), 0) & (tm - 1)
        s_col = lax.broadcasted_iota(jnp.int32, (1, tm), 1)
        pair = (s_col <= t_row) & ((s_col // C) == (t_row // C))
        o_exp = jnp.dot(jnp.where(pair, a, 0.0).astype(BF16), vb, preferred_element_type=F32)
        o = jnp.where(lane_head == 0, o_exp[0:tm], 0.0)
        for h in range(1, H):
            o = o + jnp.where(lane_head == h, o_exp[h * tm:(h + 1) * tm], 0.0)
        st = st_ref[...]
        inter = []
        for c in range(nck):
            r = slice(c * C, (c + 1) * C)
            inter.append(lax.dot_general(qt[r].astype(BF16), st.astype(BF16), _NT, preferred_element_type=F32))
            k_hat = (k[r] * jnp.exp(b_last[c] - bcum[r])).astype(BF16)
            upd = lax.dot_general(vb[r], k_hat, (((0,), (0,)), ((), ())), preferred_element_type=F32)
            st = jnp.exp(b_last[c]) * st + jnp.where(same_head, upd, 0.0)
        st_ref[...] = st
        finish(o + jnp.concatenate(inter, axis=0), slice(None))

    @pl.when(jnp.logical_not(safe))
    def _():
        tri = tri_ref[...]
        rowc = lax.broadcasted_iota(jnp.int32, (C, 1), 0)

        def chunk_step(c, _):
            r0 = pl.multiple_of(c * C, C)
            rows = pl.ds(r0, C)
            qc = jax.nn.silu(hg_ref[rows, 0:W])
            fc = lb + (1.0 - lb) * jax.nn.sigmoid(hg_ref[rows, W:2 * W])
            vc = hg_ref[rows, 2 * W:3 * W]
            kc = 1.0 - fc
            bc = _split_dot_left(tri, jnp.log(fc))
            st = st_ref[...]
            o = lax.dot_general((qc * jnp.exp(bc)).astype(BF16), st.astype(BF16), _NT, preferred_element_type=F32)

            def offset_step(dlt, carry):
                o, b_sh, k_sh, v_sh = carry
                p = jnp.where(rowc >= dlt, qc * k_sh * jnp.exp(bc - b_sh), 0.0)
                o = o + _split_dot(p, ones_blk, 2) * v_sh
                return o, pltpu.roll(b_sh, 1, 0), pltpu.roll(k_sh, 1, 0), pltpu.roll(v_sh, 1, 0)

            o, _, _, _ = lax.fori_loop(0, C, offset_step, (o, bc, kc, vc))
            bl = bc[C - 1:C, :]
            k_hat = (kc * jnp.exp(bl - bc)).astype(BF16)
            upd = lax.dot_general(vc.astype(BF16), k_hat, (((0,), (0,)), ((), ())), preferred_element_type=F32)
            st_ref[...] = jnp.exp(bl) * st + jnp.where(same_head, upd, 0.0)
            finish(o, rows)
            return 0

        lax.fori_loop(0, nck, chunk_step, 0)

    sout_ref[...] = st_ref[...]


def _split_dot_left(w_bf16, x):
    acc = None
    for _ in range(3):
        xb = x.astype(BF16)
        part = jnp.dot(w_bf16, xb, preferred_element_type=F32)
        acc = part if acc is None else acc + part
        x = x - xb.astype(F32)
    return acc


def hgrn_prompt(hg, s0, lb, gain, seq_len, tm=256, chunk=HG_CHUNK):
    import numpy as np
    m = hg.shape[0]
    nb = m // seq_len
    W, H = HG_WIDTH, HG_HEADS
    tm = _row_tile(seq_len, tm)
    nt = seq_len // tm
    head = np.arange(W) // HG_DV
    ones_blk = jnp.asarray(head[:, None] == head[None, :], BF16)
    tri = jnp.asarray(np.tril(np.ones((chunk, chunk))), BF16)
    rt = np.arange(tm)
    tri_bd = jnp.asarray((rt[None, :] <= rt[:, None]) & (rt[None, :] // chunk == rt[:, None] // chunk), BF16)
    s0_t = jnp.einsum('bhkv,hg->bhvgk', s0, jnp.eye(H, dtype=s0.dtype)).reshape(nb, W, W)
    const = lambda shape: pl.BlockSpec(shape, lambda b, t: (0,) * len(shape))
    per_b = pl.BlockSpec((None, W, W), lambda b, t: (b, 0, 0))
    out, s_t = pl.pallas_call(
        functools.partial(_hgrn_body, chunk=chunk),
        grid=(nb, nt),
        in_specs=[pl.BlockSpec((tm, 4 * W), lambda b, t: (b * nt + t, 0)), per_b,
                  const((1, W)), const((1, W)), const((chunk, chunk)), const((tm, tm)), const((W, W))],
        out_specs=[pl.BlockSpec((tm, W), lambda b, t: (b * nt + t, 0)), per_b],
        out_shape=[jax.ShapeDtypeStruct((m, W), F32), jax.ShapeDtypeStruct((nb, W, W), F32)],
        scratch_shapes=[pltpu.VMEM((W, W), F32)],
        compiler_params=pltpu.CompilerParams(dimension_semantics=("arbitrary", "arbitrary")),
        name="hgrn_prompt",
    )(hg, s0_t, lb.reshape(1, W), gain.reshape(1, W), tri, tri_bd, ones_blk)
    s5 = s_t.reshape(nb, H, HG_DV, H, HG_DK)
    s_fin = jnp.stack([s5[:, h, :, h, :] for h in range(H)], axis=1).transpose(0, 1, 3, 2)
    return out, s_fin


def _nsa_sample_body(pt_ref, *refs, n_pages, past_len, n_win):
    del pt_ref
    np2 = 2 * n_pages
    cmp_refs, sel_refs = refs[0:np2], refs[np2:2 * np2]
    (win_ref, selnew_ref, winnew_ref, qlo_ref, qq_ref, gate_ref, pe_ref, w1_ref, b1_ref, w2_ref, b2_ref,
     c_ref, s1_ref, s2_ref, smap_ref, expand_ref, wcol_ref, o_ref, winout_ref, raw_ref) = refs[2 * np2:]
    G, H, CH = NSA_KV_HEADS, NSA_HPG, CMP_HIDDEN
    R = G * H
    nch = n_pages * (PAGE_SIZE // CMP_STRIDE)
    n_cmp = nch - CMP_RATIO + 1
    _TT = (((1,), (1,)), ((), ()))
    for p in range(n_pages):
        for kv in range(2):
            raw_ref[kv, p * PAGE_SIZE:(p + 1) * PAGE_SIZE, :] = cmp_refs[2 * p + kv][...].T
    t = past_len
    row = lax.broadcasted_iota(jnp.int32, (R, 1), 0)
    grp0 = row < H
    lane = lax.broadcasted_iota(jnp.int32, (1, LANE), 1)

    gelus = []
    for kv in range(2):
        acc = [jnp.zeros((nch, G * CH), F32) for _ in range(CMP_RATIO)]
        for sp in range(CMP_STRIDE // 2):
            xs = jnp.concatenate([raw_ref[kv, pl.ds(s, nch, stride=CMP_STRIDE), :] for s in (2 * sp, 2 * sp + 1)], axis=1)
            for r in range(CMP_RATIO):
                acc[r] = acc[r] + jnp.dot((xs + pe_ref[kv, r, sp:sp + 1, :]).astype(BF16), w1_ref[kv, r, sp],
                                          preferred_element_type=F32)
        h = b1_ref[kv] + acc[0] + pltpu.roll(acc[1], nch - 1, 0)
        gelus.append(jax.nn.gelu(h).astype(BF16))
    valid_c = lax.broadcasted_iota(jnp.int32, (nch, 1), 0) < n_cmp
    qlo = qlo_ref[...].astype(BF16)
    c_end = lax.broadcasted_iota(jnp.int32, (1, nch), 1) * CMP_STRIDE + (CMP_BLOCK - 1)
    smap = smap_ref[...]
    o_c, imp = [], []
    for g in range(G):
        cols = slice(g * CH, (g + 1) * CH)
        kvc = (jnp.dot(gelus[0][:, cols], w2_ref[0], preferred_element_type=F32)
               + jnp.dot(gelus[1][:, cols], w2_ref[1], preferred_element_type=F32) + b2_ref[...])
        kvc = jnp.where(valid_c, _rope128(kvc, c_ref[...], s1_ref[...], s2_ref[...]), 0.0).astype(BF16)
        s_c = lax.dot_general(qlo, kvc, _NT, preferred_element_type=F32)
        p_c = _masked_softmax2(s_c, c_end <= t)
        o_c.append(jnp.dot(p_c.astype(BF16), kvc, preferred_element_type=F32))
        mine = grp0 if g == 0 else jnp.logical_not(grp0)
        psum = jnp.sum(jnp.where(mine, p_c, 0.0), axis=0, keepdims=True)
        imp.append(_split_dot(jnp.broadcast_to(psum, (R, nch)), smap, 2))
    o_c = jnp.where(grp0, pltpu.roll(o_c[0], HEAD_DIM, 1), o_c[1])
    imp = jnp.where(grp0, imp[0], imp[1])

    cur = t // SEL_BLOCK
    valid = lane * SEL_BLOCK <= t
    forced = (lane == 0) | (lane == cur) | (lane == cur - 1)
    score = jnp.where(valid, jnp.where(forced, FORCE_SCORE, imp), -FORCE_SCORE)

    def pick_one(_, carry):
        score, chosen = carry
        hit = lane == jnp.argmax(score, axis=1, keepdims=True)
        return jnp.where(hit, -jnp.inf, score), jnp.where(hit, 1.0, chosen)

    _, chosen = lax.fori_loop(0, SEL_TOPK, pick_one, (score, jnp.zeros((R, LANE), F32)), unroll=True)
    selm1 = (jnp.where(valid, chosen, 0.0) - 1.0).astype(BF16)
    bias = jnp.dot(selm1, expand_ref[...], preferred_element_type=F32)

    qq32 = qq_ref[...]
    qq = qq32.astype(BF16)
    sel_k = [sel_refs[2 * p][...].astype(BF16) for p in range(n_pages)]
    s_s = jnp.concatenate([jnp.dot(qq, kp, preferred_element_type=F32) for kp in sel_k], axis=1) + bias
    s_new = jnp.sum(qq32 * selnew_ref[:, 0:LANE], axis=1, keepdims=True)
    m = jnp.maximum(jnp.max(s_s, axis=1, keepdims=True), s_new)
    p = jnp.exp2(s_s - m)
    p_new = jnp.exp2(s_new - m)
    denom = jnp.sum(p, axis=1, keepdims=True) + p_new
    pb = p.astype(BF16)
    o_s = p_new * selnew_ref[:, LANE:2 * LANE]
    for pg in range(n_pages):
        o_s = o_s + lax.dot_general(pb[:, pg * PAGE_SIZE:(pg + 1) * PAGE_SIZE], sel_refs[2 * pg + 1][...].astype(BF16),
                                    _TT, preferred_element_type=F32)
    o_s = o_s / denom

    win = win_ref[...]
    s_w = jnp.dot(qq, win[0:LANE, :].astype(BF16), preferred_element_type=F32)
    col = lax.broadcasted_iota(jnp.int32, (1, n_win), 1)
    w_ok = (n_win - col < WINDOW) & (t - n_win + col >= 0)
    s_w = jnp.where(w_ok, s_w, -1e30)
    s_wn = jnp.sum(qq32 * winnew_ref[:, 0:LANE], axis=1, keepdims=True)
    m = jnp.maximum(jnp.max(s_w, axis=1, keepdims=True), s_wn)
    p = jnp.where(w_ok, jnp.exp2(s_w - m), 0.0)
    p_new = jnp.exp2(s_wn - m)
    denom = jnp.sum(p, axis=1, keepdims=True) + p_new
    o_w = (lax.dot_general(p.astype(BF16), win[LANE:2 * LANE, :].astype(BF16), _TT, preferred_element_type=F32)
           + p_new * winnew_ref[:, LANE:2 * LANE]) / denom

    o_ref[...] = gate_ref[:, 0:1] * o_c + gate_ref[:, 1:2] * o_s + gate_ref[:, 2:3] * o_w
    winout_ref[...] = jnp.where(col == n_win - 1, wcol_ref[...], pltpu.roll(win, n_win - 1, 1))


def nsa_sample(l, page_table, cache_cmp, cache_sel, cache_win, sel_new, win_new, qz, gate, cw):
    import numpy as np
    n, n_pages = page_table.shape
    past_len = n_pages * PAGE_SIZE
    n_win = cache_win.shape[2]
    depth, n_pool = cache_cmp.shape[:2]
    G, H = NSA_KV_HEADS, NSA_HPG
    R = G * H
    nch = past_len // CMP_STRIDE
    assert past_len // SEL_BLOCK + 1 <= LANE and n_win % 8 == 0
    w1bd, pe, b1, w2, b2 = cw
    w1p = w1bd.reshape(2, CMP_RATIO, CMP_STRIDE // 2, 2 * LANE, G * CMP_HIDDEN)
    pep = pe.reshape(2, CMP_RATIO, CMP_STRIDE // 2, 2 * LANE)
    c_end = jnp.arange(nch, dtype=jnp.int32) * CMP_STRIDE + CMP_BLOCK - 1
    tabs = rope_tables(c_end)[:3]
    c0 = np.arange(nch)[:, None] * CMP_STRIDE
    s0 = np.arange(LANE)[None, :] * SEL_BLOCK
    ov = (c0 < s0 + SEL_BLOCK) & (c0 + CMP_BLOCK > s0) & (np.arange(nch)[:, None] < nch - CMP_RATIO + 1)
    smap = jnp.asarray(ov, BF16)
    expand = jnp.asarray(np.where(np.arange(past_len)[None, :] // SEL_BLOCK == np.arange(LANE)[:, None], MASK_BIG, 0.0), BF16)
    qlo = qz.reshape(n, R, LANE).astype(F32)
    qq = jnp.concatenate([qlo[:, :H], jnp.roll(qlo[:, H:], HEAD_DIM, axis=-1)], axis=1)
    g3 = jnp.pad(gate[:, :, :3 * H].reshape(G, n, H, 3).transpose(1, 0, 2, 3).reshape(n, R, 3), ((0, 0), (0, 0), (0, LANE - 3)))
    cmp4 = cache_cmp.reshape(depth, n_pool, PAGE_SIZE, 2 * KV_W).transpose(0, 1, 3, 2)
    sel4 = cache_sel.reshape(depth, n_pool, PAGE_SIZE, 2 * KV_W).transpose(0, 1, 3, 2)
    win4 = cache_win.reshape(depth, n, n_win, 2 * KV_W).transpose(0, 1, 3, 2)

    def page_spec(p, half):
        return pl.BlockSpec((None, None, LANE, PAGE_SIZE), lambda b, pt: (l, pt[b, p], half, 0))

    page_specs = [page_spec(p, half) for p in range(n_pages) for half in range(2)]
    full = lambda shape: pl.BlockSpec(shape, lambda b, pt: (0,) * len(shape))
    per_seq = lambda shape: pl.BlockSpec((None,) + shape, lambda b, pt: (b,) + (0,) * len(shape))
    grid_spec = pltpu.PrefetchScalarGridSpec(
        num_scalar_prefetch=1,
        grid=(n,),
        in_specs=page_specs + page_specs + [
            pl.BlockSpec((None, None, 2 * KV_W, n_win), lambda b, pt: (l, b, 0, 0)),
            per_seq((1, 2 * KV_W)), per_seq((1, 2 * KV_W)), per_seq((R, LANE)), per_seq((R, LANE)), per_seq((R, LANE)),
            full(pep.shape), full(w1p.shape), full(b1.shape), full(w2.shape), full(b2.shape),
            full((nch, LANE)), full((nch, LANE)), full((nch, LANE)), full((nch, LANE)), full((LANE, past_len)),
            per_seq((2 * KV_W, 1))],
        out_specs=[per_seq((R, LANE)), per_seq((2 * KV_W, n_win))],
        scratch_shapes=[pltpu.VMEM((2, past_len, LANE), F32)],
    )
    o8, win_out = pl.pallas_call(
        functools.partial(_nsa_sample_body, n_pages=n_pages, past_len=past_len, n_win=n_win),
        grid_spec=grid_spec,
        out_shape=[jax.ShapeDtypeStruct((n, R, LANE), F32), jax.ShapeDtypeStruct((n, 2 * KV_W, n_win), F32)],
        compiler_params=pltpu.CompilerParams(dimension_semantics=("arbitrary",), vmem_limit_bytes=VMEM_LIMIT),
        name="nsa_sample",
    )(page_table, *([cmp4] * (2 * n_pages)), *([sel4] * (2 * n_pages)), win4,
      sel_new.reshape(n, 1, 2 * KV_W), win_new.reshape(n, 1, 2 * KV_W), qlo, qq, g3,
      pep, w1p, b1, w2, b2, *tabs, smap, expand, win_new.reshape(n, 2 * KV_W, 1))
    o5 = o8.reshape(n, G, H, G, HEAD_DIM)
    o = jnp.stack([o5[:, g, :, g, :] for g in range(G)], axis=1).reshape(n, NSA_WIDTH)
    return o, win_out.transpose(0, 2, 1)


def _rglru_sample_body(rg_ref, b0_ref, b1_ref, b2_ref, h0_ref, cw_ref, cb_ref, wg_ref, bg_ref, lam_ref, o_ref, h_ref):
    W = RG_WIDTH
    x = rg_ref[:, 0:W]
    xc = (cb_ref[...] + cw_ref[0:1, :] * b0_ref[...] + cw_ref[1:2, :] * b1_ref[...] + cw_ref[2:3, :] * b2_ref[...]
          + cw_ref[3:4, :] * x)
    gates = jnp.dot(xc.astype(BF16), wg_ref[...], preferred_element_type=F32) + bg_ref[...]
    r = jax.nn.sigmoid(gates[:, 0:W])
    i = jax.nn.sigmoid(gates[:, W:2 * W])
    log_a = -RG_C * r * jax.nn.softplus(-lam_ref[...])
    th = jnp.tanh(log_a)
    h = jnp.exp(log_a) * h0_ref[...] + jnp.sqrt(-2.0 * th / (1.0 - th)) * (i * xc)
    h_ref[...] = h
    o_ref[...] = h * jax.nn.gelu(rg_ref[:, W:2 * W])


def rglru_sample(rg, conv_buf, h0, conv_w, conv_b, wa, ba, wx, bx, lam):
    assert CONV_W == 4
    n = rg.shape[0]
    W = RG_WIDTH
    bd = jax.scipy.linalg.block_diag
    wg = jnp.concatenate([bd(*[wa[k] for k in range(RG_BLOCKS)]), bd(*[wx[k] for k in range(RG_BLOCKS)])], axis=1).astype(BF16)
    bg = jnp.concatenate([ba, bx]).reshape(1, 2 * W)
    out, h = pl.pallas_call(
        _rglru_sample_body,
        out_shape=[jax.ShapeDtypeStruct((n, W), F32), jax.ShapeDtypeStruct((n, W), F32)],
        name="rglru_sample",
    )(rg, conv_buf[:, 0], conv_buf[:, 1], conv_buf[:, 2], h0, conv_w, conv_b.reshape(1, W), wg, bg, lam.reshape(1, W))
    new_buf = jnp.concatenate([conv_buf[:, 1:], rg[:, None, 0:W]], axis=1)
    return out, h, new_buf


def _hgrn_sample_body(q_ref, f_ref, lb_ref, v_ref, g_ref, gain_ref, s0_ref, o_ref, s_ref):
    DK = HG_DK
    q = jax.nn.silu(q_ref[...])
    lb = lb_ref[...]
    f = lb + (1.0 - lb) * jax.nn.sigmoid(f_ref[...])
    k = 1.0 - f
    outs = []
    for h in range(HG_HEADS):
        r = slice(h * DK, (h + 1) * DK)
        s_new = f[r] * s0_ref[h] + k[r] * v_ref[h:h + 1, :]
        s_ref[h] = s_new
        outs.append(jnp.sum(q[r] * s_new, axis=0, keepdims=True))
    o = jnp.concatenate(outs, axis=0)
    o = o * lax.rsqrt(jnp.mean(o * o, axis=-1, keepdims=True) + EPS) * gain_ref[...]
    o_ref[...] = o * jax.nn.silu(g_ref[...])


def hgrn_sample(hg, s0, lb, gain):
    n = hg.shape[0]
    H, DK, DV, W = HG_HEADS, HG_DK, HG_DV, HG_WIDTH
    col = lambda a: a.reshape(n, W, 1)
    per_seq = lambda shape: pl.BlockSpec((None,) + shape, lambda b: (b,) + (0,) * len(shape))
    full = lambda shape: pl.BlockSpec(shape, lambda b: (0,) * len(shape))
    o, s_new = pl.pallas_call(
        _hgrn_sample_body,
        grid=(n,),
        in_specs=[per_seq((W, 1)), per_seq((W, 1)), full((W, 1)), per_seq((H, DV)), per_seq((H, DV)), full((H, DV)),
                  per_seq((H, DK, DV))],
        out_specs=[per_seq((H, DV)), per_seq((H, DK, DV))],
        out_shape=[jax.ShapeDtypeStruct((n, H, DV), F32), jax.ShapeDtypeStruct((n, H, DK, DV), F32)],
        compiler_params=pltpu.CompilerParams(dimension_semantics=("arbitrary",)),
        name="hgrn_sample",
    )(col(hg[:, 0:W]), col(hg[:, W:2 * W]), lb.reshape(W, 1), hg[:, 2 * W:3 * W].reshape(n, H, DV),
      hg[:, 3 * W:4 * W].reshape(n, H, DV), gain.reshape(H, DV), s0)
    return o.reshape(n, W), s_new


def _split(a, sizes):
    out, o = [], 0
    for s in sizes:
        out.append(a[..., o:o + s])
        o += s
    return out


def rms_norm(x, g):
    x32 = x.astype(jnp.float32)
    y = x32 * lax.rsqrt(jnp.mean(x32 * x32, axis=-1, keepdims=True) + EPS)
    return (y * g.astype(jnp.float32)).astype(x.dtype)


def partial_rope(x, pos):
    half = ROPE_DIM // 2
    inv = ROPE_THETA ** (-jnp.arange(half, dtype=jnp.float32) * 2.0 / ROPE_DIM)
    ang = pos.astype(jnp.float32)[:, None] * inv
    cos = jnp.cos(ang)[:, None, :].astype(x.dtype)
    sin = jnp.sin(ang)[:, None, :].astype(x.dtype)
    x1, x2, rest = x[..., :half], x[..., half:ROPE_DIM], x[..., ROPE_DIM:]
    return jnp.concatenate([x1 * cos - x2 * sin, x2 * cos + x1 * sin, rest], axis=-1)


def masked_softmax(s, mask):
    s = jnp.where(mask, s, -1e30)
    m = jnp.max(s, axis=-1, keepdims=True)
    e = jnp.where(mask, jnp.exp(s - m), 0.0)
    return e / jnp.maximum(jnp.sum(e, axis=-1, keepdims=True), 1e-30)


def compress_kv(raw, pos_emb, w1, b1, w2, b2):
    B, L = raw.shape[0], raw.shape[1]
    nch = L // CMP_STRIDE
    n_cmp = nch - CMP_RATIO + 1
    chunks = raw[:, :nch * CMP_STRIDE].reshape(B, nch, CMP_STRIDE, NSA_KV_HEADS, HEAD_DIM)
    h = b1
    for r in range(CMP_RATIO):
        sl = slice(r * CMP_STRIDE, (r + 1) * CMP_STRIDE)
        pre = jnp.einsum('bnsgd,sdh->bngh', chunks + pos_emb[sl][:, None, :], w1[sl])
        h = h + pre[:, r:r + n_cmp]
    out = jax.nn.gelu(h) @ w2 + b2
    end = jnp.arange(n_cmp, dtype=jnp.int32) * CMP_STRIDE + CMP_BLOCK - 1
    return out, end


def selection_map(n_cmp, n_sel):
    c0 = jnp.arange(n_cmp) * CMP_STRIDE
    s0 = jnp.arange(n_sel) * SEL_BLOCK
    ov = (c0[:, None] < s0[None, :] + SEL_BLOCK) & (c0[:, None] + CMP_BLOCK > s0[None, :])
    return ov.astype(jnp.float32)


def nsa_attend_block(q, gate, t_pos, kw, vw, w_pos, k_cmp, v_cmp, c_end, ks_blk, vs_blk, smap):
    f32 = jnp.float32
    scale = HEAD_DIM ** -0.5
    s_c = jnp.einsum('bqghd,bngd->bqghn', q, k_cmp).astype(f32) * scale
    m_c = (c_end[None, :] <= t_pos[:, None])[None, :, None, None, :]
    p_c = masked_softmax(s_c, m_c)
    o_c = jnp.einsum('bqghn,bngd->bqghd', p_c.astype(v_cmp.dtype), v_cmp)
    imp = jnp.einsum('bqghn,nj->bqgj', p_c, smap)
    n_sel = ks_blk.shape[1]
    j = jnp.arange(n_sel)[None, :]
    cur = (t_pos // SEL_BLOCK)[:, None]
    valid = j * SEL_BLOCK <= t_pos[:, None]
    forced = (j == 0) | (j == cur) | (j == cur - 1)
    score = jnp.where(valid[None, :, None, :], jnp.where(forced[None, :, None, :], FORCE_SCORE, imp), -FORCE_SCORE)
    _, idx = lax.top_k(score, min(SEL_TOPK, n_sel))
    bi = jnp.arange(q.shape[0])[:, None, None, None]
    gi = jnp.arange(NSA_KV_HEADS)[None, None, :, None]
    kb = ks_blk[bi, idx, gi]
    vb = vs_blk[bi, idx, gi]
    kpos = idx[..., None] * SEL_BLOCK + jnp.arange(SEL_BLOCK)
    m_s = (kpos <= t_pos[None, :, None, None, None])[:, :, :, None]
    s_s = jnp.einsum('bqghd,bqgkld->bqghkl', q, kb).astype(f32) * scale
    shp = s_s.shape
    p_s = masked_softmax(s_s.reshape(*shp[:4], -1), m_s.reshape(*m_s.shape[:4], -1)).reshape(shp)
    o_s = jnp.einsum('bqghkl,bqgkld->bqghd', p_s.astype(vb.dtype), vb)
    s_w = jnp.einsum('bqghd,bwgd->bqghw', q, kw).astype(f32) * scale
    dist = t_pos[:, None] - w_pos[None, :]
    m_w = ((dist >= 0) & (dist < WINDOW) & (w_pos[None, :] >= 0))[None, :, None, None, :]
    p_w = masked_softmax(s_w, m_w)
    o_w = jnp.einsum('bqghw,bwgd->bqghd', p_w.astype(vw.dtype), vw)
    return gate[..., 0:1] * o_c + gate[..., 1:2] * o_s + gate[..., 2:3] * o_w


def nsa_mixer(q, kvs, gate, pos, cmp_pos, cmp_w1, cmp_b1, cmp_w2, cmp_b2, past):
    B, T = q.shape[:2]
    G, HD = NSA_KV_HEADS, HEAD_DIM
    q = partial_rope(q.reshape(B, T, NSA_HEADS, HD), pos).reshape(B, T, G, NSA_HPG, HD)
    kc, vc, ks, vs, kw, vw = [a.reshape(B, T, G, HD) for a in _split(kvs, (KV_W,) * 6)]
    ks = partial_rope(ks, pos)
    kw = partial_rope(kw, pos)
    gate = jax.nn.sigmoid(gate.reshape(B, T, G, NSA_HPG, 3))
    new_cmp = jnp.stack([kc, vc], axis=2)
    new_sel = jnp.stack([ks, vs], axis=2)
    new_win = jnp.stack([kw, vw], axis=2)
    if past is None:
        cmp_all, sel_all = new_cmp, new_sel
        win_all = jnp.pad(new_win, ((0, 0), (WINDOW, 0), (0, 0), (0, 0), (0, 0)))
        w_pos = pos[0] - WINDOW + jnp.arange(T + WINDOW, dtype=jnp.int32)
        win_state = new_win[:, T - min(WINDOW, T):]
    else:
        cmp_past, sel_past, win_buf = past
        cmp_all = jnp.concatenate([cmp_past, new_cmp], axis=1)
        sel_all = jnp.concatenate([sel_past, new_sel], axis=1)
        win_all = jnp.concatenate([win_buf, new_win], axis=1)
        nb = win_buf.shape[1]
        w_pos = pos[0] - nb + jnp.arange(nb + T, dtype=jnp.int32)
        win_state = win_all[:, T:]
    L = cmp_all.shape[1]
    k_cmp, c_end = compress_kv(cmp_all[:, :, 0], cmp_pos[0], cmp_w1[0], cmp_b1[0], cmp_w2[0], cmp_b2[0])
    v_cmp, _ = compress_kv(cmp_all[:, :, 1], cmp_pos[1], cmp_w1[1], cmp_b1[1], cmp_w2[1], cmp_b2[1])
    k_cmp = partial_rope(k_cmp, c_end)
    n_sel = -(-L // SEL_BLOCK)
    sel_pad = jnp.pad(sel_all, ((0, 0), (0, n_sel * SEL_BLOCK - L), (0, 0), (0, 0), (0, 0)))
    sel_blk = sel_pad.reshape(B, n_sel, SEL_BLOCK, 2, G, HD).transpose(3, 0, 1, 4, 2, 5)
    smap = selection_map(k_cmp.shape[1], n_sel)
    kw_all, vw_all = win_all[:, :, 0], win_all[:, :, 1]
    if past is None and T > Q_BLOCK and T % Q_BLOCK == 0:
        def one_block(n):
            q0 = n * Q_BLOCK
            return nsa_attend_block(
                lax.dynamic_slice_in_dim(q, q0, Q_BLOCK, axis=1),
                lax.dynamic_slice_in_dim(gate, q0, Q_BLOCK, axis=1),
                lax.dynamic_slice_in_dim(pos, q0, Q_BLOCK, axis=0),
                lax.dynamic_slice_in_dim(kw_all, q0, Q_BLOCK + WINDOW, axis=1),
                lax.dynamic_slice_in_dim(vw_all, q0, Q_BLOCK + WINDOW, axis=1),
                lax.dynamic_slice_in_dim(w_pos, q0, Q_BLOCK + WINDOW, axis=0),
                k_cmp, v_cmp, c_end, sel_blk[0], sel_blk[1], smap)
        o = lax.map(one_block, jnp.arange(T // Q_BLOCK))
        o = jnp.moveaxis(o, 0, 1).reshape(B, T, NSA_WIDTH)
    else:
        o = nsa_attend_block(q, gate, pos, kw_all, vw_all, w_pos, k_cmp, v_cmp, c_end,
                             sel_blk[0], sel_blk[1], smap).reshape(B, T, NSA_WIDTH)
    return o, new_cmp, new_sel, win_state


def _lin_combine(e1, e2):
    a1, b1 = e1
    a2, b2 = e2
    return a1 * a2, a2 * b1 + b2


def rglru_mixer(xr, gr, conv_buf, h0, conv_w, conv_b, wa, ba, wx, bx, lam):
    B, T = xr.shape[:2]
    f32 = jnp.float32
    xcat = jnp.concatenate([conv_buf.astype(xr.dtype), xr], axis=1)
    xc = conv_b + sum(conv_w[k] * xcat[:, k:k + T] for k in range(CONV_W))
    new_buf = xcat[:, T:]
    xb = xc.reshape(B, T, RG_BLOCKS, RG_BW)
    r = jax.nn.sigmoid(jnp.einsum('btnd,nde->btne', xb, wa).reshape(B, T, RG_WIDTH) + ba)
    i = jax.nn.sigmoid(jnp.einsum('btnd,nde->btne', xb, wx).reshape(B, T, RG_WIDTH) + bx)
    log_a = -RG_C * r.astype(f32) * jax.nn.softplus(-lam.astype(f32))
    a = jnp.exp(log_a)
    b = jnp.sqrt(-jnp.expm1(2.0 * log_a)) * (i * xc).astype(f32)
    b = b.at[:, 0].add(a[:, 0] * h0.astype(f32))
    _, h = lax.associative_scan(_lin_combine, (a, b), axis=1)
    out = h.astype(xr.dtype) * jax.nn.gelu(gr)
    return out, h[:, -1].astype(h0.dtype), new_buf


def gated_recurrence(q, k, v, logf, s0):
    B, T, H, DK = q.shape
    DV = v.shape[-1]
    C = HG_CHUNK if T % HG_CHUNK == 0 else T
    nc = T // C

    def to_chunks(a):
        return jnp.moveaxis(a.reshape(B, nc, C, *a.shape[2:]), 1, 0)

    tri = jnp.tril(jnp.ones((C, C), dtype=bool))[None, :, :, None, None]

    def step(S, inp):
        qc, kc, vc, gc = inp
        bcum = jnp.cumsum(gc, axis=1)
        o = jnp.einsum('bthk,bhkv->bthv', qc * jnp.exp(bcum), S)
        dec = jnp.exp(jnp.where(tri, bcum[:, :, None] - bcum[:, None, :], -jnp.inf))
        A = jnp.einsum('bthk,bshk,btshk->bhts', qc, kc, dec)
        o = o + jnp.einsum('bhts,bshv->bthv', A, vc)
        bl = bcum[:, -1]
        S = jnp.exp(bl)[..., None] * S + jnp.einsum('bshk,bshv->bhkv', kc * jnp.exp(bl[:, None] - bcum), vc)
        return S, o

    s_fin, o = lax.scan(step, s0, (to_chunks(q), to_chunks(k), to_chunks(v), to_chunks(logf)))
    return jnp.moveaxis(o, 0, 1).reshape(B, T, H, DV), s_fin


def hgrn2_mixer(hq, hf, hi, hg, s0, lb, gain):
    B, T = hq.shape[:2]
    f32 = jnp.float32
    q = jax.nn.silu(hq.astype(f32)).reshape(B, T, HG_HEADS, HG_DK)
    lbh = lb.reshape(HG_HEADS, HG_DK)
    f = lbh + (1.0 - lbh) * jax.nn.sigmoid(hf.astype(f32).reshape(B, T, HG_HEADS, HG_DK))
    v = hi.astype(f32).reshape(B, T, HG_HEADS, HG_DV)
    o, s_fin = gated_recurrence(q, 1.0 - f, v, jnp.log(f), s0.astype(f32))
    o = rms_norm(o, gain.reshape(HG_HEADS, HG_DV)).reshape(B, T, HG_WIDTH).astype(hq.dtype)
    return o * jax.nn.silu(hg), s_fin.astype(s0.dtype)


def forward_layer(x, pos, w, wb, past):
    (norm_mix, w_in, w_out, norm_ffn, w_up, w_down, cmp_pos, cmp_w1, cmp_b1, cmp_w2, cmp_b2,
     rg_conv_w, rg_conv_b, rg_wa, rg_ba, rg_wx, rg_bx, rg_lambda, hg_lb, hg_gain) = w
    w_in_b, w_out_b, w_up_b, w_down_b = wb
    B, T = x.shape[:2]
    x2 = x.reshape(B * T, D_MODEL)
    proj = norm_matmul(x2, norm_mix, w_in_b)[:, :D_IN].reshape(B, T, D_IN)
    q, kvs, gate, rg_x, rg_g, hg_q, hg_f, hg_i, hg_g = _split(proj, IN_SIZES)
    if past is None:
        nsa_past = None
        rg_buf = jnp.zeros((B, CONV_W - 1, RG_WIDTH), x.dtype)
        rg_h0 = jnp.zeros((B, RG_WIDTH), x.dtype)
        hg_s0 = jnp.zeros((B, HG_HEADS, HG_DK, HG_DV), x.dtype)
    else:
        cmp_past, sel_past, win_buf, rg_h0, rg_buf, hg_s0 = past
        nsa_past = (cmp_past, sel_past, win_buf)
    o_nsa, new_cmp, new_sel, new_win = nsa_mixer(q, kvs, gate, pos, cmp_pos, cmp_w1, cmp_b1, cmp_w2, cmp_b2, nsa_past)
    o_rg, new_h, new_buf = rglru_mixer(rg_x, rg_g, rg_buf, rg_h0, rg_conv_w, rg_conv_b, rg_wa, rg_ba, rg_wx, rg_bx, rg_lambda)
    o_hg, new_s = hgrn2_mixer(hg_q, hg_f, hg_i, hg_g, hg_s0, hg_lb, hg_gain)
    mix = jnp.concatenate([o_nsa, o_rg, o_hg], axis=-1).reshape(B * T, MIX_WIDTH)
    x2 = out_ffn(x2, mix, w_out_b, norm_ffn, w_up_b, w_down_b)
    return x2.reshape(B, T, D_MODEL), (new_cmp, new_sel, new_win, new_h, new_buf, new_s)


def forward_layer_prompt(x, w, wb, pw):
    (norm_mix, w_in, w_out, norm_ffn, w_up, w_down, cmp_pos, cmp_w1, cmp_b1, cmp_w2, cmp_b2,
     rg_conv_w, rg_conv_b, rg_wa, rg_ba, rg_wx, rg_bx, rg_lambda, hg_lb, hg_gain) = w
    _, w_out_b, w_up_b, w_down_b = wb
    w_rel, cw, tables, tables_cmp = pw
    B, T = x.shape[:2]
    x2 = x.reshape(B * T, D_MODEL)
    qz, cmp, sel, win, kvs, kvw, gate, rg, hg = in_proj_prompt(x2, norm_mix, w_rel, tables, T)
    kvc = compress(cmp, cw, tables_cmp, T)
    o_nsa = nsa_prompt(qz, gate, kvc, kvs, kvw, B, T)
    kv_shape = (B, T, 2, NSA_KV_HEADS, HEAD_DIM)
    new_cmp, new_sel = cmp.reshape(kv_shape), sel.reshape(kv_shape)
    new_win = win.reshape(kv_shape)[:, T - min(WINDOW, T):]
    rg_buf = jnp.zeros((B, CONV_W - 1, RG_WIDTH), x.dtype)
    rg_h0 = jnp.zeros((B, RG_WIDTH), x.dtype)
    hg_s0 = jnp.zeros((B, HG_HEADS, HG_DK, HG_DV), x.dtype)
    o_rg, new_h, new_buf = rglru_prompt(rg, rg_buf, rg_h0, rg_conv_w, rg_conv_b, rg_wa, rg_ba, rg_wx, rg_bx,
                                        rg_lambda, T)
    o_hg, new_s = hgrn_prompt(hg, hg_s0, hg_lb, hg_gain, T)
    mix = jnp.concatenate([o_nsa, o_rg, o_hg], axis=-1)
    x2 = out_ffn(x2, mix, w_out_b, norm_ffn, w_up_b, w_down_b)
    return x2.reshape(B, T, D_MODEL), (new_cmp, new_sel, new_win, new_h, new_buf, new_s)


def forward_layer_sample(l, x, w, wb, pw, caches):
    (norm_mix, w_in, w_out, norm_ffn, w_up, w_down, cmp_pos, cmp_w1, cmp_b1, cmp_w2, cmp_b2,
     rg_conv_w, rg_conv_b, rg_wa, rg_ba, rg_wx, rg_bx, rg_lambda, hg_lb, hg_gain) = w
    _, w_out_b, w_up_b, w_down_b = wb
    w_rel, cw, tables_s = pw
    page_table, cache_cmp, cache_sel, cache_win, rg_h0, rg_buf, hg_s0 = caches
    n = x.shape[0]
    x2 = x.reshape(n, D_MODEL)
    qz, cmp_new, sel_new, win_new, _, _, gate, rg, hg = in_proj_prompt(x2, norm_mix, w_rel, tables_s, n)
    o_nsa, win_out = nsa_sample(l, page_table, cache_cmp, cache_sel, cache_win, sel_new, win_new, qz, gate, cw)
    o_rg, new_h, new_buf = rglru_sample(rg, rg_buf, rg_h0, rg_conv_w, rg_conv_b, rg_wa, rg_ba, rg_wx, rg_bx, rg_lambda)
    o_hg, new_s = hgrn_sample(hg, hg_s0, hg_lb, hg_gain)
    mix = jnp.concatenate([o_nsa, o_rg, o_hg], axis=-1)
    x2 = out_ffn(x2, mix, w_out_b, norm_ffn, w_up_b, w_down_b)
    kv_shape = (n, 1, 2, NSA_KV_HEADS, HEAD_DIM)
    new_win = win_out.reshape(n, win_out.shape[1], 2, NSA_KV_HEADS, HEAD_DIM)
    return x2.reshape(n, 1, D_MODEL), (cmp_new.reshape(kv_shape), sel_new.reshape(kv_shape), new_win, new_h, new_buf, new_s)


def kernel(x_prompt, x_sample, cache_nsa_cmp_kv, cache_nsa_sel_kv, cache_nsa_win_kv, state_rglru_h,
           state_rglru_conv, state_hgrn_s, page_table, norm_mix, w_in, w_out, norm_ffn, w_up, w_down,
           cmp_pos, cmp_w1, cmp_b1, cmp_w2, cmp_b2, rg_conv_w, rg_conv_b, rg_wa, rg_ba, rg_wx, rg_bx,
           rg_lambda, hg_lower_bounds, hg_gain, final_norm):
    lb = jnp.cumsum(jax.nn.softmax(hg_lower_bounds.astype(jnp.float32), axis=0), axis=0)
    lb = lb - lb[0]
    past_len = page_table.shape[1] * PAGE_SIZE
    n_dec = x_sample.shape[0]
    pos_p = jnp.arange(x_prompt.shape[1], dtype=jnp.int32)
    pos_s = past_len + jnp.arange(x_sample.shape[1], dtype=jnp.int32)
    d_in_pad = _round_up(D_IN, LANE)
    w_in_b = jnp.pad(w_in, ((0, 0), (0, 0), (0, d_in_pad - D_IN))).astype(BF16)
    w_out_b = w_out.astype(BF16)
    w_up_b = w_up.astype(BF16)
    w_down_b = w_down.astype(BF16)
    w_rel = relayout_w_in(w_in)
    cws = compress_weights(cmp_pos, cmp_w1, cmp_b1, cmp_w2, cmp_b2)
    T = x_prompt.shape[1]
    tables = rope_tables(pos_p)
    c_end_p = jnp.arange(T // CMP_STRIDE, dtype=jnp.int32) * CMP_STRIDE + CMP_BLOCK - 1
    tables_cmp = rope_tables(c_end_p)[:3]
    assert x_sample.shape[1] == 1
    tables_s = rope_tables(jnp.full((n_dec,), past_len, jnp.int32))
    xp, xs = x_prompt, x_sample
    st_p, st_s = [], []
    for l in range(DEPTH):
        w = (norm_mix[l], w_in[l], w_out[l], norm_ffn[l], w_up[l], w_down[l], cmp_pos[l], cmp_w1[l], cmp_b1[l],
             cmp_w2[l], cmp_b2[l], rg_conv_w[l], rg_conv_b[l], rg_wa[l], rg_ba[l], rg_wx[l], rg_bx[l],
             rg_lambda[l], lb[l], hg_gain[l])
        wb = (w_in_b[l], w_out_b[l], w_up_b[l], w_down_b[l])
        pw = (w_rel[l], tuple(a[l] for a in cws), tables, tables_cmp)
        xp, sp = forward_layer_prompt(xp, w, wb, pw)
        caches = (page_table, cache_nsa_cmp_kv, cache_nsa_sel_kv, cache_nsa_win_kv,
                  state_rglru_h[l], state_rglru_conv[l], state_hgrn_s[l])
        xs, ss = forward_layer_sample(l, xs, w, wb, (w_rel[l], pw[1], tables_s), caches)
        st_p.append(sp)
        st_s.append(ss)

    def stack(sts, i):
        return jnp.stack([s[i] for s in sts], axis=0)

    y_prompt = final_rms(xp.reshape(-1, D_MODEL), final_norm).reshape(xp.shape)
    y_sample = final_rms(xs.reshape(-1, D_MODEL), final_norm).reshape(xs.shape)
    return (y_prompt, y_sample, stack(st_p, 0), stack(st_p, 1), stack(st_p, 2), stack(st_p, 3), stack(st_p, 4),
            stack(st_p, 5), stack(st_s, 0), stack(st_s, 1), stack(st_s, 2), stack(st_s, 3), stack(st_s, 4),
            stack(st_s, 5))
```

```python
import functools

import jax
import jax.numpy as jnp
from jax import lax
from jax.experimental import pallas as pl
from jax.experimental.pallas import tpu as pltpu

F32 = jnp.float32
BF16 = jnp.bfloat16

D_MODEL = 1024
DEPTH = 4
PAGE_SIZE = 128
HEAD_DIM = 64
NSA_HEADS = D_MODEL // (2 * HEAD_DIM)
NSA_KV_HEADS = 2
NSA_HPG = NSA_HEADS // NSA_KV_HEADS
NSA_WIDTH = NSA_HEADS * HEAD_DIM
KV_W = NSA_KV_HEADS * HEAD_DIM
CMP_BLOCK = 32
CMP_STRIDE = 16
CMP_RATIO = CMP_BLOCK // CMP_STRIDE
CMP_HIDDEN = 256
SEL_BLOCK = 64
SEL_TOPK = 16
WINDOW = 512
Q_BLOCK = 128
FORCE_SCORE = 1e6
ROPE_DIM = HEAD_DIM // 4
ROPE_THETA = 500000.0
RG_WIDTH = D_MODEL // 4
RG_BLOCKS = 4
RG_BW = RG_WIDTH // RG_BLOCKS
RG_C = 8.0
CONV_W = 4
HG_HEADS = 4
HG_DK = 64
HG_DV = D_MODEL // 4 // HG_HEADS
HG_WIDTH = HG_HEADS * HG_DV
HG_CHUNK = 64
MIX_WIDTH = NSA_WIDTH + RG_WIDTH + HG_WIDTH
D_FF = 4 * D_MODEL
EPS = 1e-6

LANE = 128
VMEM_LIMIT = 56 * 1024 * 1024


def _row_tile(m, want):
    t = min(m, want)
    while m % t:
        t //= 2
    return t


def _rms(x, g):
    return x * lax.rsqrt(jnp.mean(x * x, axis=-1, keepdims=True) + EPS) * g


def _out_ffn_body(x_ref, mix_ref, wo_ref, g_ref, wu_ref, wd_ref, o_ref, *, f_tile):
    x1 = x_ref[...] + jnp.dot(mix_ref[...].astype(BF16), wo_ref[...], preferred_element_type=F32)
    y = _rms(x1, g_ref[...]).astype(BF16)
    acc = x1
    for j in range(wu_ref.shape[1] // f_tile):
        h = jnp.maximum(jnp.dot(y, wu_ref[:, j * f_tile:(j + 1) * f_tile], preferred_element_type=F32), 0.0)
        acc = acc + jnp.dot((h * h).astype(BF16), wd_ref[j * f_tile:(j + 1) * f_tile, :], preferred_element_type=F32)
    o_ref[...] = acc


def out_ffn(x, mix, wo, g, wu, wd, tm=512, f_tile=512):
    m, d = x.shape
    dm = mix.shape[1]
    dff = wu.shape[1]
    tm = _row_tile(m, tm)
    const = lambda i: (0, 0)
    return pl.pallas_call(
        functools.partial(_out_ffn_body, f_tile=f_tile),
        grid=(m // tm,),
        in_specs=[
            pl.BlockSpec((tm, d), lambda i: (i, 0)),
            pl.BlockSpec((tm, dm), lambda i: (i, 0)),
            pl.BlockSpec((dm, d), const),
            pl.BlockSpec((1, d), const),
            pl.BlockSpec((d, dff), const),
            pl.BlockSpec((dff, d), const),
        ],
        out_specs=pl.BlockSpec((tm, d), lambda i: (i, 0)),
        out_shape=jax.ShapeDtypeStruct((m, d), F32),
        compiler_params=pltpu.CompilerParams(
            dimension_semantics=("arbitrary",), vmem_limit_bytes=VMEM_LIMIT),
        name="out_proj_ffn",
    )(x, mix, wo, g.reshape(1, d), wu, wd)


def _final_norm_body(x_ref, g_ref, o_ref):
    o_ref[...] = _rms(x_ref[...], g_ref[...])


def final_rms(x, g, tm=1024):
    m, d = x.shape
    tm = _row_tile(m, tm)
    return pl.pallas_call(
        _final_norm_body,
        grid=(m // tm,),
        in_specs=[pl.BlockSpec((tm, d), lambda i: (i, 0)), pl.BlockSpec((1, d), lambda i: (0, 0))],
        out_specs=pl.BlockSpec((tm, d), lambda i: (i, 0)),
        out_shape=jax.ShapeDtypeStruct((m, d), F32),
        name="final_norm",
    )(x, g.reshape(1, d))


MASK_BIG = 2.0 ** 100
LOG2E = 1.4426950408889634

_O_Q = 0
_O_KV = NSA_WIDTH
_O_GATE = _O_KV + 6 * KV_W
_O_RG = _O_GATE + 3 * NSA_HEADS
_O_HG = _O_RG + 2 * RG_WIDTH

_P_QZ = 0
_P_CMP = _P_QZ + NSA_HEADS * LANE
_P_SEL = _P_CMP + 2 * KV_W
_P_WIN = _P_SEL + 2 * KV_W
_P_KVS = _P_WIN + 2 * KV_W
_P_KVW = _P_KVS + 2 * KV_W
_P_GATE = _P_KVW + 2 * KV_W
_P_RG = _P_GATE + NSA_KV_HEADS * LANE
_P_HG = _P_RG + 2 * RG_WIDTH
_P_END = _P_HG + 4 * HG_WIDTH


def _in_proj_columns():
    import numpy as np
    src = -np.ones((_P_END,), np.int64)
    for hq in range(NSA_HEADS):
        src[_P_QZ + hq * LANE:_P_QZ + hq * LANE + HEAD_DIM] = _O_Q + hq * HEAD_DIM + np.arange(HEAD_DIM)
    src[_P_CMP:_P_CMP + 2 * KV_W] = _O_KV + np.arange(2 * KV_W)
    src[_P_SEL:_P_SEL + 2 * KV_W] = _O_KV + 2 * KV_W + np.arange(2 * KV_W)
    src[_P_WIN:_P_WIN + 2 * KV_W] = _O_KV + 4 * KV_W + np.arange(2 * KV_W)
    for base, off in ((_P_KVS, _O_KV + 2 * KV_W), (_P_KVW, _O_KV + 4 * KV_W)):
        for g in range(NSA_KV_HEADS):
            src[base + g * LANE:base + g * LANE + HEAD_DIM] = off + g * HEAD_DIM + np.arange(HEAD_DIM)
            src[base + g * LANE + HEAD_DIM:base + (g + 1) * LANE] = off + KV_W + g * HEAD_DIM + np.arange(HEAD_DIM)
    ng = 3 * NSA_HPG
    for g in range(NSA_KV_HEADS):
        src[_P_GATE + g * LANE:_P_GATE + g * LANE + ng] = _O_GATE + g * ng + np.arange(ng)
    src[_P_RG:_P_RG + 2 * RG_WIDTH] = _O_RG + np.arange(2 * RG_WIDTH)
    src[_P_HG:_P_HG + 4 * HG_WIDTH] = _O_HG + np.arange(4 * HG_WIDTH)
    return src


def relayout_w_in(w_in):
    import numpy as np
    src = _in_proj_columns()
    cols = jnp.take(w_in, jnp.asarray(np.maximum(src, 0), jnp.int32), axis=-1)
    return jnp.where(jnp.asarray(src >= 0), cols, 0.0).astype(BF16)


def rope_tables(pos):
    import numpy as np
    half = ROPE_DIM // 2
    inv = ROPE_THETA ** (-jnp.arange(half, dtype=F32) * 2.0 / ROPE_DIM)
    ang = pos.astype(F32)[:, None] * inv
    cos, sin = jnp.cos(ang), jnp.sin(ang)
    lane = np.arange(LANE)
    d = lane % HEAD_DIM
    idx = jnp.asarray(d % half, jnp.int32)
    cos_l, sin_l = jnp.take(cos, idx, axis=1), jnp.take(sin, idx, axis=1)
    out = []
    for active in (lane < ROPE_DIM, d < ROPE_DIM):
        first = jnp.asarray(active & (d < half))
        second = jnp.asarray(active & (d >= half) & (d < ROPE_DIM))
        out += [jnp.where(first | second, cos_l, 1.0), jnp.where(first, -sin_l, 0.0), jnp.where(second, sin_l, 0.0)]
    return tuple(out)


def _rope128(x, c, s1, s2):
    return x * c + pltpu.roll(x, LANE - ROPE_DIM // 2, 1) * s1 + pltpu.roll(x, ROPE_DIM // 2, 1) * s2


def _in_proj_prompt_body(x_ref, g_ref, w_ref, cq_ref, s1q_ref, s2q_ref, ckk_ref, s1kk_ref, s2kk_ref,
                         qz_ref, cmp_ref, sel_ref, win_ref, kvs_ref, kvw_ref, gate_ref, rg_ref, hg_ref,
                         *, tiles_per_seq):
    tm = x_ref.shape[0]
    y = _rms(x_ref[...], g_ref[...]).astype(BF16)

    def mm(c0, n):
        return jnp.dot(y, w_ref[:, c0:c0 + n], preferred_element_type=F32)

    cq, s1q, s2q = cq_ref[...], s1q_ref[...], s2q_ref[...]
    ckk, s1kk, s2kk = ckk_ref[...], s1kk_ref[...], s2kk_ref[...]
    scale = HEAD_DIM ** -0.5 * LOG2E
    for hq in range(NSA_HEADS):
        qh = _rope128(mm(_P_QZ + hq * LANE, LANE), cq, s1q, s2q) * scale
        qz_ref[:, hq * LANE:(hq + 1) * LANE] = qh.astype(BF16)
    cmp_ref[...] = mm(_P_CMP, 2 * KV_W)
    for base, ref in ((_P_SEL, sel_ref), (_P_WIN, win_ref)):
        ref[:, 0:LANE] = _rope128(mm(base, LANE), ckk, s1kk, s2kk)
        ref[:, LANE:2 * LANE] = mm(base + LANE, LANE)
    row0 = (pl.program_id(0) % tiles_per_seq) * tm
    blk = (row0 + lax.broadcasted_iota(jnp.int32, (tm, 1), 0)) // SEL_BLOCK
    onehot = jnp.where(lax.broadcasted_iota(jnp.int32, (1, LANE), 1) == blk, MASK_BIG, 0.0).astype(BF16)
    for g in range(NSA_KV_HEADS):
        kvs_ref[g, :, 0:LANE] = _rope128(mm(_P_KVS + g * LANE, LANE), cq, s1q, s2q).astype(BF16)
        kvs_ref[g, :, LANE:2 * LANE] = onehot
        kvw_ref[g] = _rope128(mm(_P_KVW + g * LANE, LANE), cq, s1q, s2q).astype(BF16)
        gate_ref[g] = jax.nn.sigmoid(mm(_P_GATE + g * LANE, LANE))
    rg_ref[...] = mm(_P_RG, 2 * RG_WIDTH)
    hg_ref[...] = mm(_P_HG, 4 * HG_WIDTH)


def in_proj_prompt(x, g, w_rel, tables, seq_len, tm=512):
    m, d = x.shape
    tm = _row_tile(seq_len, tm)
    tiles_per_seq = seq_len // tm
    G = NSA_KV_HEADS
    row = lambda i: (i, 0)
    grow = lambda i: (0, i, 0)
    tab = pl.BlockSpec((tm, LANE), lambda i: (i % tiles_per_seq, 0))
    return pl.pallas_call(
        functools.partial(_in_proj_prompt_body, tiles_per_seq=tiles_per_seq),
        grid=(m // tm,),
        in_specs=[pl.BlockSpec((tm, d), row), pl.BlockSpec((1, d), lambda i: (0, 0)),
                  pl.BlockSpec((d, _P_END), lambda i: (0, 0))] + [tab] * 6,
        out_specs=[
            pl.BlockSpec((tm, NSA_HEADS * LANE), row),
            pl.BlockSpec((tm, 2 * KV_W), row), pl.BlockSpec((tm, 2 * KV_W), row), pl.BlockSpec((tm, 2 * KV_W), row),
            pl.BlockSpec((G, tm, 2 * LANE), grow), pl.BlockSpec((G, tm, LANE), grow), pl.BlockSpec((G, tm, LANE), grow),
            pl.BlockSpec((tm, 2 * RG_WIDTH), row), pl.BlockSpec((tm, 4 * HG_WIDTH), row),
        ],
        out_shape=[
            jax.ShapeDtypeStruct((m, NSA_HEADS * LANE), BF16),
            jax.ShapeDtypeStruct((m, 2 * KV_W), F32), jax.ShapeDtypeStruct((m, 2 * KV_W), F32),
            jax.ShapeDtypeStruct((m, 2 * KV_W), F32),
            jax.ShapeDtypeStruct((G, m, 2 * LANE), BF16), jax.ShapeDtypeStruct((G, m, LANE), BF16),
            jax.ShapeDtypeStruct((G, m, LANE), F32),
            jax.ShapeDtypeStruct((m, 2 * RG_WIDTH), F32), jax.ShapeDtypeStruct((m, 4 * HG_WIDTH), F32),
        ],
        compiler_params=pltpu.CompilerParams(
            dimension_semantics=("arbitrary",), vmem_limit_bytes=VMEM_LIMIT),
        name="in_proj_prompt",
    )(x, g.reshape(1, d), w_rel, *tables)


def compress_weights(cmp_pos, cmp_w1, cmp_b1, cmp_w2, cmp_b2):
    L = cmp_w1.shape[0]
    G, S, HD, CH = NSA_KV_HEADS, CMP_STRIDE, HEAD_DIM, CMP_HIDDEN
    w1 = cmp_w1.reshape(L, 2, CMP_RATIO, S, HD, CH)
    z = jnp.zeros_like(w1)
    w1bd = jnp.concatenate([jnp.concatenate([w1, z], axis=-1), jnp.concatenate([z, w1], axis=-1)], axis=-2)
    pe = jnp.tile(cmp_pos.reshape(L, 2, CMP_RATIO, S, HD), (1, 1, 1, 1, G))
    b1 = jnp.tile(cmp_b1, (1, 1, G)).reshape(L, 2, 1, G * CH)
    zc = jnp.zeros((L, CH, HD), F32)
    w2 = jnp.stack([jnp.concatenate([cmp_w2[:, 0], zc], axis=-1), jnp.concatenate([zc, cmp_w2[:, 1]], axis=-1)], axis=1)
    b2 = jnp.concatenate([cmp_b2[:, 0], cmp_b2[:, 1]], axis=-1).reshape(L, 1, 2 * HD)
    return w1bd.astype(BF16), pe, b1, w2.astype(BF16), b2


def _compress_body(xk_ref, xv_ref, pe_ref, w1_ref, b1_ref, w2_ref, b2_ref, c_ref, s1_ref, s2_ref, o_ref, *, nch, n_cmp):
    gelus = []
    for kv, x_ref in enumerate((xk_ref, xv_ref)):
        acc = [jnp.zeros((nch, NSA_KV_HEADS * CMP_HIDDEN), F32) for _ in range(CMP_RATIO)]
        for s in range(CMP_STRIDE):
            xs = x_ref[pl.ds(s, nch, stride=CMP_STRIDE), :]
            for r in range(CMP_RATIO):
                acc[r] = acc[r] + jnp.dot((xs + pe_ref[kv, r, s:s + 1, :]).astype(BF16), w1_ref[kv, r, s],
                                          preferred_element_type=F32)
        h = b1_ref[kv] + acc[0] + pltpu.roll(acc[1], nch - 1, 0)
        gelus.append(jax.nn.gelu(h).astype(BF16))
    valid = lax.broadcasted_iota(jnp.int32, (nch, 1), 0) < n_cmp
    for g in range(NSA_KV_HEADS):
        cols = slice(g * CMP_HIDDEN, (g + 1) * CMP_HIDDEN)
        out = (jnp.dot(gelus[0][:, cols], w2_ref[0], preferred_element_type=F32)
               + jnp.dot(gelus[1][:, cols], w2_ref[1], preferred_element_type=F32) + b2_ref[...])
        out = _rope128(out, c_ref[...], s1_ref[...], s2_ref[...])
        o_ref[g] = jnp.where(valid, out, 0.0).astype(BF16)


def compress(raw, cw, tables_q, seq_len):
    w1bd, pe, b1, w2, b2 = cw
    m = raw.shape[0]
    nb = m // seq_len
    nch = seq_len // CMP_STRIDE
    n_cmp = nch - CMP_RATIO + 1
    full = lambda shape: pl.BlockSpec(shape, lambda b: (0,) * len(shape))
    return pl.pallas_call(
        functools.partial(_compress_body, nch=nch, n_cmp=n_cmp),
        grid=(nb,),
        in_specs=[pl.BlockSpec((seq_len, KV_W), lambda b: (b, 0)), pl.BlockSpec((seq_len, KV_W), lambda b: (b, 1)),
                  full(pe.shape), full(w1bd.shape), full(b1.shape), full(w2.shape), full(b2.shape),
                  full((nch, LANE)), full((nch, LANE)), full((nch, LANE))],
        out_specs=pl.BlockSpec((None, NSA_KV_HEADS, nch, LANE), lambda b: (b, 0, 0, 0)),
        out_shape=jax.ShapeDtypeStruct((nb, NSA_KV_HEADS, nch, LANE), BF16),
        compiler_params=pltpu.CompilerParams(
            dimension_semantics=("arbitrary",), vmem_limit_bytes=VMEM_LIMIT),
        name="compress_kv",
    )(raw, raw, pe, w1bd, b1, w2, b2, *tables_q)


_NT = (((1,), (1,)), ((), ()))


def _masked_softmax2(s, mask):
    s = jnp.where(mask, s, -1e30)
    m = jnp.max(s, axis=-1, keepdims=True)
    e = jnp.where(mask, jnp.exp2(s - m), 0.0)
    return e / jnp.maximum(jnp.sum(e, axis=-1, keepdims=True), 1e-30)


N_FORCED = 3


def _select_blocks(imp, valid, forced, block_idx):
    score = jnp.where(valid & jnp.logical_not(forced), imp, -FORCE_SCORE)

    def pick_one(_, carry):
        score, chosen = carry
        hit = block_idx == jnp.argmax(score, axis=1, keepdims=True)
        return jnp.where(hit, -jnp.inf, score), jnp.where(hit, 1.0, chosen)

    chosen = jnp.broadcast_to(jnp.where(forced, 1.0, 0.0), imp.shape)
    _, chosen = lax.fori_loop(0, SEL_TOPK - N_FORCED, pick_one, (score, chosen), unroll=True)
    return jnp.where(valid, chosen, 0.0)


def _nsa_prompt_body(qz_ref, gate_ref, kvc_ref, smap_ref, kvs_ref, kvw_ref, o_ref, *, tk):
    QB, H = Q_BLOCK, NSA_HPG
    q0 = pl.program_id(2) * QB
    q4 = jnp.concatenate([qz_ref[:, h * LANE:(h + 1) * LANE] for h in range(H)], axis=0)
    t_col = q0 + (lax.broadcasted_iota(jnp.int32, (H * QB, 1), 0) & (QB - 1))

    kvc = kvc_ref[...]
    ncp = kvc.shape[0]
    s_c = lax.dot_general(q4, kvc, _NT, preferred_element_type=F32)
    c_end = lax.broadcasted_iota(jnp.int32, (1, ncp), 1) * CMP_STRIDE + (CMP_BLOCK - 1)
    p_c = _masked_softmax2(s_c, c_end <= t_col)
    o_c = jnp.dot(p_c.astype(BF16), kvc, preferred_element_type=F32)

    psum = p_c[0:QB]
    for h in range(1, H):
        psum = psum + p_c[h * QB:(h + 1) * QB]
    hi = psum.astype(BF16)
    lo = (psum - hi.astype(F32)).astype(BF16)
    smap = smap_ref[...]
    imp = jnp.dot(hi, smap, preferred_element_type=F32) + jnp.dot(lo, smap, preferred_element_type=F32)

    jf = lax.broadcasted_iota(jnp.int32, (QB, LANE), 1)
    t_tok = q0 + lax.broadcasted_iota(jnp.int32, (QB, 1), 0)
    cur = t_tok // SEL_BLOCK
    valid = jf * SEL_BLOCK <= t_tok
    forced = (jf == 0) | (jf == cur) | (jf == cur - 1)
    selm1 = (_select_blocks(imp, valid, forced, jf) - 1.0).astype(BF16)
    q_aug = jnp.concatenate([q4, jnp.concatenate([selm1] * H, axis=0)], axis=1)

    NCHAIN = 1
    RC = H * QB // NCHAIN

    def tile_step(kt, carry, causal=False):
        hk = tk // 2
        start = pl.multiple_of(kt * tk, tk)
        tiles = [kvs_ref[pl.ds(start + j * hk, hk), :] for j in range(2)]
        out = []
        for c in range(NCHAIN):
            m, l, acc = carry[c]
            qc = q_aug[c * RC:(c + 1) * RC]
            ss = [lax.dot_general(qc, tl, _NT, preferred_element_type=F32) for tl in tiles]
            if causal:
                for j in range(2):
                    kpos = kt * tk + j * hk + lax.broadcasted_iota(jnp.int32, (1, hk), 1)
                    ss[j] = jnp.where(kpos <= t_col[c * RC:(c + 1) * RC], ss[j], -1e30)
            m_new = jnp.maximum(m, jnp.maximum(jnp.max(ss[0], axis=1, keepdims=True),
                                               jnp.max(ss[1], axis=1, keepdims=True)))
            alpha = jnp.exp2(m - m_new)
            ps = [jnp.exp2(sj - m_new) for sj in ss]
            l = alpha * l + jnp.sum(ps[0], axis=1, keepdims=True) + jnp.sum(ps[1], axis=1, keepdims=True)
            pv = (jnp.dot(ps[0].astype(BF16), tiles[0][:, 0:LANE], preferred_element_type=F32)
                  + jnp.dot(ps[1].astype(BF16), tiles[1][:, 0:LANE], preferred_element_type=F32))
            out.append((m_new, l, alpha * acc + pv))
        return tuple(out)

    n_full = q0 // tk
    init = tuple((jnp.full((RC, 1), -1e30, F32), jnp.zeros((RC, 1), F32), jnp.zeros((RC, LANE), F32))
                 for _ in range(NCHAIN))
    UNROLL = 4

    def multi_step(i, c):
        for u in range(UNROLL):
            c = tile_step(UNROLL * i + u, c)
        return c

    n_multi = n_full // UNROLL
    carry = lax.fori_loop(0, n_multi, multi_step, init)
    carry = lax.fori_loop(UNROLL * n_multi, n_full, tile_step, carry)
    carry = tile_step(n_full, carry, causal=True)
    acc_s = jnp.concatenate([c[2] for c in carry], axis=0)
    o_s = acc_s[:, 0:LANE] / jnp.concatenate([c[1] for c in carry], axis=0)

    wlen = WINDOW + QB
    w0 = pl.multiple_of(jnp.maximum(q0 - WINDOW, 0), QB)
    wt = kvw_ref[pl.ds(w0, wlen), :]
    s_w = lax.dot_general(q4, wt, _NT, preferred_element_type=F32)
    dist = t_col - (w0 + lax.broadcasted_iota(jnp.int32, (1, wlen), 1))
    p_w = _masked_softmax2(s_w, (dist >= 0) & (dist < WINDOW))
    o_w = jnp.dot(p_w.astype(BF16), wt, preferred_element_type=F32)

    comb = []
    for h in range(H):
        r = slice(h * QB, (h + 1) * QB)
        comb.append(gate_ref[:, 3 * h:3 * h + 1] * o_c[r] + gate_ref[:, 3 * h + 1:3 * h + 2] * o_s[r]
                    + gate_ref[:, 3 * h + 2:3 * h + 3] * o_w[r])
    low = lax.broadcasted_iota(jnp.int32, (1, LANE), 1) < HEAD_DIM
    for p in range(H // 2):
        o_ref[:, p * LANE:(p + 1) * LANE] = jnp.where(low, pltpu.roll(comb[2 * p], HEAD_DIM, 1), comb[2 * p + 1])


def nsa_prompt(qz, gate, kvc, kvs, kvw, nb, seq_len, tk=512):
    m = qz.shape[0]
    G, QB = NSA_KV_HEADS, Q_BLOCK
    nqb = seq_len // QB
    ncp = kvc.shape[2]
    assert seq_len % tk == 0 and seq_len >= WINDOW + QB and seq_len // SEL_BLOCK <= LANE
    import numpy as np
    c0 = np.arange(ncp)[:, None] * CMP_STRIDE
    s0 = np.arange(LANE)[None, :] * SEL_BLOCK
    n_cmp = ncp - CMP_RATIO + 1
    ov = (c0 < s0 + SEL_BLOCK) & (c0 + CMP_BLOCK > s0) & (np.arange(ncp)[:, None] < n_cmp)
    smap = jnp.asarray(ov, BF16)
    return pl.pallas_call(
        functools.partial(_nsa_prompt_body, tk=tk),
        grid=(nb, G, nqb),
        in_specs=[
            pl.BlockSpec((QB, NSA_HPG * LANE), lambda b, g, q: (b * nqb + q, g)),
            pl.BlockSpec((None, QB, LANE), lambda b, g, q: (g, b * nqb + q, 0)),
            pl.BlockSpec((None, None, ncp, LANE), lambda b, g, q: (b, g, 0, 0)),
            pl.BlockSpec((ncp, LANE), lambda b, g, q: (0, 0)),
            pl.BlockSpec((None, seq_len, 2 * LANE), lambda b, g, q: (g, b, 0)),
            pl.BlockSpec((None, seq_len, LANE), lambda b, g, q: (g, b, 0)),
        ],
        out_specs=pl.BlockSpec((QB, NSA_HPG * HEAD_DIM), lambda b, g, q: (b * nqb + q, g)),
        out_shape=jax.ShapeDtypeStruct((m, NSA_WIDTH), F32),
        compiler_params=pltpu.CompilerParams(
            dimension_semantics=("arbitrary", "arbitrary", "arbitrary"), vmem_limit_bytes=VMEM_LIMIT),
        name="nsa_prompt",
    )(qz, gate, kvc, smap, kvs, kvw)


def _shift_rows(x, prev8, d):
    rolled = pltpu.roll(x, d, 0)
    row8 = lax.broadcasted_iota(jnp.int32, (8, 1), 0)
    first = jnp.where(row8 < d, pltpu.roll(prev8, d, 0), rolled[0:8])
    return jnp.concatenate([first, rolled[8:]], axis=0)


def _rglru_body(rg_ref, buf_ref, h0_ref, cw_ref, cb_ref, wg_ref, bg_ref, lam_ref,
                o_ref, hout_ref, bufout_ref, tail_ref, h_ref):
    tm = rg_ref.shape[0]
    W = RG_WIDTH

    @pl.when(pl.program_id(1) == 0)
    def _():
        tail_ref[...] = buf_ref[...]
        h_ref[...] = h0_ref[...]

    x = rg_ref[:, 0:W]
    gr = rg_ref[:, W:2 * W]
    prev8 = tail_ref[...]
    xc = cb_ref[...] + cw_ref[CONV_W - 1:CONV_W, :] * x
    for d in range(1, CONV_W):
        xc = xc + cw_ref[CONV_W - 1 - d:CONV_W - d, :] * _shift_rows(x, prev8, d)
    gates = jnp.dot(xc.astype(BF16), wg_ref[...], preferred_element_type=F32) + bg_ref[...]
    r = jax.nn.sigmoid(gates[:, 0:W])
    i = jax.nn.sigmoid(gates[:, W:2 * W])
    log_a = -RG_C * r * jax.nn.softplus(-lam_ref[...])
    a = jnp.exp(log_a)
    th = jnp.tanh(log_a)
    b = jnp.sqrt(-2.0 * th / (1.0 - th)) * (i * xc)
    row = lax.broadcasted_iota(jnp.int32, (tm, 1), 0)
    d = 1
    while d < tm:
        keep = row >= d
        a_sh = jnp.where(keep, pltpu.roll(a, d, 0), 1.0)
        b_sh = jnp.where(keep, pltpu.roll(b, d, 0), 0.0)
        b = a * b_sh + b
        a = a * a_sh
        d *= 2
    h = a * h_ref[7:8, :] + b
    o_ref[...] = h * jax.nn.gelu(gr)
    h_ref[...] = h[tm - 8:tm]
    tail_ref[...] = x[tm - 8:tm]
    hout_ref[...] = h[tm - 8:tm]
    bufout_ref[...] = x[tm - 8:tm]


def rglru_prompt(rg, conv_buf, h0, conv_w, conv_b, wa, ba, wx, bx, lam, seq_len, tm=256):
    m = rg.shape[0]
    nb = m // seq_len
    W = RG_WIDTH
    tm = _row_tile(seq_len, tm)
    nt = seq_len // tm
    bd = jax.scipy.linalg.block_diag
    wg = jnp.concatenate([bd(*[wa[n] for n in range(RG_BLOCKS)]), bd(*[wx[n] for n in range(RG_BLOCKS)])], axis=1).astype(BF16)
    bg = jnp.concatenate([ba, bx]).reshape(1, 2 * W)
    buf8 = jnp.pad(conv_buf, ((0, 0), (8 - (CONV_W - 1), 0), (0, 0)))
    h08 = jnp.broadcast_to(h0[:, None, :], (nb, 8, W))
    const = lambda shape: pl.BlockSpec(shape, lambda b, t: (0,) * len(shape))
    per_b = pl.BlockSpec((None, 8, W), lambda b, t: (b, 0, 0))
    out, h8, nbuf8 = pl.pallas_call(
        _rglru_body,
        grid=(nb, nt),
        in_specs=[pl.BlockSpec((tm, 2 * W), lambda b, t: (b * nt + t, 0)), per_b, per_b,
                  const((CONV_W, W)), const((1, W)), const((W, 2 * W)), const((1, 2 * W)), const((1, W))],
        out_specs=[pl.BlockSpec((tm, W), lambda b, t: (b * nt + t, 0)), per_b, per_b],
        out_shape=[jax.ShapeDtypeStruct((m, W), F32), jax.ShapeDtypeStruct((nb, 8, W), F32),
                   jax.ShapeDtypeStruct((nb, 8, W), F32)],
        scratch_shapes=[pltpu.VMEM((8, W), F32), pltpu.VMEM((8, W), F32)],
        compiler_params=pltpu.CompilerParams(dimension_semantics=("arbitrary", "arbitrary")),
        name="rglru_prompt",
    )(rg, buf8, h08, conv_w, conv_b.reshape(1, W), wg, bg, lam.reshape(1, W))
    return out, h8[:, 7], nbuf8[:, 8 - (CONV_W - 1):]


def _split_dot(x, w_bf16, pieces):
    acc = None
    for _ in range(pieces):
        xb = x.astype(BF16)
        part = jnp.dot(xb, w_bf16, preferred_element_type=F32)
        acc = part if acc is None else acc + part
        x = x - xb.astype(F32)
    return acc


HG_SAFE_DECAY = 75.0


def _hgrn_body(hg_ref, s0_ref, lb_ref, gain_ref, tri_ref, tribd_ref, ones_ref, o_ref, sout_ref, st_ref, *, chunk):
    tm = hg_ref.shape[0]
    W, H = HG_WIDTH, HG_HEADS
    C = chunk
    nck = tm // C

    @pl.when(pl.program_id(1) == 0)
    def _():
        st_ref[...] = s0_ref[...]

    lb = lb_ref[...]
    ones_blk = ones_ref[...]
    same_head = ones_blk > 0

    def finish(o, rows):
        ms = _split_dot(o * o, ones_blk, 2) * (1.0 / HG_DV)
        o = o * lax.rsqrt(ms + EPS) * gain_ref[...]
        o_ref[rows, :] = o * jax.nn.silu(hg_ref[rows, 3 * W:4 * W])

    q = jax.nn.silu(hg_ref[:, 0:W])
    f = lb + (1.0 - lb) * jax.nn.sigmoid(hg_ref[:, W:2 * W])
    v = hg_ref[:, 2 * W:3 * W]
    k = 1.0 - f
    bcum = _split_dot_left(tribd_ref[...], jnp.log(f))
    b_last = [bcum[(c + 1) * C - 1:(c + 1) * C, :] for c in range(nck)]
    total = b_last[0]
    for c in range(1, nck):
        total = jnp.minimum(total, b_last[c])
    safe = jnp.min(total) >= -HG_SAFE_DECAY

    @pl.when(safe)
    def _():
        qt = q * jnp.exp(bcum)
        kt = (k * jnp.exp(-bcum)).astype(BF16)
        vb = v.astype(BF16)
        lane_head = lax.broadcasted_iota(jnp.int32, (1, W), 1) // HG_DV
        q_exp = jnp.concatenate([jnp.where(lane_head == h, qt, 0.0) for h in range(H)], axis=0).astype(BF16)
        a = lax.dot_general(q_exp, kt, _NT, preferred_element_type=F32)
        t_row = lax.broadcasted_iota(jnp.int32, (H * tm, 1), 0) & (tm - 1)
        s_col = lax.broadcasted_iota(jnp.int32, (1, tm), 1)
        pair = (s_col <= t_row) & ((s_col // C) == (t_row // C))
        o_exp = jnp.dot(jnp.where(pair, a, 0.0).astype(BF16), vb, preferred_element_type=F32)
        o = jnp.where(lane_head == 0, o_exp[0:tm], 0.0)
        for h in range(1, H):
            o = o + jnp.where(lane_head == h, o_exp[h * tm:(h + 1) * tm], 0.0)
        st = st_ref[...]
        inter = []
        for c in range(nck):
            r = slice(c * C, (c + 1) * C)
            inter.append(lax.dot_general(qt[r].astype(BF16), st.astype(BF16), _NT, preferred_element_type=F32))
            k_hat = (k[r] * jnp.exp(b_last[c] - bcum[r])).astype(BF16)
            upd = lax.dot_general(vb[r], k_hat, (((0,), (0,)), ((), ())), preferred_element_type=F32)
            st = jnp.exp(b_last[c]) * st + jnp.where(same_head, upd, 0.0)
        st_ref[...] = st
        finish(o + jnp.concatenate(inter, axis=0), slice(None))

    @pl.when(jnp.logical_not(safe))
    def _():
        tri = tri_ref[...]
        rowc = lax.broadcasted_iota(jnp.int32, (C, 1), 0)

        def chunk_step(c, _):
            r0 = pl.multiple_of(c * C, C)
            rows = pl.ds(r0, C)
            qc = jax.nn.silu(hg_ref[rows, 0:W])
            fc = lb + (1.0 - lb) * jax.nn.sigmoid(hg_ref[rows, W:2 * W])
            vc = hg_ref[rows, 2 * W:3 * W]
            kc = 1.0 - fc
            bc = _split_dot_left(tri, jnp.log(fc))
            st = st_ref[...]
            o = lax.dot_general((qc * jnp.exp(bc)).astype(BF16), st.astype(BF16), _NT, preferred_element_type=F32)

            def offset_step(dlt, carry):
                o, b_sh, k_sh, v_sh = carry
                p = jnp.where(rowc >= dlt, qc * k_sh * jnp.exp(bc - b_sh), 0.0)
                o = o + _split_dot(p, ones_blk, 2) * v_sh
                return o, pltpu.roll(b_sh, 1, 0), pltpu.roll(k_sh, 1, 0), pltpu.roll(v_sh, 1, 0)

            o, _, _, _ = lax.fori_loop(0, C, offset_step, (o, bc, kc, vc))
            bl = bc[C - 1:C, :]
            k_hat = (kc * jnp.exp(bl - bc)).astype(BF16)
            upd = lax.dot_general(vc.astype(BF16), k_hat, (((0,), (0,)), ((), ())), preferred_element_type=F32)
            st_ref[...] = jnp.exp(bl) * st + jnp.where(same_head, upd, 0.0)
            finish(o, rows)
            return 0

        lax.fori_loop(0, nck, chunk_step, 0)

    sout_ref[...] = st_ref[...]


def _split_dot_left(w_bf16, x):
    acc = None
    for _ in range(3):
        xb = x.astype(BF16)
        part = jnp.dot(w_bf16, xb, preferred_element_type=F32)
        acc = part if acc is None else acc + part
        x = x - xb.astype(F32)
    return acc


def hgrn_prompt(hg, s0, lb, gain, seq_len, tm=256, chunk=HG_CHUNK):
    import numpy as np
    m = hg.shape[0]
    nb = m // seq_len
    W, H = HG_WIDTH, HG_HEADS
    tm = _row_tile(seq_len, tm)
    nt = seq_len // tm
    head = np.arange(W) // HG_DV
    ones_blk = jnp.asarray(head[:, None] == head[None, :], BF16)
    tri = jnp.asarray(np.tril(np.ones((chunk, chunk))), BF16)
    rt = np.arange(tm)
    tri_bd = jnp.asarray((rt[None, :] <= rt[:, None]) & (rt[None, :] // chunk == rt[:, None] // chunk), BF16)
    s0_t = jnp.einsum('bhkv,hg->bhvgk', s0, jnp.eye(H, dtype=s0.dtype)).reshape(nb, W, W)
    const = lambda shape: pl.BlockSpec(shape, lambda b, t: (0,) * len(shape))
    per_b = pl.BlockSpec((None, W, W), lambda b, t: (b, 0, 0))
    out, s_t = pl.pallas_call(
        functools.partial(_hgrn_body, chunk=chunk),
        grid=(nb, nt),
        in_specs=[pl.BlockSpec((tm, 4 * W), lambda b, t: (b * nt + t, 0)), per_b,
                  const((1, W)), const((1, W)), const((chunk, chunk)), const((tm, tm)), const((W, W))],
        out_specs=[pl.BlockSpec((tm, W), lambda b, t: (b * nt + t, 0)), per_b],
        out_shape=[jax.ShapeDtypeStruct((m, W), F32), jax.ShapeDtypeStruct((nb, W, W), F32)],
        scratch_shapes=[pltpu.VMEM((W, W), F32)],
        compiler_params=pltpu.CompilerParams(dimension_semantics=("arbitrary", "arbitrary")),
        name="hgrn_prompt",
    )(hg, s0_t, lb.reshape(1, W), gain.reshape(1, W), tri, tri_bd, ones_blk)
    s5 = s_t.reshape(nb, H, HG_DV, H, HG_DK)
    s_fin = jnp.stack([s5[:, h, :, h, :] for h in range(H)], axis=1).transpose(0, 1, 3, 2)
    return out, s_fin


def _nsa_sample_body(pt_ref, *refs, n_pages, past_len, n_win, spb):
    del pt_ref
    np2 = 2 * n_pages
    cmp_all, sel_all = refs[0:spb * np2], refs[spb * np2:2 * spb * np2]
    (win_ref, selnew_ref, winnew_ref, qlo_ref, qq_ref, gate_ref, pe_ref, w1_ref, b1_ref, w2_ref, b2_ref,
     c_ref, s1_ref, s2_ref, smap_ref, expand_ref, wcol_ref, o_ref, winout_ref, raw_ref) = refs[2 * spb * np2:]
    R = NSA_KV_HEADS * NSA_HPG
    parts = [_nsa_sample_cmp(cmp_all[j * np2:(j + 1) * np2], qlo_ref.at[j], pe_ref, w1_ref, b1_ref, w2_ref, b2_ref,
                             c_ref, s1_ref, s2_ref, smap_ref, raw_ref.at[j], n_pages=n_pages, past_len=past_len)
             for j in range(spb)]
    lane = lax.broadcasted_iota(jnp.int32, (1, LANE), 1)
    cur = past_len // SEL_BLOCK
    valid = lane * SEL_BLOCK <= past_len
    forced = (lane == 0) | (lane == cur) | (lane == cur - 1)
    imp = jnp.concatenate([pt_[1] for pt_ in parts], axis=0)
    selm1 = (_select_blocks(imp, valid, forced, jnp.broadcast_to(lane, (spb * R, LANE))) - 1.0).astype(BF16)
    bias = jnp.dot(selm1, expand_ref[...], preferred_element_type=F32)
    each = lambda ref: [ref.at[j] for j in range(spb)]
    _nsa_sample_attend([sel_all[j * np2:(j + 1) * np2] for j in range(spb)], each(win_ref), each(selnew_ref),
                       each(winnew_ref), each(qq_ref), each(gate_ref), each(wcol_ref), each(o_ref), each(winout_ref),
                       jnp.concatenate([pt_[0] for pt_ in parts], axis=0), bias,
                       n_pages=n_pages, past_len=past_len, n_win=n_win)


def _nsa_sample_cmp(cmp_refs, qlo_ref, pe_ref, w1_ref, b1_ref, w2_ref, b2_ref, c_ref, s1_ref, s2_ref, smap_ref,
                    raw_ref, *, n_pages, past_len):
    G, H, CH = NSA_KV_HEADS, NSA_HPG, CMP_HIDDEN
    R = G * H
    nch = n_pages * (PAGE_SIZE // CMP_STRIDE)
    n_cmp = nch - CMP_RATIO + 1
    _TT = (((1,), (1,)), ((), ()))
    for p in range(n_pages):
        for kv in range(2):
            raw_ref[kv, p * PAGE_SIZE:(p + 1) * PAGE_SIZE, :] = cmp_refs[2 * p + kv][...].T
    t = past_len
    row = lax.broadcasted_iota(jnp.int32, (R, 1), 0)
    grp0 = row < H
    lane = lax.broadcasted_iota(jnp.int32, (1, LANE), 1)

    gelus = []
    for kv in range(2):
        acc = [jnp.zeros((nch, G * CH), F32) for _ in range(CMP_RATIO)]
        for sp in range(CMP_STRIDE // 2):
            xs = jnp.concatenate([raw_ref[kv, pl.ds(s, nch, stride=CMP_STRIDE), :] for s in (2 * sp, 2 * sp + 1)], axis=1)
            for r in range(CMP_RATIO):
                acc[r] = acc[r] + jnp.dot((xs + pe_ref[kv, r, sp:sp + 1, :]).astype(BF16), w1_ref[kv, r, sp],
                                          preferred_element_type=F32)
        h = b1_ref[kv] + acc[0] + pltpu.roll(acc[1], nch - 1, 0)
        gelus.append(jax.nn.gelu(h).astype(BF16))
    valid_c = lax.broadcasted_iota(jnp.int32, (nch, 1), 0) < n_cmp
    qlo = qlo_ref[...].astype(BF16)
    c_end = lax.broadcasted_iota(jnp.int32, (1, nch), 1) * CMP_STRIDE + (CMP_BLOCK - 1)
    smap = smap_ref[...]
    o_c, imp = [], []
    for g in range(G):
        cols = slice(g * CH, (g + 1) * CH)
        kvc = (jnp.dot(gelus[0][:, cols], w2_ref[0], preferred_element_type=F32)
               + jnp.dot(gelus[1][:, cols], w2_ref[1], preferred_element_type=F32) + b2_ref[...])
        kvc = jnp.where(valid_c, _rope128(kvc, c_ref[...], s1_ref[...], s2_ref[...]), 0.0).astype(BF16)
        s_c = lax.dot_general(qlo, kvc, _NT, preferred_element_type=F32)
        p_c = _masked_softmax2(s_c, c_end <= t)
        o_c.append(jnp.dot(p_c.astype(BF16), kvc, preferred_element_type=F32))
        mine = grp0 if g == 0 else jnp.logical_not(grp0)
        psum = jnp.sum(jnp.where(mine, p_c, 0.0), axis=0, keepdims=True)
        imp.append(_split_dot(jnp.broadcast_to(psum, (R, nch)), smap, 2))
    o_c = jnp.where(grp0, pltpu.roll(o_c[0], HEAD_DIM, 1), o_c[1])
    return o_c, jnp.where(grp0, imp[0], imp[1])


def _nsa_sample_attend(sel_refs, win_refs, selnew_refs, winnew_refs, qq_refs, gate_refs, wcol_refs, o_refs,
                       winout_refs, o_c, bias, *, n_pages, past_len, n_win):
    t = past_len
    S = len(qq_refs)
    R = NSA_KV_HEADS * NSA_HPG
    _TT = (((1,), (1,)), ((), ()))
    stack = lambda xs: jnp.concatenate(xs, axis=0)
    rows = lambda x, j: x[j * R:(j + 1) * R]
    qq32 = [r[...] for r in qq_refs]
    qq = [q.astype(BF16) for q in qq32]

    s_s = stack([jnp.concatenate([jnp.dot(qq[j], sel_refs[j][2 * p][...].astype(BF16), preferred_element_type=F32)
                                  for p in range(n_pages)], axis=1) for j in range(S)]) + bias
    s_new = stack([jnp.sum(qq32[j] * selnew_refs[j][:, 0:LANE], axis=1, keepdims=True) for j in range(S)])
    m = jnp.maximum(jnp.max(s_s, axis=1, keepdims=True), s_new)
    p = jnp.exp2(s_s - m)
    p_new = jnp.exp2(s_new - m)
    denom = jnp.sum(p, axis=1, keepdims=True) + p_new
    pb = p.astype(BF16)
    o_s = []
    for j in range(S):
        acc = rows(p_new, j) * selnew_refs[j][:, LANE:2 * LANE]
        for pg in range(n_pages):
            acc = acc + lax.dot_general(rows(pb, j)[:, pg * PAGE_SIZE:(pg + 1) * PAGE_SIZE],
                                        sel_refs[j][2 * pg + 1][...].astype(BF16), _TT, preferred_element_type=F32)
        o_s.append(acc)
    o_s = stack(o_s) / denom

    wins = [r[...] for r in win_refs]
    col = lax.broadcasted_iota(jnp.int32, (1, n_win), 1)
    w_ok = (n_win - col < WINDOW) & (t - n_win + col >= 0)
    s_w = stack([jnp.dot(qq[j], wins[j][0:LANE, :].astype(BF16), preferred_element_type=F32) for j in range(S)])
    s_w = jnp.where(w_ok, s_w, -1e30)
    s_wn = stack([jnp.sum(qq32[j] * winnew_refs[j][:, 0:LANE], axis=1, keepdims=True) for j in range(S)])
    m = jnp.maximum(jnp.max(s_w, axis=1, keepdims=True), s_wn)
    p = jnp.where(w_ok, jnp.exp2(s_w - m), 0.0)
    p_new = jnp.exp2(s_wn - m)
    denom = jnp.sum(p, axis=1, keepdims=True) + p_new
    pb = p.astype(BF16)
    o_w = stack([lax.dot_general(rows(pb, j), wins[j][LANE:2 * LANE, :].astype(BF16), _TT, preferred_element_type=F32)
                 + rows(p_new, j) * winnew_refs[j][:, LANE:2 * LANE] for j in range(S)]) / denom

    gate = stack([r[...] for r in gate_refs])
    o = gate[:, 0:1] * o_c + gate[:, 1:2] * o_s + gate[:, 2:3] * o_w
    for j in range(S):
        o_refs[j][...] = rows(o, j)
        winout_refs[j][...] = jnp.where(col == n_win - 1, wcol_refs[j][...], pltpu.roll(wins[j], n_win - 1, 1))


def nsa_sample(l, page_table, cache_cmp, cache_sel, cache_win, sel_new, win_new, qz, gate, cw):
    import numpy as np
    n, n_pages = page_table.shape
    past_len = n_pages * PAGE_SIZE
    n_win = cache_win.shape[2]
    depth, n_pool = cache_cmp.shape[:2]
    G, H = NSA_KV_HEADS, NSA_HPG
    R = G * H
    nch = past_len // CMP_STRIDE
    assert past_len // SEL_BLOCK + 1 <= LANE and n_win % 8 == 0
    w1bd, pe, b1, w2, b2 = cw
    w1p = w1bd.reshape(2, CMP_RATIO, CMP_STRIDE // 2, 2 * LANE, G * CMP_HIDDEN)
    pep = pe.reshape(2, CMP_RATIO, CMP_STRIDE // 2, 2 * LANE)
    c_end = jnp.arange(nch, dtype=jnp.int32) * CMP_STRIDE + CMP_BLOCK - 1
    tabs = rope_tables(c_end)[:3]
    c0 = np.arange(nch)[:, None] * CMP_STRIDE
    s0 = np.arange(LANE)[None, :] * SEL_BLOCK
    ov = (c0 < s0 + SEL_BLOCK) & (c0 + CMP_BLOCK > s0) & (np.arange(nch)[:, None] < nch - CMP_RATIO + 1)
    smap = jnp.asarray(ov, BF16)
    expand = jnp.asarray(np.where(np.arange(past_len)[None, :] // SEL_BLOCK == np.arange(LANE)[:, None], MASK_BIG, 0.0), BF16)
    qlo = qz.reshape(n, R, LANE).astype(F32)
    qq = jnp.concatenate([qlo[:, :H], jnp.roll(qlo[:, H:], HEAD_DIM, axis=-1)], axis=1)
    g3 = jnp.pad(gate[:, :, :3 * H].reshape(G, n, H, 3).transpose(1, 0, 2, 3).reshape(n, R, 3), ((0, 0), (0, 0), (0, LANE - 3)))
    cmp4 = cache_cmp.reshape(depth, n_pool, PAGE_SIZE, 2 * KV_W).transpose(0, 1, 3, 2)
    sel4 = cache_sel.reshape(depth, n_pool, PAGE_SIZE, 2 * KV_W).transpose(0, 1, 3, 2)
    win4 = cache_win.reshape(depth, n, n_win, 2 * KV_W).transpose(0, 1, 3, 2)

    spb = 2 if n % 2 == 0 else 1

    def page_spec(j, p, half):
        return pl.BlockSpec((None, None, LANE, PAGE_SIZE), lambda b, pt: (l, pt[b * spb + j, p], half, 0))

    page_specs = [page_spec(j, p, half) for j in range(spb) for p in range(n_pages) for half in range(2)]
    full = lambda shape: pl.BlockSpec(shape, lambda b, pt: (0,) * len(shape))
    per_seq = lambda shape: pl.BlockSpec((spb,) + shape, lambda b, pt: (b,) + (0,) * len(shape))
    grid_spec = pltpu.PrefetchScalarGridSpec(
        num_scalar_prefetch=1,
        grid=(n // spb,),
        in_specs=page_specs + page_specs + [
            pl.BlockSpec((None, spb, 2 * KV_W, n_win), lambda b, pt: (l, b, 0, 0)),
            per_seq((1, 2 * KV_W)), per_seq((1, 2 * KV_W)), per_seq((R, LANE)), per_seq((R, LANE)), per_seq((R, LANE)),
            full(pep.shape), full(w1p.shape), full(b1.shape), full(w2.shape), full(b2.shape),
            full((nch, LANE)), full((nch, LANE)), full((nch, LANE)), full((nch, LANE)), full((LANE, past_len)),
            per_seq((2 * KV_W, 1))],
        out_specs=[per_seq((R, LANE)), per_seq((2 * KV_W, n_win))],
        scratch_shapes=[pltpu.VMEM((spb, 2, past_len, LANE), F32)],
    )
    o8, win_out = pl.pallas_call(
        functools.partial(_nsa_sample_body, n_pages=n_pages, past_len=past_len, n_win=n_win, spb=spb),
        grid_spec=grid_spec,
        out_shape=[jax.ShapeDtypeStruct((n, R, LANE), F32), jax.ShapeDtypeStruct((n, 2 * KV_W, n_win), F32)],
        compiler_params=pltpu.CompilerParams(dimension_semantics=("arbitrary",), vmem_limit_bytes=VMEM_LIMIT),
        name="nsa_sample",
    )(page_table, *([cmp4] * (2 * n_pages * spb)), *([sel4] * (2 * n_pages * spb)), win4,
      sel_new.reshape(n, 1, 2 * KV_W), win_new.reshape(n, 1, 2 * KV_W), qlo, qq, g3,
      pep, w1p, b1, w2, b2, *tabs, smap, expand, win_new.reshape(n, 2 * KV_W, 1))
    o5 = o8.reshape(n, G, H, G, HEAD_DIM)
    o = jnp.stack([o5[:, g, :, g, :] for g in range(G)], axis=1).reshape(n, NSA_WIDTH)
    return o, win_out.transpose(0, 2, 1)


def _rglru_sample_body(rg_ref, b0_ref, b1_ref, b2_ref, h0_ref, cw_ref, cb_ref, wg_ref, bg_ref, lam_ref, o_ref, h_ref):
    W = RG_WIDTH
    x = rg_ref[:, 0:W]
    xc = (cb_ref[...] + cw_ref[0:1, :] * b0_ref[...] + cw_ref[1:2, :] * b1_ref[...] + cw_ref[2:3, :] * b2_ref[...]
          + cw_ref[3:4, :] * x)
    gates = jnp.dot(xc.astype(BF16), wg_ref[...], preferred_element_type=F32) + bg_ref[...]
    r = jax.nn.sigmoid(gates[:, 0:W])
    i = jax.nn.sigmoid(gates[:, W:2 * W])
    log_a = -RG_C * r * jax.nn.softplus(-lam_ref[...])
    th = jnp.tanh(log_a)
    h = jnp.exp(log_a) * h0_ref[...] + jnp.sqrt(-2.0 * th / (1.0 - th)) * (i * xc)
    h_ref[...] = h
    o_ref[...] = h * jax.nn.gelu(rg_ref[:, W:2 * W])


def rglru_sample(rg, conv_buf, h0, conv_w, conv_b, wa, ba, wx, bx, lam):
    assert CONV_W == 4
    n = rg.shape[0]
    W = RG_WIDTH
    bd = jax.scipy.linalg.block_diag
    wg = jnp.concatenate([bd(*[wa[k] for k in range(RG_BLOCKS)]), bd(*[wx[k] for k in range(RG_BLOCKS)])], axis=1).astype(BF16)
    bg = jnp.concatenate([ba, bx]).reshape(1, 2 * W)
    out, h = pl.pallas_call(
        _rglru_sample_body,
        out_shape=[jax.ShapeDtypeStruct((n, W), F32), jax.ShapeDtypeStruct((n, W), F32)],
        name="rglru_sample",
    )(rg, conv_buf[:, 0], conv_buf[:, 1], conv_buf[:, 2], h0, conv_w, conv_b.reshape(1, W), wg, bg, lam.reshape(1, W))
    new_buf = jnp.concatenate([conv_buf[:, 1:], rg[:, None, 0:W]], axis=1)
    return out, h, new_buf


def _hgrn_sample_body(q_ref, f_ref, lb_ref, v_ref, g_ref, gain_ref, s0_ref, o_ref, s_ref):
    DK = HG_DK
    q = jax.nn.silu(q_ref[...])
    lb = lb_ref[...]
    f = lb + (1.0 - lb) * jax.nn.sigmoid(f_ref[...])
    k = 1.0 - f
    outs = []
    for h in range(HG_HEADS):
        r = slice(h * DK, (h + 1) * DK)
        s_new = f[r] * s0_ref[h] + k[r] * v_ref[h:h + 1, :]
        s_ref[h] = s_new
        outs.append(jnp.sum(q[r] * s_new, axis=0, keepdims=True))
    o = jnp.concatenate(outs, axis=0)
    o = o * lax.rsqrt(jnp.mean(o * o, axis=-1, keepdims=True) + EPS) * gain_ref[...]
    o_ref[...] = o * jax.nn.silu(g_ref[...])


def hgrn_sample(hg, s0, lb, gain):
    n = hg.shape[0]
    H, DK, DV, W = HG_HEADS, HG_DK, HG_DV, HG_WIDTH
    col = lambda a: a.reshape(n, W, 1)
    per_seq = lambda shape: pl.BlockSpec((None,) + shape, lambda b: (b,) + (0,) * len(shape))
    full = lambda shape: pl.BlockSpec(shape, lambda b: (0,) * len(shape))
    o, s_new = pl.pallas_call(
        _hgrn_sample_body,
        grid=(n,),
        in_specs=[per_seq((W, 1)), per_seq((W, 1)), full((W, 1)), per_seq((H, DV)), per_seq((H, DV)), full((H, DV)),
                  per_seq((H, DK, DV))],
        out_specs=[per_seq((H, DV)), per_seq((H, DK, DV))],
        out_shape=[jax.ShapeDtypeStruct((n, H, DV), F32), jax.ShapeDtypeStruct((n, H, DK, DV), F32)],
        compiler_params=pltpu.CompilerParams(dimension_semantics=("arbitrary",)),
        name="hgrn_sample",
    )(col(hg[:, 0:W]), col(hg[:, W:2 * W]), lb.reshape(W, 1), hg[:, 2 * W:3 * W].reshape(n, H, DV),
      hg[:, 3 * W:4 * W].reshape(n, H, DV), gain.reshape(H, DV), s0)
    return o.reshape(n, W), s_new


def forward_layer_prompt(x, w, wb, pw):
    (norm_mix, w_in, w_out, norm_ffn, w_up, w_down, cmp_pos, cmp_w1, cmp_b1, cmp_w2, cmp_b2,
     rg_conv_w, rg_conv_b, rg_wa, rg_ba, rg_wx, rg_bx, rg_lambda, hg_lb, hg_gain) = w
    _, w_out_b, w_up_b, w_down_b = wb
    w_rel, cw, tables, tables_cmp = pw
    B, T = x.shape[:2]
    x2 = x.reshape(B * T, D_MODEL)
    qz, cmp, sel, win, kvs, kvw, gate, rg, hg = in_proj_prompt(x2, norm_mix, w_rel, tables, T)
    kvc = compress(cmp, cw, tables_cmp, T)
    o_nsa = nsa_prompt(qz, gate, kvc, kvs, kvw, B, T)
    kv_shape = (B, T, 2, NSA_KV_HEADS, HEAD_DIM)
    new_cmp, new_sel = cmp.reshape(kv_shape), sel.reshape(kv_shape)
    new_win = win.reshape(kv_shape)[:, T - min(WINDOW, T):]
    rg_buf = jnp.zeros((B, CONV_W - 1, RG_WIDTH), x.dtype)
    rg_h0 = jnp.zeros((B, RG_WIDTH), x.dtype)
    hg_s0 = jnp.zeros((B, HG_HEADS, HG_DK, HG_DV), x.dtype)
    o_rg, new_h, new_buf = rglru_prompt(rg, rg_buf, rg_h0, rg_conv_w, rg_conv_b, rg_wa, rg_ba, rg_wx, rg_bx,
                                        rg_lambda, T)
    o_hg, new_s = hgrn_prompt(hg, hg_s0, hg_lb, hg_gain, T)
    mix = jnp.concatenate([o_nsa, o_rg, o_hg], axis=-1)
    x2 = out_ffn(x2, mix, w_out_b, norm_ffn, w_up_b, w_down_b)
    return x2.reshape(B, T, D_MODEL), (new_cmp, new_sel, new_win, new_h, new_buf, new_s)


def forward_layer_sample(l, x, w, wb, pw, caches):
    (norm_mix, w_in, w_out, norm_ffn, w_up, w_down, cmp_pos, cmp_w1, cmp_b1, cmp_w2, cmp_b2,
     rg_conv_w, rg_conv_b, rg_wa, rg_ba, rg_wx, rg_bx, rg_lambda, hg_lb, hg_gain) = w
    _, w_out_b, w_up_b, w_down_b = wb
    w_rel, cw, tables_s = pw
    page_table, cache_cmp, cache_sel, cache_win, rg_h0, rg_buf, hg_s0 = caches
    n = x.shape[0]
    x2 = x.reshape(n, D_MODEL)
    qz, cmp_new, sel_new, win_new, _, _, gate, rg, hg = in_proj_prompt(x2, norm_mix, w_rel, tables_s, n)
    o_nsa, win_out = nsa_sample(l, page_table, cache_cmp, cache_sel, cache_win, sel_new, win_new, qz, gate, cw)
    o_rg, new_h, new_buf = rglru_sample(rg, rg_buf, rg_h0, rg_conv_w, rg_conv_b, rg_wa, rg_ba, rg_wx, rg_bx, rg_lambda)
    o_hg, new_s = hgrn_sample(hg, hg_s0, hg_lb, hg_gain)
    mix = jnp.concatenate([o_nsa, o_rg, o_hg], axis=-1)
    x2 = out_ffn(x2, mix, w_out_b, norm_ffn, w_up_b, w_down_b)
    kv_shape = (n, 1, 2, NSA_KV_HEADS, HEAD_DIM)
    new_win = win_out.reshape(n, win_out.shape[1], 2, NSA_KV_HEADS, HEAD_DIM)
    return x2.reshape(n, 1, D_MODEL), (cmp_new.reshape(kv_shape), sel_new.reshape(kv_shape), new_win, new_h, new_buf, new_s)


def kernel(x_prompt, x_sample, cache_nsa_cmp_kv, cache_nsa_sel_kv, cache_nsa_win_kv, state_rglru_h,
           state_rglru_conv, state_hgrn_s, page_table, norm_mix, w_in, w_out, norm_ffn, w_up, w_down,
           cmp_pos, cmp_w1, cmp_b1, cmp_w2, cmp_b2, rg_conv_w, rg_conv_b, rg_wa, rg_ba, rg_wx, rg_bx,
           rg_lambda, hg_lower_bounds, hg_gain, final_norm):
    lb = jnp.cumsum(jax.nn.softmax(hg_lower_bounds.astype(jnp.float32), axis=0), axis=0)
    lb = lb - lb[0]
    past_len = page_table.shape[1] * PAGE_SIZE
    n_dec = x_sample.shape[0]
    pos_p = jnp.arange(x_prompt.shape[1], dtype=jnp.int32)
    w_out_b = w_out.astype(BF16)
    w_up_b = w_up.astype(BF16)
    w_down_b = w_down.astype(BF16)
    w_rel = relayout_w_in(w_in)
    cws = compress_weights(cmp_pos, cmp_w1, cmp_b1, cmp_w2, cmp_b2)
    T = x_prompt.shape[1]
    tables = rope_tables(pos_p)
    c_end_p = jnp.arange(T // CMP_STRIDE, dtype=jnp.int32) * CMP_STRIDE + CMP_BLOCK - 1
    tables_cmp = rope_tables(c_end_p)[:3]
    assert x_sample.shape[1] == 1
    tables_s = rope_tables(jnp.full((n_dec,), past_len, jnp.int32))
    xp, xs = x_prompt, x_sample
    st_p, st_s = [], []
    for l in range(DEPTH):
        w = (norm_mix[l], w_in[l], w_out[l], norm_ffn[l], w_up[l], w_down[l], cmp_pos[l], cmp_w1[l], cmp_b1[l],
             cmp_w2[l], cmp_b2[l], rg_conv_w[l], rg_conv_b[l], rg_wa[l], rg_ba[l], rg_wx[l], rg_bx[l],
             rg_lambda[l], lb[l], hg_gain[l])
        wb = (None, w_out_b[l], w_up_b[l], w_down_b[l])
        pw = (w_rel[l], tuple(a[l] for a in cws), tables, tables_cmp)
        xp, sp = forward_layer_prompt(xp, w, wb, pw)
        caches = (page_table, cache_nsa_cmp_kv, cache_nsa_sel_kv, cache_nsa_win_kv,
                  state_rglru_h[l], state_rglru_conv[l], state_hgrn_s[l])
        xs, ss = forward_layer_sample(l, xs, w, wb, (w_rel[l], pw[1], tables_s), caches)
        st_p.append(sp)
        st_s.append(ss)

    def stack(sts, i):
        return jnp.stack([s[i] for s in sts], axis=0)

    y_prompt = final_rms(xp.reshape(-1, D_MODEL), final_norm).reshape(xp.shape)
    y_sample = final_rms(xs.reshape(-1, D_MODEL), final_norm).reshape(xs.shape)
    return (y_prompt, y_sample, stack(st_p, 0), stack(st_p, 1), stack(st_p, 2), stack(st_p, 3), stack(st_p, 4),
            stack(st_p, 5), stack(st_s, 0), stack(st_s, 1), stack(st_s, 2), stack(st_s, 3), stack(st_s, 4),
            stack(st_s, 5))
```

```python
import functools

import jax
import jax.numpy as jnp
from jax import lax
from jax.experimental import pallas as pl
from jax.experimental.pallas import tpu as pltpu

F32 = jnp.float32
BF16 = jnp.bfloat16

D_MODEL = 1024
DEPTH = 4
PAGE_SIZE = 128
HEAD_DIM = 64
NSA_HEADS = D_MODEL // (2 * HEAD_DIM)
NSA_KV_HEADS = 2
NSA_HPG = NSA_HEADS // NSA_KV_HEADS
NSA_WIDTH = NSA_HEADS * HEAD_DIM
KV_W = NSA_KV_HEADS * HEAD_DIM
CMP_BLOCK = 32
CMP_STRIDE = 16
CMP_RATIO = CMP_BLOCK // CMP_STRIDE
CMP_HIDDEN = 256
SEL_BLOCK = 64
SEL_TOPK = 16
WINDOW = 512
Q_BLOCK = 128
FORCE_SCORE = 1e6
ROPE_DIM = HEAD_DIM // 4
ROPE_THETA = 500000.0
RG_WIDTH = D_MODEL // 4
RG_BLOCKS = 4
RG_BW = RG_WIDTH // RG_BLOCKS
RG_C = 8.0
CONV_W = 4
HG_HEADS = 4
HG_DK = 64
HG_DV = D_MODEL // 4 // HG_HEADS
HG_WIDTH = HG_HEADS * HG_DV
HG_CHUNK = 64
MIX_WIDTH = NSA_WIDTH + RG_WIDTH + HG_WIDTH
D_FF = 4 * D_MODEL
EPS = 1e-6

LANE = 128
VMEM_LIMIT = 56 * 1024 * 1024


def _row_tile(m, want):
    t = min(m, want)
    while m % t:
        t //= 2
    return t


def _rms(x, g):
    return x * lax.rsqrt(jnp.mean(x * x, axis=-1, keepdims=True) + EPS) * g


def _out_ffn_body(x_ref, mix_ref, wo_ref, g_ref, wu_ref, wd_ref, o_ref, *, f_tile):
    x1 = x_ref[...] + jnp.dot(mix_ref[...].astype(BF16), wo_ref[...], preferred_element_type=F32)
    y = _rms(x1, g_ref[...]).astype(BF16)
    acc = x1
    for j in range(wu_ref.shape[1] // f_tile):
        h = jnp.maximum(jnp.dot(y, wu_ref[:, j * f_tile:(j + 1) * f_tile], preferred_element_type=F32), 0.0)
        acc = acc + jnp.dot((h * h).astype(BF16), wd_ref[j * f_tile:(j + 1) * f_tile, :], preferred_element_type=F32)
    o_ref[...] = acc


def out_ffn(x, mix, wo, g, wu, wd, tm=512, f_tile=512):
    m, d = x.shape
    dm = mix.shape[1]
    dff = wu.shape[1]
    tm = _row_tile(m, tm)
    const = lambda i: (0, 0)
    return pl.pallas_call(
        functools.partial(_out_ffn_body, f_tile=f_tile),
        grid=(m // tm,),
        in_specs=[
            pl.BlockSpec((tm, d), lambda i: (i, 0)),
            pl.BlockSpec((tm, dm), lambda i: (i, 0)),
            pl.BlockSpec((dm, d), const),
            pl.BlockSpec((1, d), const),
            pl.BlockSpec((d, dff), const),
            pl.BlockSpec((dff, d), const),
        ],
        out_specs=pl.BlockSpec((tm, d), lambda i: (i, 0)),
        out_shape=jax.ShapeDtypeStruct((m, d), F32),
        compiler_params=pltpu.CompilerParams(
            dimension_semantics=("arbitrary",), vmem_limit_bytes=VMEM_LIMIT),
        name="out_proj_ffn",
    )(x, mix, wo, g.reshape(1, d), wu, wd)


def _final_norm_body(x_ref, g_ref, o_ref):
    o_ref[...] = _rms(x_ref[...], g_ref[...])


def final_rms(x, g, tm=1024):
    m, d = x.shape
    tm = _row_tile(m, tm)
    return pl.pallas_call(
        _final_norm_body,
        grid=(m // tm,),
        in_specs=[pl.BlockSpec((tm, d), lambda i: (i, 0)), pl.BlockSpec((1, d), lambda i: (0, 0))],
        out_specs=pl.BlockSpec((tm, d), lambda i: (i, 0)),
        out_shape=jax.ShapeDtypeStruct((m, d), F32),
        name="final_norm",
    )(x, g.reshape(1, d))


MASK_BIG = 2.0 ** 100
LOG2E = 1.4426950408889634

_O_Q = 0
_O_KV = NSA_WIDTH
_O_GATE = _O_KV + 6 * KV_W
_O_RG = _O_GATE + 3 * NSA_HEADS
_O_HG = _O_RG + 2 * RG_WIDTH

_P_QZ = 0
_P_CMP = _P_QZ + NSA_HEADS * LANE
_P_SEL = _P_CMP + 2 * KV_W
_P_WIN = _P_SEL + 2 * KV_W
_P_KVS = _P_WIN + 2 * KV_W
_P_KVW = _P_KVS + 2 * KV_W
_P_GATE = _P_KVW + 2 * KV_W
_P_RG = _P_GATE + NSA_KV_HEADS * LANE
_P_HG = _P_RG + 2 * RG_WIDTH
_P_END = _P_HG + 4 * HG_WIDTH


def _in_proj_columns():
    import numpy as np
    src = -np.ones((_P_END,), np.int64)
    for hq in range(NSA_HEADS):
        src[_P_QZ + hq * LANE:_P_QZ + hq * LANE + HEAD_DIM] = _O_Q + hq * HEAD_DIM + np.arange(HEAD_DIM)
    src[_P_CMP:_P_CMP + 2 * KV_W] = _O_KV + np.arange(2 * KV_W)
    src[_P_SEL:_P_SEL + 2 * KV_W] = _O_KV + 2 * KV_W + np.arange(2 * KV_W)
    src[_P_WIN:_P_WIN + 2 * KV_W] = _O_KV + 4 * KV_W + np.arange(2 * KV_W)
    for base, off in ((_P_KVS, _O_KV + 2 * KV_W), (_P_KVW, _O_KV + 4 * KV_W)):
        for g in range(NSA_KV_HEADS):
            src[base + g * LANE:base + g * LANE + HEAD_DIM] = off + g * HEAD_DIM + np.arange(HEAD_DIM)
            src[base + g * LANE + HEAD_DIM:base + (g + 1) * LANE] = off + KV_W + g * HEAD_DIM + np.arange(HEAD_DIM)
    ng = 3 * NSA_HPG
    for g in range(NSA_KV_HEADS):
        src[_P_GATE + g * LANE:_P_GATE + g * LANE + ng] = _O_GATE + g * ng + np.arange(ng)
    src[_P_RG:_P_RG + 2 * RG_WIDTH] = _O_RG + np.arange(2 * RG_WIDTH)
    src[_P_HG:_P_HG + 4 * HG_WIDTH] = _O_HG + np.arange(4 * HG_WIDTH)
    return src


def relayout_w_in(w_in):
    import numpy as np
    src = _in_proj_columns()
    cols = jnp.take(w_in, jnp.asarray(np.maximum(src, 0), jnp.int32), axis=-1)
    return jnp.where(jnp.asarray(src >= 0), cols, 0.0).astype(BF16)


def rope_tables(pos):
    import numpy as np
    half = ROPE_DIM // 2
    inv = ROPE_THETA ** (-jnp.arange(half, dtype=F32) * 2.0 / ROPE_DIM)
    ang = pos.astype(F32)[:, None] * inv
    cos, sin = jnp.cos(ang), jnp.sin(ang)
    lane = np.arange(LANE)
    d = lane % HEAD_DIM
    idx = jnp.asarray(d % half, jnp.int32)
    cos_l, sin_l = jnp.take(cos, idx, axis=1), jnp.take(sin, idx, axis=1)
    out = []
    for active in (lane < ROPE_DIM, d < ROPE_DIM):
        first = jnp.asarray(active & (d < half))
        second = jnp.asarray(active & (d >= half) & (d < ROPE_DIM))
        out += [jnp.where(first | second, cos_l, 1.0), jnp.where(first, -sin_l, 0.0), jnp.where(second, sin_l, 0.0)]
    return tuple(out)


def _rope128(x, c, s1, s2):
    return x * c + pltpu.roll(x, LANE - ROPE_DIM // 2, 1) * s1 + pltpu.roll(x, ROPE_DIM // 2, 1) * s2


def _in_proj_prompt_body(x_ref, g_ref, w_ref, cq_ref, s1q_ref, s2q_ref, ckk_ref, s1kk_ref, s2kk_ref,
                         qz_ref, cmp_ref, sel_ref, win_ref, kvs_ref, kvw_ref, gate_ref, rg_ref, hg_ref,
                         *, tiles_per_seq):
    tm = x_ref.shape[0]
    y = _rms(x_ref[...], g_ref[...]).astype(BF16)

    def mm(c0, n):
        return jnp.dot(y, w_ref[:, c0:c0 + n], preferred_element_type=F32)

    cq, s1q, s2q = cq_ref[...], s1q_ref[...], s2q_ref[...]
    ckk, s1kk, s2kk = ckk_ref[...], s1kk_ref[...], s2kk_ref[...]
    scale = HEAD_DIM ** -0.5 * LOG2E
    for hq in range(NSA_HEADS):
        qh = _rope128(mm(_P_QZ + hq * LANE, LANE), cq, s1q, s2q) * scale
        qz_ref[:, hq * LANE:(hq + 1) * LANE] = qh.astype(BF16)
    cmp_ref[...] = mm(_P_CMP, 2 * KV_W)
    for base, ref in ((_P_SEL, sel_ref), (_P_WIN, win_ref)):
        ref[:, 0:LANE] = _rope128(mm(base, LANE), ckk, s1kk, s2kk)
        ref[:, LANE:2 * LANE] = mm(base + LANE, LANE)
    row0 = (pl.program_id(0) % tiles_per_seq) * tm
    blk = (row0 + lax.broadcasted_iota(jnp.int32, (tm, 1), 0)) // SEL_BLOCK
    onehot = jnp.where(lax.broadcasted_iota(jnp.int32, (1, LANE), 1) == blk, MASK_BIG, 0.0).astype(BF16)
    for g in range(NSA_KV_HEADS):
        kvs_ref[g, :, 0:LANE] = _rope128(mm(_P_KVS + g * LANE, LANE), cq, s1q, s2q).astype(BF16)
        kvs_ref[g, :, LANE:2 * LANE] = onehot
        kvw_ref[g] = _rope128(mm(_P_KVW + g * LANE, LANE), cq, s1q, s2q).astype(BF16)
        gate_ref[g] = jax.nn.sigmoid(mm(_P_GATE + g * LANE, LANE))
    rg_ref[...] = mm(_P_RG, 2 * RG_WIDTH)
    hg_ref[...] = mm(_P_HG, 4 * HG_WIDTH)


def in_proj_prompt(x, g, w_rel, tables, seq_len, tm=512):
    m, d = x.shape
    tm = _row_tile(seq_len, tm)
    tiles_per_seq = seq_len // tm
    G = NSA_KV_HEADS
    row = lambda i: (i, 0)
    grow = lambda i: (0, i, 0)
    tab = pl.BlockSpec((tm, LANE), lambda i: (i % tiles_per_seq, 0))
    return pl.pallas_call(
        functools.partial(_in_proj_prompt_body, tiles_per_seq=tiles_per_seq),
        grid=(m // tm,),
        in_specs=[pl.BlockSpec((tm, d), row), pl.BlockSpec((1, d), lambda i: (0, 0)),
                  pl.BlockSpec((d, _P_END), lambda i: (0, 0))] + [tab] * 6,
        out_specs=[
            pl.BlockSpec((tm, NSA_HEADS * LANE), row),
            pl.BlockSpec((tm, 2 * KV_W), row), pl.BlockSpec((tm, 2 * KV_W), row), pl.BlockSpec((tm, 2 * KV_W), row),
            pl.BlockSpec((G, tm, 2 * LANE), grow), pl.BlockSpec((G, tm, LANE), grow), pl.BlockSpec((G, tm, LANE), grow),
            pl.BlockSpec((tm, 2 * RG_WIDTH), row), pl.BlockSpec((tm, 4 * HG_WIDTH), row),
        ],
        out_shape=[
            jax.ShapeDtypeStruct((m, NSA_HEADS * LANE), BF16),
            jax.ShapeDtypeStruct((m, 2 * KV_W), F32), jax.ShapeDtypeStruct((m, 2 * KV_W), F32),
            jax.ShapeDtypeStruct((m, 2 * KV_W), F32),
            jax.ShapeDtypeStruct((G, m, 2 * LANE), BF16), jax.ShapeDtypeStruct((G, m, LANE), BF16),
            jax.ShapeDtypeStruct((G, m, LANE), F32),
            jax.ShapeDtypeStruct((m, 2 * RG_WIDTH), F32), jax.ShapeDtypeStruct((m, 4 * HG_WIDTH), F32),
        ],
        compiler_params=pltpu.CompilerParams(
            dimension_semantics=("arbitrary",), vmem_limit_bytes=VMEM_LIMIT),
        name="in_proj_prompt",
    )(x, g.reshape(1, d), w_rel, *tables)


def compress_weights(cmp_pos, cmp_w1, cmp_b1, cmp_w2, cmp_b2):
    L = cmp_w1.shape[0]
    G, S, HD, CH = NSA_KV_HEADS, CMP_STRIDE, HEAD_DIM, CMP_HIDDEN
    w1 = cmp_w1.reshape(L, 2, CMP_RATIO, S, HD, CH)
    z = jnp.zeros_like(w1)
    w1bd = jnp.concatenate([jnp.concatenate([w1, z], axis=-1), jnp.concatenate([z, w1], axis=-1)], axis=-2)
    pe = jnp.tile(cmp_pos.reshape(L, 2, CMP_RATIO, S, HD), (1, 1, 1, 1, G))
    b1 = jnp.tile(cmp_b1, (1, 1, G)).reshape(L, 2, 1, G * CH)
    zc = jnp.zeros((L, CH, HD), F32)
    w2 = jnp.stack([jnp.concatenate([cmp_w2[:, 0], zc], axis=-1), jnp.concatenate([zc, cmp_w2[:, 1]], axis=-1)], axis=1)
    b2 = jnp.concatenate([cmp_b2[:, 0], cmp_b2[:, 1]], axis=-1).reshape(L, 1, 2 * HD)
    return w1bd.astype(BF16), pe, b1, w2.astype(BF16), b2


def _compress_body(xk_ref, xv_ref, pe_ref, w1_ref, b1_ref, w2_ref, b2_ref, c_ref, s1_ref, s2_ref, o_ref, *, nch, n_cmp):
    gelus = []
    for kv, x_ref in enumerate((xk_ref, xv_ref)):
        acc = [jnp.zeros((nch, NSA_KV_HEADS * CMP_HIDDEN), F32) for _ in range(CMP_RATIO)]
        for s in range(CMP_STRIDE):
            xs = x_ref[pl.ds(s, nch, stride=CMP_STRIDE), :]
            for r in range(CMP_RATIO):
                acc[r] = acc[r] + jnp.dot((xs + pe_ref[kv, r, s:s + 1, :]).astype(BF16), w1_ref[kv, r, s],
                                          preferred_element_type=F32)
        h = b1_ref[kv] + acc[0] + pltpu.roll(acc[1], nch - 1, 0)
        gelus.append(jax.nn.gelu(h).astype(BF16))
    valid = lax.broadcasted_iota(jnp.int32, (nch, 1), 0) < n_cmp
    for g in range(NSA_KV_HEADS):
        cols = slice(g * CMP_HIDDEN, (g + 1) * CMP_HIDDEN)
        out = (jnp.dot(gelus[0][:, cols], w2_ref[0], preferred_element_type=F32)
               + jnp.dot(gelus[1][:, cols], w2_ref[1], preferred_element_type=F32) + b2_ref[...])
        out = _rope128(out, c_ref[...], s1_ref[...], s2_ref[...])
        o_ref[g] = jnp.where(valid, out, 0.0).astype(BF16)


def compress(raw, cw, tables_q, seq_len):
    w1bd, pe, b1, w2, b2 = cw
    m = raw.shape[0]
    nb = m // seq_len
    nch = seq_len // CMP_STRIDE
    n_cmp = nch - CMP_RATIO + 1
    full = lambda shape: pl.BlockSpec(shape, lambda b: (0,) * len(shape))
    return pl.pallas_call(
        functools.partial(_compress_body, nch=nch, n_cmp=n_cmp),
        grid=(nb,),
        in_specs=[pl.BlockSpec((seq_len, KV_W), lambda b: (b, 0)), pl.BlockSpec((seq_len, KV_W), lambda b: (b, 1)),
                  full(pe.shape), full(w1bd.shape), full(b1.shape), full(w2.shape), full(b2.shape),
                  full((nch, LANE)), full((nch, LANE)), full((nch, LANE))],
        out_specs=pl.BlockSpec((None, NSA_KV_HEADS, nch, LANE), lambda b: (b, 0, 0, 0)),
        out_shape=jax.ShapeDtypeStruct((nb, NSA_KV_HEADS, nch, LANE), BF16),
        compiler_params=pltpu.CompilerParams(
            dimension_semantics=("arbitrary",), vmem_limit_bytes=VMEM_LIMIT),
        name="compress_kv",
    )(raw, raw, pe, w1bd, b1, w2, b2, *tables_q)


_NT = (((1,), (1,)), ((), ()))


def _masked_softmax2(s, mask):
    s = jnp.where(mask, s, -1e30)
    m = jnp.max(s, axis=-1, keepdims=True)
    e = jnp.where(mask, jnp.exp2(s - m), 0.0)
    return e / jnp.maximum(jnp.sum(e, axis=-1, keepdims=True), 1e-30)


N_FORCED = 3


def _select_blocks(imp, valid, forced, block_idx):
    score = jnp.where(valid & jnp.logical_not(forced), imp, -FORCE_SCORE)

    def pick_one(_, carry):
        score, chosen = carry
        hit = block_idx == jnp.argmax(score, axis=1, keepdims=True)
        return jnp.where(hit, -jnp.inf, score), jnp.where(hit, 1.0, chosen)

    chosen = jnp.broadcast_to(jnp.where(forced, 1.0, 0.0), imp.shape)
    _, chosen = lax.fori_loop(0, SEL_TOPK - N_FORCED, pick_one, (score, chosen), unroll=True)
    return jnp.where(valid, chosen, 0.0)


def _nsa_prompt_body(qz_ref, gate_ref, kvc_ref, smap_ref, kvs_ref, kvw_ref, o_ref, *, tk):
    QB, H = Q_BLOCK, NSA_HPG
    q0 = pl.program_id(2) * QB
    q4 = jnp.concatenate([qz_ref[:, h * LANE:(h + 1) * LANE] for h in range(H)], axis=0)
    t_col = q0 + (lax.broadcasted_iota(jnp.int32, (H * QB, 1), 0) & (QB - 1))

    kvc = kvc_ref[...]
    ncp = kvc.shape[0]
    s_c = lax.dot_general(q4, kvc, _NT, preferred_element_type=F32)
    c_end = lax.broadcasted_iota(jnp.int32, (1, ncp), 1) * CMP_STRIDE + (CMP_BLOCK - 1)
    p_c = _masked_softmax2(s_c, c_end <= t_col)
    o_c = jnp.dot(p_c.astype(BF16), kvc, preferred_element_type=F32)

    psum = p_c[0:QB]
    for h in range(1, H):
        psum = psum + p_c[h * QB:(h + 1) * QB]
    hi = psum.astype(BF16)
    lo = (psum - hi.astype(F32)).astype(BF16)
    smap = smap_ref[...]
    imp = jnp.dot(hi, smap, preferred_element_type=F32) + jnp.dot(lo, smap, preferred_element_type=F32)

    jf = lax.broadcasted_iota(jnp.int32, (QB, LANE), 1)
    t_tok = q0 + lax.broadcasted_iota(jnp.int32, (QB, 1), 0)
    cur = t_tok // SEL_BLOCK
    valid = jf * SEL_BLOCK <= t_tok
    forced = (jf == 0) | (jf == cur) | (jf == cur - 1)
    selm1 = (_select_blocks(imp, valid, forced, jf) - 1.0).astype(BF16)
    q_aug = jnp.concatenate([q4, jnp.concatenate([selm1] * H, axis=0)], axis=1)

    def tile_step(kt, carry):
        m, l, acc = carry
        hk = tk // 2
        start = pl.multiple_of(kt * tk, tk)
        tiles = [kvs_ref[pl.ds(start + j * hk, hk), :] for j in range(2)]
        ss = [lax.dot_general(q_aug, tl, _NT, preferred_element_type=F32) for tl in tiles]
        for j in range(2):
            kpos = kt * tk + j * hk + lax.broadcasted_iota(jnp.int32, (1, hk), 1)
            ss[j] = jnp.where(kpos <= t_col, ss[j], -1e30)
        m_new = jnp.maximum(m, jnp.maximum(jnp.max(ss[0], axis=1, keepdims=True),
                                           jnp.max(ss[1], axis=1, keepdims=True)))
        alpha = jnp.exp2(m - m_new)
        ps = [jnp.exp2(sj - m_new) for sj in ss]
        l = alpha * l + jnp.sum(ps[0], axis=1, keepdims=True) + jnp.sum(ps[1], axis=1, keepdims=True)
        pv = (jnp.dot(ps[0].astype(BF16), tiles[0][:, 0:LANE], preferred_element_type=F32)
              + jnp.dot(ps[1].astype(BF16), tiles[1][:, 0:LANE], preferred_element_type=F32))
        return m_new, l, alpha * acc + pv

    init = (jnp.full((H * QB, 1), -1e30, F32), jnp.zeros((H * QB, 1), F32), jnp.zeros((H * QB, LANE), F32))
    UNROLL = 4

    def multi_step(i, c):
        for u in range(UNROLL):
            c = tile_step(UNROLL * i + u, c)
        return c

    n_tiles = q0 // tk + 1
    n_multi = n_tiles // UNROLL
    carry = lax.fori_loop(0, n_multi, multi_step, init)
    _, l_s, acc_s = lax.fori_loop(UNROLL * n_multi, n_tiles, tile_step, carry)
    o_s = acc_s / l_s

    wlen = WINDOW + QB
    w0 = pl.multiple_of(jnp.maximum(q0 - WINDOW, 0), QB)
    wt = kvw_ref[pl.ds(w0, wlen), :]
    s_w = lax.dot_general(q4, wt, _NT, preferred_element_type=F32)
    dist = t_col - (w0 + lax.broadcasted_iota(jnp.int32, (1, wlen), 1))
    p_w = _masked_softmax2(s_w, (dist >= 0) & (dist < WINDOW))
    o_w = jnp.dot(p_w.astype(BF16), wt, preferred_element_type=F32)

    comb = []
    for h in range(H):
        r = slice(h * QB, (h + 1) * QB)
        comb.append(gate_ref[:, 3 * h:3 * h + 1] * o_c[r] + gate_ref[:, 3 * h + 1:3 * h + 2] * o_s[r]
                    + gate_ref[:, 3 * h + 2:3 * h + 3] * o_w[r])
    low = lax.broadcasted_iota(jnp.int32, (1, LANE), 1) < HEAD_DIM
    for p in range(H // 2):
        o_ref[:, p * LANE:(p + 1) * LANE] = jnp.where(low, pltpu.roll(comb[2 * p], HEAD_DIM, 1), comb[2 * p + 1])


def nsa_prompt(qz, gate, kvc, kvs, kvw, nb, seq_len, tk=512):
    m = qz.shape[0]
    G, QB = NSA_KV_HEADS, Q_BLOCK
    nqb = seq_len // QB
    ncp = kvc.shape[2]
    assert seq_len % tk == 0 and seq_len >= WINDOW + QB and seq_len // SEL_BLOCK <= LANE
    import numpy as np
    c0 = np.arange(ncp)[:, None] * CMP_STRIDE
    s0 = np.arange(LANE)[None, :] * SEL_BLOCK
    n_cmp = ncp - CMP_RATIO + 1
    ov = (c0 < s0 + SEL_BLOCK) & (c0 + CMP_BLOCK > s0) & (np.arange(ncp)[:, None] < n_cmp)
    smap = jnp.asarray(ov, BF16)
    return pl.pallas_call(
        functools.partial(_nsa_prompt_body, tk=tk),
        grid=(nb, G, nqb),
        in_specs=[
            pl.BlockSpec((QB, NSA_HPG * LANE), lambda b, g, q: (b * nqb + q, g)),
            pl.BlockSpec((None, QB, LANE), lambda b, g, q: (g, b * nqb + q, 0)),
            pl.BlockSpec((None, None, ncp, LANE), lambda b, g, q: (b, g, 0, 0)),
            pl.BlockSpec((ncp, LANE), lambda b, g, q: (0, 0)),
            pl.BlockSpec((None, seq_len, 2 * LANE), lambda b, g, q: (g, b, 0)),
            pl.BlockSpec((None, seq_len, LANE), lambda b, g, q: (g, b, 0)),
        ],
        out_specs=pl.BlockSpec((QB, NSA_HPG * HEAD_DIM), lambda b, g, q: (b * nqb + q, g)),
        out_shape=jax.ShapeDtypeStruct((m, NSA_WIDTH), F32),
        compiler_params=pltpu.CompilerParams(
            dimension_semantics=("arbitrary", "arbitrary", "arbitrary"), vmem_limit_bytes=VMEM_LIMIT),
        name="nsa_prompt",
    )(qz, gate, kvc, smap, kvs, kvw)


def _shift_rows(x, prev8, d):
    rolled = pltpu.roll(x, d, 0)
    row8 = lax.broadcasted_iota(jnp.int32, (8, 1), 0)
    first = jnp.where(row8 < d, pltpu.roll(prev8, d, 0), rolled[0:8])
    return jnp.concatenate([first, rolled[8:]], axis=0)


def _rglru_body(rg_ref, buf_ref, h0_ref, cw_ref, cb_ref, wg_ref, bg_ref, lam_ref,
                o_ref, hout_ref, bufout_ref, tail_ref, h_ref):
    tm = rg_ref.shape[0]
    W = RG_WIDTH

    @pl.when(pl.program_id(1) == 0)
    def _():
        tail_ref[...] = buf_ref[...]
        h_ref[...] = h0_ref[...]

    x = rg_ref[:, 0:W]
    gr = rg_ref[:, W:2 * W]
    prev8 = tail_ref[...]
    xc = cb_ref[...] + cw_ref[CONV_W - 1:CONV_W, :] * x
    for d in range(1, CONV_W):
        xc = xc + cw_ref[CONV_W - 1 - d:CONV_W - d, :] * _shift_rows(x, prev8, d)
    gates = jnp.dot(xc.astype(BF16), wg_ref[...], preferred_element_type=F32) + bg_ref[...]
    r = jax.nn.sigmoid(gates[:, 0:W])
    i = jax.nn.sigmoid(gates[:, W:2 * W])
    log_a = -RG_C * r * jax.nn.softplus(-lam_ref[...])
    a = jnp.exp(log_a)
    th = jnp.tanh(log_a)
    b = jnp.sqrt(-2.0 * th / (1.0 - th)) * (i * xc)
    row = lax.broadcasted_iota(jnp.int32, (tm, 1), 0)
    d = 1
    while d < tm:
        keep = row >= d
        a_sh = jnp.where(keep, pltpu.roll(a, d, 0), 1.0)
        b_sh = jnp.where(keep, pltpu.roll(b, d, 0), 0.0)
        b = a * b_sh + b
        a = a * a_sh
        d *= 2
    h = a * h_ref[7:8, :] + b
    o_ref[...] = h * jax.nn.gelu(gr)
    h_ref[...] = h[tm - 8:tm]
    tail_ref[...] = x[tm - 8:tm]
    hout_ref[...] = h[tm - 8:tm]
    bufout_ref[...] = x[tm - 8:tm]


def rglru_prompt(rg, conv_buf, h0, conv_w, conv_b, wa, ba, wx, bx, lam, seq_len, tm=256):
    m = rg.shape[0]
    nb = m // seq_len
    W = RG_WIDTH
    tm = _row_tile(seq_len, tm)
    nt = seq_len // tm
    bd = jax.scipy.linalg.block_diag
    wg = jnp.concatenate([bd(*[wa[n] for n in range(RG_BLOCKS)]), bd(*[wx[n] for n in range(RG_BLOCKS)])], axis=1).astype(BF16)
    bg = jnp.concatenate([ba, bx]).reshape(1, 2 * W)
    buf8 = jnp.pad(conv_buf, ((0, 0), (8 - (CONV_W - 1), 0), (0, 0)))
    h08 = jnp.broadcast_to(h0[:, None, :], (nb, 8, W))
    const = lambda shape: pl.BlockSpec(shape, lambda b, t: (0,) * len(shape))
    per_b = pl.BlockSpec((None, 8, W), lambda b, t: (b, 0, 0))
    out, h8, nbuf8 = pl.pallas_call(
        _rglru_body,
        grid=(nb, nt),
        in_specs=[pl.BlockSpec((tm, 2 * W), lambda b, t: (b * nt + t, 0)), per_b, per_b,
                  const((CONV_W, W)), const((1, W)), const((W, 2 * W)), const((1, 2 * W)), const((1, W))],
        out_specs=[pl.BlockSpec((tm, W), lambda b, t: (b * nt + t, 0)), per_b, per_b],
        out_shape=[jax.ShapeDtypeStruct((m, W), F32), jax.ShapeDtypeStruct((nb, 8, W), F32),
                   jax.ShapeDtypeStruct((nb, 8, W), F32)],
        scratch_shapes=[pltpu.VMEM((8, W), F32), pltpu.VMEM((8, W), F32)],
        compiler_params=pltpu.CompilerParams(dimension_semantics=("arbitrary", "arbitrary")),
        name="rglru_prompt",
    )(rg, buf8, h08, conv_w, conv_b.reshape(1, W), wg, bg, lam.reshape(1, W))
    return out, h8[:, 7], nbuf8[:, 8 - (CONV_W - 1):]


def _split_dot(x, w_bf16, pieces):
    acc = None
    for _ in range(pieces):
        xb = x.astype(BF16)
        part = jnp.dot(xb, w_bf16, preferred_element_type=F32)
        acc = part if acc is None else acc + part
        x = x - xb.astype(F32)
    return acc


HG_SAFE_DECAY = 75.0


def _hgrn_body(hg_ref, s0_ref, lb_ref, gain_ref, tri_ref, tribd_ref, ones_ref, o_ref, sout_ref, st_ref, *, chunk):
    tm = hg_ref.shape[0]
    W, H = HG_WIDTH, HG_HEADS
    C = chunk
    nck = tm // C

    @pl.when(pl.program_id(1) == 0)
    def _():
        st_ref[...] = s0_ref[...]

    lb = lb_ref[...]
    ones_blk = ones_ref[...]
    same_head = ones_blk > 0

    def finish(o, rows):
        ms = _split_dot(o * o, ones_blk, 2) * (1.0 / HG_DV)
        o = o * lax.rsqrt(ms + EPS) * gain_ref[...]
        o_ref[rows, :] = o * jax.nn.silu(hg_ref[rows, 3 * W:4 * W])

    q = jax.nn.silu(hg_ref[:, 0:W])
    f = lb + (1.0 - lb) * jax.nn.sigmoid(hg_ref[:, W:2 * W])
    v = hg_ref[:, 2 * W:3 * W]
    k = 1.0 - f
    bcum = _split_dot_left(tribd_ref[...], jnp.log(f))
    b_last = [bcum[(c + 1) * C - 1:(c + 1) * C, :] for c in range(nck)]
    total = b_last[0]
    for c in range(1, nck):
        total = jnp.minimum(total, b_last[c])
    safe = jnp.min(total) >= -HG_SAFE_DECAY

    @pl.when(safe)
    def _():
        qt = q * jnp.exp(bcum)
        kt = (k * jnp.exp(-bcum)).astype(BF16)
        vb = v.astype(BF16)
        lane_head = lax.broadcasted_iota(jnp.int32, (1, W), 1) // HG_DV
        q_exp = jnp.concatenate([jnp.where(lane_head == h, qt, 0.0) for h in range(H)], axis=0).astype(BF16)
        a = lax.dot_general(q_exp, kt, _NT, preferred_element_type=F32)
        t_row = lax.broadcasted_iota(jnp.int32, (H * tm, 1), 0) & (tm - 1)
        s_col = lax.broadcasted_iota(jnp.int32, (1, tm), 1)
        pair = (s_col <= t_row) & ((s_col // C) == (t_row // C))
        o_exp = jnp.dot(jnp.where(pair, a, 0.0).astype(BF16), vb, preferred_element_type=F32)
        o = jnp.where(lane_head == 0, o_exp[0:tm], 0.0)
        for h in range(1, H):
            o = o + jnp.where(lane_head == h, o_exp[h * tm:(h + 1) * tm], 0.0)
        st = st_ref[...]
        inter = []
        for c in range(nck):
            r = slice(c * C, (c + 1) * C)
            inter.append(lax.dot_general(qt[r].astype(BF16), st.astype(BF16), _NT, preferred_element_type=F32))
            k_hat = (k[r] * jnp.exp(b_last[c] - bcum[r])).astype(BF16)
            upd = lax.dot_general(vb[r], k_hat, (((0,), (0,)), ((), ())), preferred_element_type=F32)
            st = jnp.exp(b_last[c]) * st + jnp.where(same_head, upd, 0.0)
        st_ref[...] = st
        finish(o + jnp.concatenate(inter, axis=0), slice(None))

    @pl.when(jnp.logical_not(safe))
    def _():
        tri = tri_ref[...]
        rowc = lax.broadcasted_iota(jnp.int32, (C, 1), 0)

        def chunk_step(c, _):
            r0 = pl.multiple_of(c * C, C)
            rows = pl.ds(r0, C)
            qc = jax.nn.silu(hg_ref[rows, 0:W])
            fc = lb + (1.0 - lb) * jax.nn.sigmoid(hg_ref[rows, W:2 * W])
            vc = hg_ref[rows, 2 * W:3 * W]
            kc = 1.0 - fc
            bc = _split_dot_left(tri, jnp.log(fc))
            st = st_ref[...]
            o = lax.dot_general((qc * jnp.exp(bc)).astype(BF16), st.astype(BF16), _NT, preferred_element_type=F32)

            def offset_step(dlt, carry):
                o, b_sh, k_sh, v_sh = carry
                p = jnp.where(rowc >= dlt, qc * k_sh * jnp.exp(bc - b_sh), 0.0)
                o = o + _split_dot(p, ones_blk, 2) * v_sh
                return o, pltpu.roll(b_sh, 1, 0), pltpu.roll(k_sh, 1, 0), pltpu.roll(v_sh, 1, 0)

            o, _, _, _ = lax.fori_loop(0, C, offset_step, (o, bc, kc, vc))
            bl = bc[C - 1:C, :]
            k_hat = (kc * jnp.exp(bl - bc)).astype(BF16)
            upd = lax.dot_general(vc.astype(BF16), k_hat, (((0,), (0,)), ((), ())), preferred_element_type=F32)
            st_ref[...] = jnp.exp(bl) * st + jnp.where(same_head, upd, 0.0)
            finish(o, rows)
            return 0

        lax.fori_loop(0, nck, chunk_step, 0)

    sout_ref[...] = st_ref[...]


def _split_dot_left(w_bf16, x):
    acc = None
    for _ in range(3):
        xb = x.astype(BF16)
        part = jnp.dot(w_bf16, xb, preferred_element_type=F32)
        acc = part if acc is None else acc + part
        x = x - xb.astype(F32)
    return acc


def hgrn_prompt(hg, s0, lb, gain, seq_len, tm=256, chunk=HG_CHUNK):
    import numpy as np
    m = hg.shape[0]
    nb = m // seq_len
    W, H = HG_WIDTH, HG_HEADS
    tm = _row_tile(seq_len, tm)
    nt = seq_len // tm
    head = np.arange(W) // HG_DV
    ones_blk = jnp.asarray(head[:, None] == head[None, :], BF16)
    tri = jnp.asarray(np.tril(np.ones((chunk, chunk))), BF16)
    rt = np.arange(tm)
    tri_bd = jnp.asarray((rt[None, :] <= rt[:, None]) & (rt[None, :] // chunk == rt[:, None] // chunk), BF16)
    s0_t = jnp.einsum('bhkv,hg->bhvgk', s0, jnp.eye(H, dtype=s0.dtype)).reshape(nb, W, W)
    const = lambda shape: pl.BlockSpec(shape, lambda b, t: (0,) * len(shape))
    per_b = pl.BlockSpec((None, W, W), lambda b, t: (b, 0, 0))
    out, s_t = pl.pallas_call(
        functools.partial(_hgrn_body, chunk=chunk),
        grid=(nb, nt),
        in_specs=[pl.BlockSpec((tm, 4 * W), lambda b, t: (b * nt + t, 0)), per_b,
                  const((1, W)), const((1, W)), const((chunk, chunk)), const((tm, tm)), const((W, W))],
        out_specs=[pl.BlockSpec((tm, W), lambda b, t: (b * nt + t, 0)), per_b],
        out_shape=[jax.ShapeDtypeStruct((m, W), F32), jax.ShapeDtypeStruct((nb, W, W), F32)],
        scratch_shapes=[pltpu.VMEM((W, W), F32)],
        compiler_params=pltpu.CompilerParams(dimension_semantics=("arbitrary", "arbitrary")),
        name="hgrn_prompt",
    )(hg, s0_t, lb.reshape(1, W), gain.reshape(1, W), tri, tri_bd, ones_blk)
    s5 = s_t.reshape(nb, H, HG_DV, H, HG_DK)
    s_fin = jnp.stack([s5[:, h, :, h, :] for h in range(H)], axis=1).transpose(0, 1, 3, 2)
    return out, s_fin


def _nsa_sample_body(pt_ref, *refs, n_pages, past_len, n_win, spb):
    del pt_ref
    np2 = 2 * n_pages
    cmp_all, sel_all = refs[0:spb * np2], refs[spb * np2:2 * spb * np2]
    (win_ref, selnew_ref, winnew_ref, qlo_ref, qq_ref, gate_ref, pe_ref, w1_ref, b1_ref, w2_ref, b2_ref,
     c_ref, s1_ref, s2_ref, smap_ref, expand_ref, wcol_ref, o_ref, winout_ref, raw_ref) = refs[2 * spb * np2:]
    R = NSA_KV_HEADS * NSA_HPG
    parts = [_nsa_sample_cmp(cmp_all[j * np2:(j + 1) * np2], qlo_ref.at[j], pe_ref, w1_ref, b1_ref, w2_ref, b2_ref,
                             c_ref, s1_ref, s2_ref, smap_ref, raw_ref.at[j], n_pages=n_pages, past_len=past_len)
             for j in range(spb)]
    lane = lax.broadcasted_iota(jnp.int32, (1, LANE), 1)
    cur = past_len // SEL_BLOCK
    valid = lane * SEL_BLOCK <= past_len
    forced = (lane == 0) | (lane == cur) | (lane == cur - 1)
    imp = jnp.concatenate([pt_[1] for pt_ in parts], axis=0)
    selm1 = (_select_blocks(imp, valid, forced, jnp.broadcast_to(lane, (spb * R, LANE))) - 1.0).astype(BF16)
    bias = jnp.dot(selm1, expand_ref[...], preferred_element_type=F32)
    each = lambda ref: [ref.at[j] for j in range(spb)]
    _nsa_sample_attend([sel_all[j * np2:(j + 1) * np2] for j in range(spb)], each(win_ref), each(selnew_ref),
                       each(winnew_ref), each(qq_ref), each(gate_ref), each(wcol_ref), each(o_ref), each(winout_ref),
                       jnp.concatenate([pt_[0] for pt_ in parts], axis=0), bias,
                       n_pages=n_pages, past_len=past_len, n_win=n_win)


def _nsa_sample_cmp(cmp_refs, qlo_ref, pe_ref, w1_ref, b1_ref, w2_ref, b2_ref, c_ref, s1_ref, s2_ref, smap_ref,
                    raw_ref, *, n_pages, past_len):
    G, H, CH = NSA_KV_HEADS, NSA_HPG, CMP_HIDDEN
    R = G * H
    nch = n_pages * (PAGE_SIZE // CMP_STRIDE)
    n_cmp = nch - CMP_RATIO + 1
    _TT = (((1,), (1,)), ((), ()))
    for p in range(n_pages):
        for kv in range(2):
            raw_ref[kv, p * PAGE_SIZE:(p + 1) * PAGE_SIZE, :] = cmp_refs[2 * p + kv][...].T
    t = past_len
    row = lax.broadcasted_iota(jnp.int32, (R, 1), 0)
    grp0 = row < H
    lane = lax.broadcasted_iota(jnp.int32, (1, LANE), 1)

    gelus = []
    for kv in range(2):
        acc = [jnp.zeros((nch, G * CH), F32) for _ in range(CMP_RATIO)]
        for sp in range(CMP_STRIDE // 2):
            xs = jnp.concatenate([raw_ref[kv, pl.ds(s, nch, stride=CMP_STRIDE), :] for s in (2 * sp, 2 * sp + 1)], axis=1)
            for r in range(CMP_RATIO):
                acc[r] = acc[r] + jnp.dot((xs + pe_ref[kv, r, sp:sp + 1, :]).astype(BF16), w1_ref[kv, r, sp],
                                          preferred_element_type=F32)
        h = b1_ref[kv] + acc[0] + pltpu.roll(acc[1], nch - 1, 0)
        gelus.append(jax.nn.gelu(h).astype(BF16))
    valid_c = lax.broadcasted_iota(jnp.int32, (nch, 1), 0) < n_cmp
    qlo = qlo_ref[...].astype(BF16)
    c_end = lax.broadcasted_iota(jnp.int32, (1, nch), 1) * CMP_STRIDE + (CMP_BLOCK - 1)
    smap = smap_ref[...]
    o_c, imp = [], []
    for g in range(G):
        cols = slice(g * CH, (g + 1) * CH)
        kvc = (jnp.dot(gelus[0][:, cols], w2_ref[0], preferred_element_type=F32)
               + jnp.dot(gelus[1][:, cols], w2_ref[1], preferred_element_type=F32) + b2_ref[...])
        kvc = jnp.where(valid_c, _rope128(kvc, c_ref[...], s1_ref[...], s2_ref[...]), 0.0).astype(BF16)
        s_c = lax.dot_general(qlo, kvc, _NT, preferred_element_type=F32)
        p_c = _masked_softmax2(s_c, c_end <= t)
        o_c.append(jnp.dot(p_c.astype(BF16), kvc, preferred_element_type=F32))
        mine = grp0 if g == 0 else jnp.logical_not(grp0)
        psum = jnp.sum(jnp.where(mine, p_c, 0.0), axis=0, keepdims=True)
        imp.append(_split_dot(jnp.broadcast_to(psum, (R, nch)), smap, 2))
    o_c = jnp.where(grp0, pltpu.roll(o_c[0], HEAD_DIM, 1), o_c[1])
    return o_c, jnp.where(grp0, imp[0], imp[1])


def _nsa_sample_attend(sel_refs, win_refs, selnew_refs, winnew_refs, qq_refs, gate_refs, wcol_refs, o_refs,
                       winout_refs, o_c, bias, *, n_pages, past_len, n_win):
    t = past_len
    S = len(qq_refs)
    R = NSA_KV_HEADS * NSA_HPG
    _TT = (((1,), (1,)), ((), ()))
    stack = lambda xs: jnp.concatenate(xs, axis=0)
    rows = lambda x, j: x[j * R:(j + 1) * R]
    qq32 = [r[...] for r in qq_refs]
    qq = [q.astype(BF16) for q in qq32]

    s_s = stack([jnp.concatenate([jnp.dot(qq[j], sel_refs[j][2 * p][...].astype(BF16), preferred_element_type=F32)
                                  for p in range(n_pages)], axis=1) for j in range(S)]) + bias
    s_new = stack([jnp.sum(qq32[j] * selnew_refs[j][:, 0:LANE], axis=1, keepdims=True) for j in range(S)])
    m = jnp.maximum(jnp.max(s_s, axis=1, keepdims=True), s_new)
    p = jnp.exp2(s_s - m)
    p_new = jnp.exp2(s_new - m)
    denom = jnp.sum(p, axis=1, keepdims=True) + p_new
    pb = p.astype(BF16)
    o_s = []
    for j in range(S):
        acc = rows(p_new, j) * selnew_refs[j][:, LANE:2 * LANE]
        for pg in range(n_pages):
            acc = acc + lax.dot_general(rows(pb, j)[:, pg * PAGE_SIZE:(pg + 1) * PAGE_SIZE],
                                        sel_refs[j][2 * pg + 1][...].astype(BF16), _TT, preferred_element_type=F32)
        o_s.append(acc)
    o_s = stack(o_s) / denom

    wins = [r[...] for r in win_refs]
    col = lax.broadcasted_iota(jnp.int32, (1, n_win), 1)
    w_ok = (n_win - col < WINDOW) & (t - n_win + col >= 0)
    s_w = stack([jnp.dot(qq[j], wins[j][0:LANE, :].astype(BF16), preferred_element_type=F32) for j in range(S)])
    s_w = jnp.where(w_ok, s_w, -1e30)
    s_wn = stack([jnp.sum(qq32[j] * winnew_refs[j][:, 0:LANE], axis=1, keepdims=True) for j in range(S)])
    m = jnp.maximum(jnp.max(s_w, axis=1, keepdims=True), s_wn)
    p = jnp.where(w_ok, jnp.exp2(s_w - m), 0.0)
    p_new = jnp.exp2(s_wn - m)
    denom = jnp.sum(p, axis=1, keepdims=True) + p_new
    pb = p.astype(BF16)
    o_w = stack([lax.dot_general(rows(pb, j), wins[j][LANE:2 * LANE, :].astype(BF16), _TT, preferred_element_type=F32)
                 + rows(p_new, j) * winnew_refs[j][:, LANE:2 * LANE] for j in range(S)]) / denom

    gate = stack([r[...] for r in gate_refs])
    o = gate[:, 0:1] * o_c + gate[:, 1:2] * o_s + gate[:, 2:3] * o_w
    for j in range(S):
        o_refs[j][...] = rows(o, j)
        winout_refs[j][...] = jnp.where(col == n_win - 1, wcol_refs[j][...], pltpu.roll(wins[j], n_win - 1, 1))


def nsa_sample(l, page_table, cache_cmp, cache_sel, cache_win, sel_new, win_new, qz, gate, cw):
    import numpy as np
    n, n_pages = page_table.shape
    past_len = n_pages * PAGE_SIZE
    n_win = cache_win.shape[2]
    depth, n_pool = cache_cmp.shape[:2]
    G, H = NSA_KV_HEADS, NSA_HPG
    R = G * H
    nch = past_len // CMP_STRIDE
    assert past_len // SEL_BLOCK + 1 <= LANE and n_win % 8 == 0
    w1bd, pe, b1, w2, b2 = cw
    w1p = w1bd.reshape(2, CMP_RATIO, CMP_STRIDE // 2, 2 * LANE, G * CMP_HIDDEN)
    pep = pe.reshape(2, CMP_RATIO, CMP_STRIDE // 2, 2 * LANE)
    c_end = jnp.arange(nch, dtype=jnp.int32) * CMP_STRIDE + CMP_BLOCK - 1
    tabs = rope_tables(c_end)[:3]
    c0 = np.arange(nch)[:, None] * CMP_STRIDE
    s0 = np.arange(LANE)[None, :] * SEL_BLOCK
    ov = (c0 < s0 + SEL_BLOCK) & (c0 + CMP_BLOCK > s0) & (np.arange(nch)[:, None] < nch - CMP_RATIO + 1)
    smap = jnp.asarray(ov, BF16)
    expand = jnp.asarray(np.where(np.arange(past_len)[None, :] // SEL_BLOCK == np.arange(LANE)[:, None], MASK_BIG, 0.0), BF16)
    qlo = qz.reshape(n, R, LANE).astype(F32)
    qq = jnp.concatenate([qlo[:, :H], jnp.roll(qlo[:, H:], HEAD_DIM, axis=-1)], axis=1)
    g3 = jnp.pad(gate[:, :, :3 * H].reshape(G, n, H, 3).transpose(1, 0, 2, 3).reshape(n, R, 3), ((0, 0), (0, 0), (0, LANE - 3)))
    cmp4 = cache_cmp.reshape(depth, n_pool, PAGE_SIZE, 2 * KV_W).transpose(0, 1, 3, 2)
    sel4 = cache_sel.reshape(depth, n_pool, PAGE_SIZE, 2 * KV_W).transpose(0, 1, 3, 2)
    win4 = cache_win.reshape(depth, n, n_win, 2 * KV_W).transpose(0, 1, 3, 2)

    spb = 2 if n % 2 == 0 else 1

    def page_spec(j, p, half):
        return pl.BlockSpec((None, None, LANE, PAGE_SIZE), lambda b, pt: (l, pt[b * spb + j, p], half, 0))

    page_specs = [page_spec(j, p, half) for j in range(spb) for p in range(n_pages) for half in range(2)]
    full = lambda shape: pl.BlockSpec(shape, lambda b, pt: (0,) * len(shape))
    per_seq = lambda shape: pl.BlockSpec((spb,) + shape, lambda b, pt: (b,) + (0,) * len(shape))
    grid_spec = pltpu.PrefetchScalarGridSpec(
        num_scalar_prefetch=1,
        grid=(n // spb,),
        in_specs=page_specs + page_specs + [
            pl.BlockSpec((None, spb, 2 * KV_W, n_win), lambda b, pt: (l, b, 0, 0)),
            per_seq((1, 2 * KV_W)), per_seq((1, 2 * KV_W)), per_seq((R, LANE)), per_seq((R, LANE)), per_seq((R, LANE)),
            full(pep.shape), full(w1p.shape), full(b1.shape), full(w2.shape), full(b2.shape),
            full((nch, LANE)), full((nch, LANE)), full((nch, LANE)), full((nch, LANE)), full((LANE, past_len)),
            per_seq((2 * KV_W, 1))],
        out_specs=[per_seq((R, LANE)), per_seq((2 * KV_W, n_win))],
        scratch_shapes=[pltpu.VMEM((spb, 2, past_len, LANE), F32)],
    )
    o8, win_out = pl.pallas_call(
        functools.partial(_nsa_sample_body, n_pages=n_pages, past_len=past_len, n_win=n_win, spb=spb),
        grid_spec=grid_spec,
        out_shape=[jax.ShapeDtypeStruct((n, R, LANE), F32), jax.ShapeDtypeStruct((n, 2 * KV_W, n_win), F32)],
        compiler_params=pltpu.CompilerParams(dimension_semantics=("arbitrary",), vmem_limit_bytes=VMEM_LIMIT),
        name="nsa_sample",
    )(page_table, *([cmp4] * (2 * n_pages * spb)), *([sel4] * (2 * n_pages * spb)), win4,
      sel_new.reshape(n, 1, 2 * KV_W), win_new.reshape(n, 1, 2 * KV_W), qlo, qq, g3,
      pep, w1p, b1, w2, b2, *tabs, smap, expand, win_new.reshape(n, 2 * KV_W, 1))
    o5 = o8.reshape(n, G, H, G, HEAD_DIM)
    o = jnp.stack([o5[:, g, :, g, :] for g in range(G)], axis=1).reshape(n, NSA_WIDTH)
    return o, win_out.transpose(0, 2, 1)


def _rglru_sample_body(rg_ref, b0_ref, b1_ref, b2_ref, h0_ref, cw_ref, cb_ref, wg_ref, bg_ref, lam_ref, o_ref, h_ref):
    W = RG_WIDTH
    x = rg_ref[:, 0:W]
    xc = (cb_ref[...] + cw_ref[0:1, :] * b0_ref[...] + cw_ref[1:2, :] * b1_ref[...] + cw_ref[2:3, :] * b2_ref[...]
          + cw_ref[3:4, :] * x)
    gates = jnp.dot(xc.astype(BF16), wg_ref[...], preferred_element_type=F32) + bg_ref[...]
    r = jax.nn.sigmoid(gates[:, 0:W])
    i = jax.nn.sigmoid(gates[:, W:2 * W])
    log_a = -RG_C * r * jax.nn.softplus(-lam_ref[...])
    th = jnp.tanh(log_a)
    h = jnp.exp(log_a) * h0_ref[...] + jnp.sqrt(-2.0 * th / (1.0 - th)) * (i * xc)
    h_ref[...] = h
    o_ref[...] = h * jax.nn.gelu(rg_ref[:, W:2 * W])


def rglru_sample(rg, conv_buf, h0, conv_w, conv_b, wa, ba, wx, bx, lam):
    assert CONV_W == 4
    n = rg.shape[0]
    W = RG_WIDTH
    bd = jax.scipy.linalg.block_diag
    wg = jnp.concatenate([bd(*[wa[k] for k in range(RG_BLOCKS)]), bd(*[wx[k] for k in range(RG_BLOCKS)])], axis=1).astype(BF16)
    bg = jnp.concatenate([ba, bx]).reshape(1, 2 * W)
    out, h = pl.pallas_call(
        _rglru_sample_body,
        out_shape=[jax.ShapeDtypeStruct((n, W), F32), jax.ShapeDtypeStruct((n, W), F32)],
        name="rglru_sample",
    )(rg, conv_buf[:, 0], conv_buf[:, 1], conv_buf[:, 2], h0, conv_w, conv_b.reshape(1, W), wg, bg, lam.reshape(1, W))
    new_buf = jnp.concatenate([conv_buf[:, 1:], rg[:, None, 0:W]], axis=1)
    return out, h, new_buf


def _hgrn_sample_body(q_ref, f_ref, lb_ref, v_ref, g_ref, gain_ref, s0_ref, o_ref, s_ref):
    DK = HG_DK
    lb = lb_ref[...]
    for j in range(q_ref.shape[0]):
        q = jax.nn.silu(q_ref[j])
        f = lb + (1.0 - lb) * jax.nn.sigmoid(f_ref[j])
        k = 1.0 - f
        outs = []
        for h in range(HG_HEADS):
            r = slice(h * DK, (h + 1) * DK)
            s_new = f[r] * s0_ref[j, h] + k[r] * v_ref[j, h:h + 1, :]
            s_ref[j, h] = s_new
            outs.append(jnp.sum(q[r] * s_new, axis=0, keepdims=True))
        o = jnp.concatenate(outs, axis=0)
        o = o * lax.rsqrt(jnp.mean(o * o, axis=-1, keepdims=True) + EPS) * gain_ref[...]
        o_ref[j] = o * jax.nn.silu(g_ref[j])


def hgrn_sample(hg, s0, lb, gain):
    n = hg.shape[0]
    H, DK, DV, W = HG_HEADS, HG_DK, HG_DV, HG_WIDTH
    col = lambda a: a.reshape(n, W, 1)
    spb = _row_tile(n, 8)
    per_seq = lambda shape: pl.BlockSpec((spb,) + shape, lambda b: (b,) + (0,) * len(shape))
    full = lambda shape: pl.BlockSpec(shape, lambda b: (0,) * len(shape))
    o, s_new = pl.pallas_call(
        _hgrn_sample_body,
        grid=(n // spb,),
        in_specs=[per_seq((W, 1)), per_seq((W, 1)), full((W, 1)), per_seq((H, DV)), per_seq((H, DV)), full((H, DV)),
                  per_seq((H, DK, DV))],
        out_specs=[per_seq((H, DV)), per_seq((H, DK, DV))],
        out_shape=[jax.ShapeDtypeStruct((n, H, DV), F32), jax.ShapeDtypeStruct((n, H, DK, DV), F32)],
        compiler_params=pltpu.CompilerParams(dimension_semantics=("arbitrary",)),
        name="hgrn_sample",
    )(col(hg[:, 0:W]), col(hg[:, W:2 * W]), lb.reshape(W, 1), hg[:, 2 * W:3 * W].reshape(n, H, DV),
      hg[:, 3 * W:4 * W].reshape(n, H, DV), gain.reshape(H, DV), s0)
    return o.reshape(n, W), s_new


def forward_layer_prompt(x, w, wb, pw):
    (norm_mix, w_in, w_out, norm_ffn, w_up, w_down, cmp_pos, cmp_w1, cmp_b1, cmp_w2, cmp_b2,
     rg_conv_w, rg_conv_b, rg_wa, rg_ba, rg_wx, rg_bx, rg_lambda, hg_lb, hg_gain) = w
    _, w_out_b, w_up_b, w_down_b = wb
    w_rel, cw, tables, tables_cmp = pw
    B, T = x.shape[:2]
    x2 = x.reshape(B * T, D_MODEL)
    qz, cmp, sel, win, kvs, kvw, gate, rg, hg = in_proj_prompt(x2, norm_mix, w_rel, tables, T)
    kvc = compress(cmp, cw, tables_cmp, T)
    o_nsa = nsa_prompt(qz, gate, kvc, kvs, kvw, B, T)
    kv_shape = (B, T, 2, NSA_KV_HEADS, HEAD_DIM)
    new_cmp, new_sel = cmp.reshape(kv_shape), sel.reshape(kv_shape)
    new_win = win.reshape(kv_shape)[:, T - min(WINDOW, T):]
    rg_buf = jnp.zeros((B, CONV_W - 1, RG_WIDTH), x.dtype)
    rg_h0 = jnp.zeros((B, RG_WIDTH), x.dtype)
    hg_s0 = jnp.zeros((B, HG_HEADS, HG_DK, HG_DV), x.dtype)
    o_rg, new_h, new_buf = rglru_prompt(rg, rg_buf, rg_h0, rg_conv_w, rg_conv_b, rg_wa, rg_ba, rg_wx, rg_bx,
                                        rg_lambda, T)
    o_hg, new_s = hgrn_prompt(hg, hg_s0, hg_lb, hg_gain, T)
    mix = jnp.concatenate([o_nsa, o_rg, o_hg], axis=-1)
    x2 = out_ffn(x2, mix, w_out_b, norm_ffn, w_up_b, w_down_b)
    return x2.reshape(B, T, D_MODEL), (new_cmp, new_sel, new_win, new_h, new_buf, new_s)


def forward_layer_sample(l, x, w, wb, pw, caches):
    (norm_mix, w_in, w_out, norm_ffn, w_up, w_down, cmp_pos, cmp_w1, cmp_b1, cmp_w2, cmp_b2,
     rg_conv_w, rg_conv_b, rg_wa, rg_ba, rg_wx, rg_bx, rg_lambda, hg_lb, hg_gain) = w
    _, w_out_b, w_up_b, w_down_b = wb
    w_rel, cw, tables_s = pw
    page_table, cache_cmp, cache_sel, cache_win, rg_h0, rg_buf, hg_s0 = caches
    n = x.shape[0]
    x2 = x.reshape(n, D_MODEL)
    qz, cmp_new, sel_new, win_new, _, _, gate, rg, hg = in_proj_prompt(x2, norm_mix, w_rel, tables_s, n)
    o_nsa, win_out = nsa_sample(l, page_table, cache_cmp, cache_sel, cache_win, sel_new, win_new, qz, gate, cw)
    o_rg, new_h, new_buf = rglru_sample(rg, rg_buf, rg_h0, rg_conv_w, rg_conv_b, rg_wa, rg_ba, rg_wx, rg_bx, rg_lambda)
    o_hg, new_s = hgrn_sample(hg, hg_s0, hg_lb, hg_gain)
    mix = jnp.concatenate([o_nsa, o_rg, o_hg], axis=-1)
    x2 = out_ffn(x2, mix, w_out_b, norm_ffn, w_up_b, w_down_b)
    kv_shape = (n, 1, 2, NSA_KV_HEADS, HEAD_DIM)
    new_win = win_out.reshape(n, win_out.shape[1], 2, NSA_KV_HEADS, HEAD_DIM)
    return x2.reshape(n, 1, D_MODEL), (cmp_new.reshape(kv_shape), sel_new.reshape(kv_shape), new_win, new_h, new_buf, new_s)


def kernel(x_prompt, x_sample, cache_nsa_cmp_kv, cache_nsa_sel_kv, cache_nsa_win_kv, state_rglru_h,
           state_rglru_conv, state_hgrn_s, page_table, norm_mix, w_in, w_out, norm_ffn, w_up, w_down,
           cmp_pos, cmp_w1, cmp_b1, cmp_w2, cmp_b2, rg_conv_w, rg_conv_b, rg_wa, rg_ba, rg_wx, rg_bx,
           rg_lambda, hg_lower_bounds, hg_gain, final_norm):
    lb = jnp.cumsum(jax.nn.softmax(hg_lower_bounds.astype(jnp.float32), axis=0), axis=0)
    lb = lb - lb[0]
    past_len = page_table.shape[1] * PAGE_SIZE
    n_dec = x_sample.shape[0]
    pos_p = jnp.arange(x_prompt.shape[1], dtype=jnp.int32)
    w_out_b = w_out.astype(BF16)
    w_up_b = w_up.astype(BF16)
    w_down_b = w_down.astype(BF16)
    w_rel = relayout_w_in(w_in)
    cws = compress_weights(cmp_pos, cmp_w1, cmp_b1, cmp_w2, cmp_b2)
    T = x_prompt.shape[1]
    tables = rope_tables(pos_p)
    c_end_p = jnp.arange(T // CMP_STRIDE, dtype=jnp.int32) * CMP_STRIDE + CMP_BLOCK - 1
    tables_cmp = rope_tables(c_end_p)[:3]
    assert x_sample.shape[1] == 1
    tables_s = rope_tables(jnp.full((n_dec,), past_len, jnp.int32))
    xp, xs = x_prompt, x_sample
    st_p, st_s = [], []
    for l in range(DEPTH):
        w = (norm_mix[l], w_in[l], w_out[l], norm_ffn[l], w_up[l], w_down[l], cmp_pos[l], cmp_w1[l], cmp_b1[l],
             cmp_w2[l], cmp_b2[l], rg_conv_w[l], rg_conv_b[l], rg_wa[l], rg_ba[l], rg_wx[l], rg_bx[l],
             rg_lambda[l], lb[l], hg_gain[l])
        wb = (None, w_out_b[l], w_up_b[l], w_down_b[l])
        pw = (w_rel[l], tuple(a[l] for a in cws), tables, tables_cmp)
        xp, sp = forward_layer_prompt(xp, w, wb, pw)
        caches = (page_table, cache_nsa_cmp_kv, cache_nsa_sel_kv, cache_nsa_win_kv,
                  state_rglru_h[l], state_rglru_conv[l], state_hgrn_s[l])
        xs, ss = forward_layer_sample(l, xs, w, wb, (w_rel[l], pw[1], tables_s), caches)
        st_p.append(sp)
        st_s.append(ss)

    def stack(sts, i):
        return jnp.stack([s[i] for s in sts], axis=0)

    y_prompt = final_rms(xp.reshape(-1, D_MODEL), final_norm).reshape(xp.shape)
    y_sample = final_rms(xs.reshape(-1, D_MODEL), final_norm).reshape(xs.shape)
    return (y_prompt, y_sample, stack(st_p, 0), stack(st_p, 1), stack(st_p, 2), stack(st_p, 3), stack(st_p, 4),
            stack(st_p, 5), stack(st_s, 0), stack(st_s, 1), stack(st_s, 2), stack(st_s, 3), stack(st_s, 4),
            stack(st_s, 5))
```

```python
import functools

import jax
import jax.numpy as jnp
from jax import lax
from jax.experimental import pallas as pl
from jax.experimental.pallas import tpu as pltpu

F32 = jnp.float32
BF16 = jnp.bfloat16

D_MODEL = 1024
DEPTH = 4
PAGE_SIZE = 128
HEAD_DIM = 64
NSA_HEADS = D_MODEL // (2 * HEAD_DIM)
NSA_KV_HEADS = 2
NSA_HPG = NSA_HEADS // NSA_KV_HEADS
NSA_WIDTH = NSA_HEADS * HEAD_DIM
KV_W = NSA_KV_HEADS * HEAD_DIM
CMP_BLOCK = 32
CMP_STRIDE = 16
CMP_RATIO = CMP_BLOCK // CMP_STRIDE
CMP_HIDDEN = 256
SEL_BLOCK = 64
SEL_TOPK = 16
WINDOW = 512
Q_BLOCK = 256
FORCE_SCORE = 1e6
ROPE_DIM = HEAD_DIM // 4
ROPE_THETA = 500000.0
RG_WIDTH = D_MODEL // 4
RG_BLOCKS = 4
RG_BW = RG_WIDTH // RG_BLOCKS
RG_C = 8.0
CONV_W = 4
HG_HEADS = 4
HG_DK = 64
HG_DV = D_MODEL // 4 // HG_HEADS
HG_WIDTH = HG_HEADS * HG_DV
HG_CHUNK = 64
MIX_WIDTH = NSA_WIDTH + RG_WIDTH + HG_WIDTH
D_FF = 4 * D_MODEL
EPS = 1e-6

LANE = 128
VMEM_LIMIT = 56 * 1024 * 1024


def _row_tile(m, want):
    t = min(m, want)
    while m % t:
        t //= 2
    return t


def _rms(x, g):
    return x * lax.rsqrt(jnp.mean(x * x, axis=-1, keepdims=True) + EPS) * g


def _out_ffn_body(x_ref, mix_ref, wo_ref, g_ref, wu_ref, wd_ref, o_ref, *, f_tile):
    x1 = x_ref[...] + jnp.dot(mix_ref[...].astype(BF16), wo_ref[...], preferred_element_type=F32)
    y = _rms(x1, g_ref[...]).astype(BF16)
    acc = x1
    for j in range(wu_ref.shape[1] // f_tile):
        h = jnp.maximum(jnp.dot(y, wu_ref[:, j * f_tile:(j + 1) * f_tile], preferred_element_type=F32), 0.0)
        acc = acc + jnp.dot((h * h).astype(BF16), wd_ref[j * f_tile:(j + 1) * f_tile, :], preferred_element_type=F32)
    o_ref[...] = acc


def out_ffn(x, mix, wo, g, wu, wd, tm=512, f_tile=512):
    m, d = x.shape
    dm = mix.shape[1]
    dff = wu.shape[1]
    tm = _row_tile(m, tm)
    const = lambda i: (0, 0)
    return pl.pallas_call(
        functools.partial(_out_ffn_body, f_tile=f_tile),
        grid=(m // tm,),
        in_specs=[
            pl.BlockSpec((tm, d), lambda i: (i, 0)),
            pl.BlockSpec((tm, dm), lambda i: (i, 0)),
            pl.BlockSpec((dm, d), const),
            pl.BlockSpec((1, d), const),
            pl.BlockSpec((d, dff), const),
            pl.BlockSpec((dff, d), const),
        ],
        out_specs=pl.BlockSpec((tm, d), lambda i: (i, 0)),
        out_shape=jax.ShapeDtypeStruct((m, d), F32),
        compiler_params=pltpu.CompilerParams(
            dimension_semantics=("arbitrary",), vmem_limit_bytes=VMEM_LIMIT),
        name="out_proj_ffn",
    )(x, mix, wo, g.reshape(1, d), wu, wd)


def _final_norm_body(x_ref, g_ref, o_ref):
    o_ref[...] = _rms(x_ref[...], g_ref[...])


def final_rms(x, g, tm=1024):
    m, d = x.shape
    tm = _row_tile(m, tm)
    return pl.pallas_call(
        _final_norm_body,
        grid=(m // tm,),
        in_specs=[pl.BlockSpec((tm, d), lambda i: (i, 0)), pl.BlockSpec((1, d), lambda i: (0, 0))],
        out_specs=pl.BlockSpec((tm, d), lambda i: (i, 0)),
        out_shape=jax.ShapeDtypeStruct((m, d), F32),
        name="final_norm",
    )(x, g.reshape(1, d))


MASK_BIG = 2.0 ** 100
LOG2E = 1.4426950408889634

_O_Q = 0
_O_KV = NSA_WIDTH
_O_GATE = _O_KV + 6 * KV_W
_O_RG = _O_GATE + 3 * NSA_HEADS
_O_HG = _O_RG + 2 * RG_WIDTH

_P_QZ = 0
_P_CMP = _P_QZ + NSA_HEADS * LANE
_P_SEL = _P_CMP + 2 * KV_W
_P_WIN = _P_SEL + 2 * KV_W
_P_KVS = _P_WIN + 2 * KV_W
_P_KVW = _P_KVS + 2 * KV_W
_P_GATE = _P_KVW + 2 * KV_W
_P_RG = _P_GATE + NSA_KV_HEADS * LANE
_P_HG = _P_RG + 2 * RG_WIDTH
_P_END = _P_HG + 4 * HG_WIDTH


def _in_proj_columns():
    import numpy as np
    src = -np.ones((_P_END,), np.int64)
    for hq in range(NSA_HEADS):
        src[_P_QZ + hq * LANE:_P_QZ + hq * LANE + HEAD_DIM] = _O_Q + hq * HEAD_DIM + np.arange(HEAD_DIM)
    src[_P_CMP:_P_CMP + 2 * KV_W] = _O_KV + np.arange(2 * KV_W)
    src[_P_SEL:_P_SEL + 2 * KV_W] = _O_KV + 2 * KV_W + np.arange(2 * KV_W)
    src[_P_WIN:_P_WIN + 2 * KV_W] = _O_KV + 4 * KV_W + np.arange(2 * KV_W)
    for base, off in ((_P_KVS, _O_KV + 2 * KV_W), (_P_KVW, _O_KV + 4 * KV_W)):
        for g in range(NSA_KV_HEADS):
            src[base + g * LANE:base + g * LANE + HEAD_DIM] = off + g * HEAD_DIM + np.arange(HEAD_DIM)
            src[base + g * LANE + HEAD_DIM:base + (g + 1) * LANE] = off + KV_W + g * HEAD_DIM + np.arange(HEAD_DIM)
    ng = 3 * NSA_HPG
    for g in range(NSA_KV_HEADS):
        src[_P_GATE + g * LANE:_P_GATE + g * LANE + ng] = _O_GATE + g * ng + np.arange(ng)
    src[_P_RG:_P_RG + 2 * RG_WIDTH] = _O_RG + np.arange(2 * RG_WIDTH)
    src[_P_HG:_P_HG + 4 * HG_WIDTH] = _O_HG + np.arange(4 * HG_WIDTH)
    return src


def relayout_w_in(w_in):
    import numpy as np
    src = _in_proj_columns()
    cols = jnp.take(w_in, jnp.asarray(np.maximum(src, 0), jnp.int32), axis=-1)
    return jnp.where(jnp.asarray(src >= 0), cols, 0.0).astype(BF16)


def rope_tables(pos):
    import numpy as np
    half = ROPE_DIM // 2
    inv = ROPE_THETA ** (-jnp.arange(half, dtype=F32) * 2.0 / ROPE_DIM)
    ang = pos.astype(F32)[:, None] * inv
    cos, sin = jnp.cos(ang), jnp.sin(ang)
    lane = np.arange(LANE)
    d = lane % HEAD_DIM
    idx = jnp.asarray(d % half, jnp.int32)
    cos_l, sin_l = jnp.take(cos, idx, axis=1), jnp.take(sin, idx, axis=1)
    out = []
    for active in (lane < ROPE_DIM, d < ROPE_DIM):
        first = jnp.asarray(active & (d < half))
        second = jnp.asarray(active & (d >= half) & (d < ROPE_DIM))
        out += [jnp.where(first | second, cos_l, 1.0), jnp.where(first, -sin_l, 0.0), jnp.where(second, sin_l, 0.0)]
    return tuple(out)


def _rope128(x, c, s1, s2):
    return x * c + pltpu.roll(x, LANE - ROPE_DIM // 2, 1) * s1 + pltpu.roll(x, ROPE_DIM // 2, 1) * s2


def _in_proj_prompt_body(x_ref, g_ref, w_ref, cq_ref, s1q_ref, s2q_ref, ckk_ref, s1kk_ref, s2kk_ref,
                         qz_ref, cmp_ref, sel_ref, win_ref, kvs_ref, kvw_ref, gate_ref, rg_ref, hg_ref,
                         *, tiles_per_seq):
    tm = x_ref.shape[0]
    y = _rms(x_ref[...], g_ref[...]).astype(BF16)

    def mm(c0, n):
        return jnp.dot(y, w_ref[:, c0:c0 + n], preferred_element_type=F32)

    cq, s1q, s2q = cq_ref[...], s1q_ref[...], s2q_ref[...]
    ckk, s1kk, s2kk = ckk_ref[...], s1kk_ref[...], s2kk_ref[...]
    scale = HEAD_DIM ** -0.5 * LOG2E
    for hq in range(NSA_HEADS):
        qh = _rope128(mm(_P_QZ + hq * LANE, LANE), cq, s1q, s2q) * scale
        qz_ref[:, hq * LANE:(hq + 1) * LANE] = qh.astype(BF16)
    cmp_ref[...] = mm(_P_CMP, 2 * KV_W)
    for base, ref in ((_P_SEL, sel_ref), (_P_WIN, win_ref)):
        ref[:, 0:LANE] = _rope128(mm(base, LANE), ckk, s1kk, s2kk)
        ref[:, LANE:2 * LANE] = mm(base + LANE, LANE)
    row0 = (pl.program_id(0) % tiles_per_seq) * tm
    blk = (row0 + lax.broadcasted_iota(jnp.int32, (tm, 1), 0)) // SEL_BLOCK
    onehot = jnp.where(lax.broadcasted_iota(jnp.int32, (1, LANE), 1) == blk, MASK_BIG, 0.0).astype(BF16)
    for g in range(NSA_KV_HEADS):
        kvs_ref[g, :, 0:LANE] = _rope128(mm(_P_KVS + g * LANE, LANE), cq, s1q, s2q).astype(BF16)
        kvs_ref[g, :, LANE:2 * LANE] = onehot
        kvw_ref[g] = _rope128(mm(_P_KVW + g * LANE, LANE), cq, s1q, s2q).astype(BF16)
        gate_ref[g] = jax.nn.sigmoid(mm(_P_GATE + g * LANE, LANE))
    rg_ref[...] = mm(_P_RG, 2 * RG_WIDTH)
    hg_ref[...] = mm(_P_HG, 4 * HG_WIDTH)


def in_proj_prompt(x, g, w_rel, tables, seq_len, tm=512):
    m, d = x.shape
    tm = _row_tile(seq_len, tm)
    tiles_per_seq = seq_len // tm
    G = NSA_KV_HEADS
    row = lambda i: (i, 0)
    grow = lambda i: (0, i, 0)
    tab = pl.BlockSpec((tm, LANE), lambda i: (i % tiles_per_seq, 0))
    return pl.pallas_call(
        functools.partial(_in_proj_prompt_body, tiles_per_seq=tiles_per_seq),
        grid=(m // tm,),
        in_specs=[pl.BlockSpec((tm, d), row), pl.BlockSpec((1, d), lambda i: (0, 0)),
                  pl.BlockSpec((d, _P_END), lambda i: (0, 0))] + [tab] * 6,
        out_specs=[
            pl.BlockSpec((tm, NSA_HEADS * LANE), row),
            pl.BlockSpec((tm, 2 * KV_W), row), pl.BlockSpec((tm, 2 * KV_W), row), pl.BlockSpec((tm, 2 * KV_W), row),
            pl.BlockSpec((G, tm, 2 * LANE), grow), pl.BlockSpec((G, tm, LANE), grow), pl.BlockSpec((G, tm, LANE), grow),
            pl.BlockSpec((tm, 2 * RG_WIDTH), row), pl.BlockSpec((tm, 4 * HG_WIDTH), row),
        ],
        out_shape=[
            jax.ShapeDtypeStruct((m, NSA_HEADS * LANE), BF16),
            jax.ShapeDtypeStruct((m, 2 * KV_W), F32), jax.ShapeDtypeStruct((m, 2 * KV_W), F32),
            jax.ShapeDtypeStruct((m, 2 * KV_W), F32),
            jax.ShapeDtypeStruct((G, m, 2 * LANE), BF16), jax.ShapeDtypeStruct((G, m, LANE), BF16),
            jax.ShapeDtypeStruct((G, m, LANE), F32),
            jax.ShapeDtypeStruct((m, 2 * RG_WIDTH), F32), jax.ShapeDtypeStruct((m, 4 * HG_WIDTH), F32),
        ],
        compiler_params=pltpu.CompilerParams(
            dimension_semantics=("arbitrary",), vmem_limit_bytes=VMEM_LIMIT),
        name="in_proj_prompt",
    )(x, g.reshape(1, d), w_rel, *tables)


def compress_weights(cmp_pos, cmp_w1, cmp_b1, cmp_w2, cmp_b2):
    L = cmp_w1.shape[0]
    G, S, HD, CH = NSA_KV_HEADS, CMP_STRIDE, HEAD_DIM, CMP_HIDDEN
    w1 = cmp_w1.reshape(L, 2, CMP_RATIO, S, HD, CH)
    z = jnp.zeros_like(w1)
    w1bd = jnp.concatenate([jnp.concatenate([w1, z], axis=-1), jnp.concatenate([z, w1], axis=-1)], axis=-2)
    pe = jnp.tile(cmp_pos.reshape(L, 2, CMP_RATIO, S, HD), (1, 1, 1, 1, G))
    b1 = jnp.tile(cmp_b1, (1, 1, G)).reshape(L, 2, 1, G * CH)
    zc = jnp.zeros((L, CH, HD), F32)
    w2 = jnp.stack([jnp.concatenate([cmp_w2[:, 0], zc], axis=-1), jnp.concatenate([zc, cmp_w2[:, 1]], axis=-1)], axis=1)
    b2 = jnp.concatenate([cmp_b2[:, 0], cmp_b2[:, 1]], axis=-1).reshape(L, 1, 2 * HD)
    return w1bd.astype(BF16), pe, b1, w2.astype(BF16), b2


def _compress_body(xk_ref, xv_ref, pe_ref, w1_ref, b1_ref, w2_ref, b2_ref, c_ref, s1_ref, s2_ref, o_ref, *, nch, n_cmp):
    gelus = []
    for kv, x_ref in enumerate((xk_ref, xv_ref)):
        acc = [jnp.zeros((nch, NSA_KV_HEADS * CMP_HIDDEN), F32) for _ in range(CMP_RATIO)]
        for s in range(CMP_STRIDE):
            xs = x_ref[pl.ds(s, nch, stride=CMP_STRIDE), :]
            for r in range(CMP_RATIO):
                acc[r] = acc[r] + jnp.dot((xs + pe_ref[kv, r, s:s + 1, :]).astype(BF16), w1_ref[kv, r, s],
                                          preferred_element_type=F32)
        h = b1_ref[kv] + acc[0] + pltpu.roll(acc[1], nch - 1, 0)
        gelus.append(jax.nn.gelu(h).astype(BF16))
    valid = lax.broadcasted_iota(jnp.int32, (nch, 1), 0) < n_cmp
    for g in range(NSA_KV_HEADS):
        cols = slice(g * CMP_HIDDEN, (g + 1) * CMP_HIDDEN)
        out = (jnp.dot(gelus[0][:, cols], w2_ref[0], preferred_element_type=F32)
               + jnp.dot(gelus[1][:, cols], w2_ref[1], preferred_element_type=F32) + b2_ref[...])
        out = _rope128(out, c_ref[...], s1_ref[...], s2_ref[...])
        o_ref[g] = jnp.where(valid, out, 0.0).astype(BF16)


def compress(raw, cw, tables_q, seq_len):
    w1bd, pe, b1, w2, b2 = cw
    m = raw.shape[0]
    nb = m // seq_len
    nch = seq_len // CMP_STRIDE
    n_cmp = nch - CMP_RATIO + 1
    full = lambda shape: pl.BlockSpec(shape, lambda b: (0,) * len(shape))
    return pl.pallas_call(
        functools.partial(_compress_body, nch=nch, n_cmp=n_cmp),
        grid=(nb,),
        in_specs=[pl.BlockSpec((seq_len, KV_W), lambda b: (b, 0)), pl.BlockSpec((seq_len, KV_W), lambda b: (b, 1)),
                  full(pe.shape), full(w1bd.shape), full(b1.shape), full(w2.shape), full(b2.shape),
                  full((nch, LANE)), full((nch, LANE)), full((nch, LANE))],
        out_specs=pl.BlockSpec((None, NSA_KV_HEADS, nch, LANE), lambda b: (b, 0, 0, 0)),
        out_shape=jax.ShapeDtypeStruct((nb, NSA_KV_HEADS, nch, LANE), BF16),
        compiler_params=pltpu.CompilerParams(
            dimension_semantics=("arbitrary",), vmem_limit_bytes=VMEM_LIMIT),
        name="compress_kv",
    )(raw, raw, pe, w1bd, b1, w2, b2, *tables_q)


_NT = (((1,), (1,)), ((), ()))


def _masked_softmax2(s, mask):
    s = jnp.where(mask, s, -1e30)
    m = jnp.max(s, axis=-1, keepdims=True)
    e = jnp.where(mask, jnp.exp2(s - m), 0.0)
    return e / jnp.maximum(jnp.sum(e, axis=-1, keepdims=True), 1e-30)


N_FORCED = 3


def _select_blocks(imp, valid, forced, block_idx):
    score = jnp.where(valid & jnp.logical_not(forced), imp, -FORCE_SCORE)

    def pick_one(_, carry):
        score, chosen = carry
        hit = block_idx == jnp.argmax(score, axis=1, keepdims=True)
        return jnp.where(hit, -jnp.inf, score), jnp.where(hit, 1.0, chosen)

    chosen = jnp.broadcast_to(jnp.where(forced, 1.0, 0.0), imp.shape)
    _, chosen = lax.fori_loop(0, SEL_TOPK - N_FORCED, pick_one, (score, chosen), unroll=True)
    return jnp.where(valid, chosen, 0.0)


def _nsa_prompt_body(qz_ref, gate_ref, kvc_ref, smap_ref, kvs_ref, kvw_ref, o_ref, *, tk):
    QB, H = Q_BLOCK, NSA_HPG
    q0 = pl.program_id(2) * QB
    q4 = jnp.concatenate([qz_ref[:, h * LANE:(h + 1) * LANE] for h in range(H)], axis=0)
    t_col = q0 + (lax.broadcasted_iota(jnp.int32, (H * QB, 1), 0) & (QB - 1))

    kvc = kvc_ref[...]
    ncp = kvc.shape[0]
    s_c = lax.dot_general(q4, kvc, _NT, preferred_element_type=F32)
    c_end = lax.broadcasted_iota(jnp.int32, (1, ncp), 1) * CMP_STRIDE + (CMP_BLOCK - 1)
    p_c = _masked_softmax2(s_c, c_end <= t_col)
    o_c = jnp.dot(p_c.astype(BF16), kvc, preferred_element_type=F32)

    psum = p_c[0:QB]
    for h in range(1, H):
        psum = psum + p_c[h * QB:(h + 1) * QB]
    hi = psum.astype(BF16)
    lo = (psum - hi.astype(F32)).astype(BF16)
    smap = smap_ref[...]
    imp = jnp.dot(hi, smap, preferred_element_type=F32) + jnp.dot(lo, smap, preferred_element_type=F32)

    jf = lax.broadcasted_iota(jnp.int32, (QB, LANE), 1)
    t_tok = q0 + lax.broadcasted_iota(jnp.int32, (QB, 1), 0)
    cur = t_tok // SEL_BLOCK
    valid = jf * SEL_BLOCK <= t_tok
    forced = (jf == 0) | (jf == cur) | (jf == cur - 1)
    selm1 = (_select_blocks(imp, valid, forced, jf) - 1.0).astype(BF16)
    q_aug = jnp.concatenate([q4, jnp.concatenate([selm1] * H, axis=0)], axis=1)

    def tile_step(kt, carry):
        m, l, acc = carry
        hk = tk // 2
        start = pl.multiple_of(kt * tk, tk)
        tiles = [kvs_ref[pl.ds(start + j * hk, hk), :] for j in range(2)]
        ss = [lax.dot_general(q_aug, tl, _NT, preferred_element_type=F32) for tl in tiles]
        for j in range(2):
            kpos = kt * tk + j * hk + lax.broadcasted_iota(jnp.int32, (1, hk), 1)
            ss[j] = jnp.where(kpos <= t_col, ss[j], -1e30)
        m_new = jnp.maximum(m, jnp.maximum(jnp.max(ss[0], axis=1, keepdims=True),
                                           jnp.max(ss[1], axis=1, keepdims=True)))
        alpha = jnp.exp2(m - m_new)
        ps = [jnp.exp2(sj - m_new) for sj in ss]
        l = alpha * l + jnp.sum(ps[0], axis=1, keepdims=True) + jnp.sum(ps[1], axis=1, keepdims=True)
        pv = (jnp.dot(ps[0].astype(BF16), tiles[0][:, 0:LANE], preferred_element_type=F32)
              + jnp.dot(ps[1].astype(BF16), tiles[1][:, 0:LANE], preferred_element_type=F32))
        return m_new, l, alpha * acc + pv

    init = (jnp.full((H * QB, 1), -1e30, F32), jnp.zeros((H * QB, 1), F32), jnp.zeros((H * QB, LANE), F32))
    UNROLL = 4

    def multi_step(i, c):
        for u in range(UNROLL):
            c = tile_step(UNROLL * i + u, c)
        return c

    n_tiles = q0 // tk + 1
    n_multi = n_tiles // UNROLL
    carry = lax.fori_loop(0, n_multi, multi_step, init)
    _, l_s, acc_s = lax.fori_loop(UNROLL * n_multi, n_tiles, tile_step, carry)
    o_s = acc_s / l_s

    wlen = WINDOW + QB
    w0 = pl.multiple_of(jnp.maximum(q0 - WINDOW, 0), QB)
    wt = kvw_ref[pl.ds(w0, wlen), :]
    s_w = lax.dot_general(q4, wt, _NT, preferred_element_type=F32)
    dist = t_col - (w0 + lax.broadcasted_iota(jnp.int32, (1, wlen), 1))
    p_w = _masked_softmax2(s_w, (dist >= 0) & (dist < WINDOW))
    o_w = jnp.dot(p_w.astype(BF16), wt, preferred_element_type=F32)

    comb = []
    for h in range(H):
        r = slice(h * QB, (h + 1) * QB)
        comb.append(gate_ref[:, 3 * h:3 * h + 1] * o_c[r] + gate_ref[:, 3 * h + 1:3 * h + 2] * o_s[r]
                    + gate_ref[:, 3 * h + 2:3 * h + 3] * o_w[r])
    low = lax.broadcasted_iota(jnp.int32, (1, LANE), 1) < HEAD_DIM
    for p in range(H // 2):
        o_ref[:, p * LANE:(p + 1) * LANE] = jnp.where(low, pltpu.roll(comb[2 * p], HEAD_DIM, 1), comb[2 * p + 1])


def nsa_prompt(qz, gate, kvc, kvs, kvw, nb, seq_len, tk=512):
    m = qz.shape[0]
    G, QB = NSA_KV_HEADS, Q_BLOCK
    nqb = seq_len // QB
    ncp = kvc.shape[2]
    assert seq_len % tk == 0 and seq_len >= WINDOW + QB and seq_len // SEL_BLOCK <= LANE
    import numpy as np
    c0 = np.arange(ncp)[:, None] * CMP_STRIDE
    s0 = np.arange(LANE)[None, :] * SEL_BLOCK
    n_cmp = ncp - CMP_RATIO + 1
    ov = (c0 < s0 + SEL_BLOCK) & (c0 + CMP_BLOCK > s0) & (np.arange(ncp)[:, None] < n_cmp)
    smap = jnp.asarray(ov, BF16)
    return pl.pallas_call(
        functools.partial(_nsa_prompt_body, tk=tk),
        grid=(nb, G, nqb),
        in_specs=[
            pl.BlockSpec((QB, NSA_HPG * LANE), lambda b, g, q: (b * nqb + q, g)),
            pl.BlockSpec((None, QB, LANE), lambda b, g, q: (g, b * nqb + q, 0)),
            pl.BlockSpec((None, None, ncp, LANE), lambda b, g, q: (b, g, 0, 0)),
            pl.BlockSpec((ncp, LANE), lambda b, g, q: (0, 0)),
            pl.BlockSpec((None, seq_len, 2 * LANE), lambda b, g, q: (g, b, 0)),
            pl.BlockSpec((None, seq_len, LANE), lambda b, g, q: (g, b, 0)),
        ],
        out_specs=pl.BlockSpec((QB, NSA_HPG * HEAD_DIM), lambda b, g, q: (b * nqb + q, g)),
        out_shape=jax.ShapeDtypeStruct((m, NSA_WIDTH), F32),
        compiler_params=pltpu.CompilerParams(
            dimension_semantics=("arbitrary", "arbitrary", "arbitrary"), vmem_limit_bytes=VMEM_LIMIT),
        name="nsa_prompt",
    )(qz, gate, kvc, smap, kvs, kvw)


def _shift_rows(x, prev8, d):
    rolled = pltpu.roll(x, d, 0)
    row8 = lax.broadcasted_iota(jnp.int32, (8, 1), 0)
    first = jnp.where(row8 < d, pltpu.roll(prev8, d, 0), rolled[0:8])
    return jnp.concatenate([first, rolled[8:]], axis=0)


def _rglru_body(rg_ref, buf_ref, h0_ref, cw_ref, cb_ref, wg_ref, bg_ref, lam_ref,
                o_ref, hout_ref, bufout_ref, tail_ref, h_ref):
    tm = rg_ref.shape[0]
    W = RG_WIDTH

    @pl.when(pl.program_id(1) == 0)
    def _():
        tail_ref[...] = buf_ref[...]
        h_ref[...] = h0_ref[...]

    x = rg_ref[:, 0:W]
    gr = rg_ref[:, W:2 * W]
    prev8 = tail_ref[...]
    xc = cb_ref[...] + cw_ref[CONV_W - 1:CONV_W, :] * x
    for d in range(1, CONV_W):
        xc = xc + cw_ref[CONV_W - 1 - d:CONV_W - d, :] * _shift_rows(x, prev8, d)
    gates = jnp.dot(xc.astype(BF16), wg_ref[...], preferred_element_type=F32) + bg_ref[...]
    r = jax.nn.sigmoid(gates[:, 0:W])
    i = jax.nn.sigmoid(gates[:, W:2 * W])
    log_a = -RG_C * r * jax.nn.softplus(-lam_ref[...])
    a = jnp.exp(log_a)
    th = jnp.tanh(log_a)
    b = jnp.sqrt(-2.0 * th / (1.0 - th)) * (i * xc)
    row = lax.broadcasted_iota(jnp.int32, (tm, 1), 0)
    d = 1
    while d < tm:
        keep = row >= d
        a_sh = jnp.where(keep, pltpu.roll(a, d, 0), 1.0)
        b_sh = jnp.where(keep, pltpu.roll(b, d, 0), 0.0)
        b = a * b_sh + b
        a = a * a_sh
        d *= 2
    h = a * h_ref[7:8, :] + b
    o_ref[...] = h * jax.nn.gelu(gr)
    h_ref[...] = h[tm - 8:tm]
    tail_ref[...] = x[tm - 8:tm]
    hout_ref[...] = h[tm - 8:tm]
    bufout_ref[...] = x[tm - 8:tm]


def rglru_prompt(rg, conv_buf, h0, conv_w, conv_b, wa, ba, wx, bx, lam, seq_len, tm=256):
    m = rg.shape[0]
    nb = m // seq_len
    W = RG_WIDTH
    tm = _row_tile(seq_len, tm)
    nt = seq_len // tm
    bd = jax.scipy.linalg.block_diag
    wg = jnp.concatenate([bd(*[wa[n] for n in range(RG_BLOCKS)]), bd(*[wx[n] for n in range(RG_BLOCKS)])], axis=1).astype(BF16)
    bg = jnp.concatenate([ba, bx]).reshape(1, 2 * W)
    buf8 = jnp.pad(conv_buf, ((0, 0), (8 - (CONV_W - 1), 0), (0, 0)))
    h08 = jnp.broadcast_to(h0[:, None, :], (nb, 8, W))
    const = lambda shape: pl.BlockSpec(shape, lambda b, t: (0,) * len(shape))
    per_b = pl.BlockSpec((None, 8, W), lambda b, t: (b, 0, 0))
    out, h8, nbuf8 = pl.pallas_call(
        _rglru_body,
        grid=(nb, nt),
        in_specs=[pl.BlockSpec((tm, 2 * W), lambda b, t: (b * nt + t, 0)), per_b, per_b,
                  const((CONV_W, W)), const((1, W)), const((W, 2 * W)), const((1, 2 * W)), const((1, W))],
        out_specs=[pl.BlockSpec((tm, W), lambda b, t: (b * nt + t, 0)), per_b, per_b],
        out_shape=[jax.ShapeDtypeStruct((m, W), F32), jax.ShapeDtypeStruct((nb, 8, W), F32),
                   jax.ShapeDtypeStruct((nb, 8, W), F32)],
        scratch_shapes=[pltpu.VMEM((8, W), F32), pltpu.VMEM((8, W), F32)],
        compiler_params=pltpu.CompilerParams(dimension_semantics=("arbitrary", "arbitrary")),
        name="rglru_prompt",
    )(rg, buf8, h08, conv_w, conv_b.reshape(1, W), wg, bg, lam.reshape(1, W))
    return out, h8[:, 7], nbuf8[:, 8 - (CONV_W - 1):]


def _split_dot(x, w_bf16, pieces):
    acc = None
    for _ in range(pieces):
        xb = x.astype(BF16)
        part = jnp.dot(xb, w_bf16, preferred_element_type=F32)
        acc = part if acc is None else acc + part
        x = x - xb.astype(F32)
    return acc


HG_SAFE_DECAY = 75.0


def _hgrn_body(hg_ref, s0_ref, lb_ref, gain_ref, tri_ref, tribd_ref, ones_ref, o_ref, sout_ref, st_ref, *, chunk):
    tm = hg_ref.shape[0]
    W, H = HG_WIDTH, HG_HEADS
    C = chunk
    nck = tm // C

    @pl.when(pl.program_id(1) == 0)
    def _():
        st_ref[...] = s0_ref[...]

    lb = lb_ref[...]
    ones_blk = ones_ref[...]
    same_head = ones_blk > 0

    def finish(o, rows):
        ms = _split_dot(o * o, ones_blk, 2) * (1.0 / HG_DV)
        o = o * lax.rsqrt(ms + EPS) * gain_ref[...]
        o_ref[rows, :] = o * jax.nn.silu(hg_ref[rows, 3 * W:4 * W])

    q = jax.nn.silu(hg_ref[:, 0:W])
    f = lb + (1.0 - lb) * jax.nn.sigmoid(hg_ref[:, W:2 * W])
    v = hg_ref[:, 2 * W:3 * W]
    k = 1.0 - f
    bcum = _split_dot_left(tribd_ref[...], jnp.log(f))
    b_last = [bcum[(c + 1) * C - 1:(c + 1) * C, :] for c in range(nck)]
    total = b_last[0]
    for c in range(1, nck):
        total = jnp.minimum(total, b_last[c])
    safe = jnp.min(total) >= -HG_SAFE_DECAY

    @pl.when(safe)
    def _():
        qt = q * jnp.exp(bcum)
        kt = (k * jnp.exp(-bcum)).astype(BF16)
        vb = v.astype(BF16)
        lane_head = lax.broadcasted_iota(jnp.int32, (1, W), 1) // HG_DV
        q_exp = jnp.concatenate([jnp.where(lane_head == h, qt, 0.0) for h in range(H)], axis=0).astype(BF16)
        a = lax.dot_general(q_exp, kt, _NT, preferred_element_type=F32)
        t_row = lax.broadcasted_iota(jnp.int32, (H * tm, 1), 0) & (tm - 1)
        s_col = lax.broadcasted_iota(jnp.int32, (1, tm), 1)
        pair = (s_col <= t_row) & ((s_col // C) == (t_row // C))
        o_exp = jnp.dot(jnp.where(pair, a, 0.0).astype(BF16), vb, preferred_element_type=F32)
        o = jnp.where(lane_head == 0, o_exp[0:tm], 0.0)
        for h in range(1, H):
            o = o + jnp.where(lane_head == h, o_exp[h * tm:(h + 1) * tm], 0.0)
        st = st_ref[...]
        inter = []
        for c in range(nck):
            r = slice(c * C, (c + 1) * C)
            inter.append(lax.dot_general(qt[r].astype(BF16), st.astype(BF16), _NT, preferred_element_type=F32))
            k_hat = (k[r] * jnp.exp(b_last[c] - bcum[r])).astype(BF16)
            upd = lax.dot_general(vb[r], k_hat, (((0,), (0,)), ((), ())), preferred_element_type=F32)
            st = jnp.exp(b_last[c]) * st + jnp.where(same_head, upd, 0.0)
        st_ref[...] = st
        finish(o + jnp.concatenate(inter, axis=0), slice(None))

    @pl.when(jnp.logical_not(safe))
    def _():
        tri = tri_ref[...]
        rowc = lax.broadcasted_iota(jnp.int32, (C, 1), 0)

        def chunk_step(c, _):
            r0 = pl.multiple_of(c * C, C)
            rows = pl.ds(r0, C)
            qc = jax.nn.silu(hg_ref[rows, 0:W])
            fc = lb + (1.0 - lb) * jax.nn.sigmoid(hg_ref[rows, W:2 * W])
            vc = hg_ref[rows, 2 * W:3 * W]
            kc = 1.0 - fc
            bc = _split_dot_left(tri, jnp.log(fc))
            st = st_ref[...]
            o = lax.dot_general((qc * jnp.exp(bc)).astype(BF16), st.astype(BF16), _NT, preferred_element_type=F32)

            def offset_step(dlt, carry):
                o, b_sh, k_sh, v_sh = carry
                p = jnp.where(rowc >= dlt, qc * k_sh * jnp.exp(bc - b_sh), 0.0)
                o = o + _split_dot(p, ones_blk, 2) * v_sh
                return o, pltpu.roll(b_sh, 1, 0), pltpu.roll(k_sh, 1, 0), pltpu.roll(v_sh, 1, 0)

            o, _, _, _ = lax.fori_loop(0, C, offset_step, (o, bc, kc, vc))
            bl = bc[C - 1:C, :]
            k_hat = (kc * jnp.exp(bl - bc)).astype(BF16)
            upd = lax.dot_general(vc.astype(BF16), k_hat, (((0,), (0,)), ((), ())), preferred_element_type=F32)
            st_ref[...] = jnp.exp(bl) * st + jnp.where(same_head, upd, 0.0)
            finish(o, rows)
            return 0

        lax.fori_loop(0, nck, chunk_step, 0)

    sout_ref[...] = st_ref[...]


def _split_dot_left(w_bf16, x):
    acc = None
    for _ in range(3):
        xb = x.astype(BF16)
        part = jnp.dot(w_bf16, xb, preferred_element_type=F32)
        acc = part if acc is None else acc + part
        x = x - xb.astype(F32)
    return acc


def hgrn_prompt(hg, s0, lb, gain, seq_len, tm=256, chunk=HG_CHUNK):
    import numpy as np
    m = hg.shape[0]
    nb = m // seq_len
    W, H = HG_WIDTH, HG_HEADS
    tm = _row_tile(seq_len, tm)
    nt = seq_len // tm
    head = np.arange(W) // HG_DV
    ones_blk = jnp.asarray(head[:, None] == head[None, :], BF16)
    tri = jnp.asarray(np.tril(np.ones((chunk, chunk))), BF16)
    rt = np.arange(tm)
    tri_bd = jnp.asarray((rt[None, :] <= rt[:, None]) & (rt[None, :] // chunk == rt[:, None] // chunk), BF16)
    s0_t = jnp.einsum('bhkv,hg->bhvgk', s0, jnp.eye(H, dtype=s0.dtype)).reshape(nb, W, W)
    const = lambda shape: pl.BlockSpec(shape, lambda b, t: (0,) * len(shape))
    per_b = pl.BlockSpec((None, W, W), lambda b, t: (b, 0, 0))
    out, s_t = pl.pallas_call(
        functools.partial(_hgrn_body, chunk=chunk),
        grid=(nb, nt),
        in_specs=[pl.BlockSpec((tm, 4 * W), lambda b, t: (b * nt + t, 0)), per_b,
                  const((1, W)), const((1, W)), const((chunk, chunk)), const((tm, tm)), const((W, W))],
        out_specs=[pl.BlockSpec((tm, W), lambda b, t: (b * nt + t, 0)), per_b],
        out_shape=[jax.ShapeDtypeStruct((m, W), F32), jax.ShapeDtypeStruct((nb, W, W), F32)],
        scratch_shapes=[pltpu.VMEM((W, W), F32)],
        compiler_params=pltpu.CompilerParams(dimension_semantics=("arbitrary", "arbitrary")),
        name="hgrn_prompt",
    )(hg, s0_t, lb.reshape(1, W), gain.reshape(1, W), tri, tri_bd, ones_blk)
    s5 = s_t.reshape(nb, H, HG_DV, H, HG_DK)
    s_fin = jnp.stack([s5[:, h, :, h, :] for h in range(H)], axis=1).transpose(0, 1, 3, 2)
    return out, s_fin


def _nsa_sample_body(pt_ref, *refs, n_pages, past_len, n_win, spb):
    del pt_ref
    np2 = 2 * n_pages
    cmp_all, sel_all = refs[0:spb * np2], refs[spb * np2:2 * spb * np2]
    (win_ref, selnew_ref, winnew_ref, qlo_ref, qq_ref, gate_ref, pe_ref, w1_ref, b1_ref, w2_ref, b2_ref,
     c_ref, s1_ref, s2_ref, smap_ref, expand_ref, wcol_ref, o_ref, winout_ref, raw_ref) = refs[2 * spb * np2:]
    R = NSA_KV_HEADS * NSA_HPG
    parts = [_nsa_sample_cmp(cmp_all[j * np2:(j + 1) * np2], qlo_ref.at[j], pe_ref, w1_ref, b1_ref, w2_ref, b2_ref,
                             c_ref, s1_ref, s2_ref, smap_ref, raw_ref.at[j], n_pages=n_pages, past_len=past_len)
             for j in range(spb)]
    lane = lax.broadcasted_iota(jnp.int32, (1, LANE), 1)
    cur = past_len // SEL_BLOCK
    valid = lane * SEL_BLOCK <= past_len
    forced = (lane == 0) | (lane == cur) | (lane == cur - 1)
    imp = jnp.concatenate([pt_[1] for pt_ in parts], axis=0)
    selm1 = (_select_blocks(imp, valid, forced, jnp.broadcast_to(lane, (spb * R, LANE))) - 1.0).astype(BF16)
    bias = jnp.dot(selm1, expand_ref[...], preferred_element_type=F32)
    each = lambda ref: [ref.at[j] for j in range(spb)]
    _nsa_sample_attend([sel_all[j * np2:(j + 1) * np2] for j in range(spb)], each(win_ref), each(selnew_ref),
                       each(winnew_ref), each(qq_ref), each(gate_ref), each(wcol_ref), each(o_ref), each(winout_ref),
                       jnp.concatenate([pt_[0] for pt_ in parts], axis=0), bias,
                       n_pages=n_pages, past_len=past_len, n_win=n_win)


def _nsa_sample_cmp(cmp_refs, qlo_ref, pe_ref, w1_ref, b1_ref, w2_ref, b2_ref, c_ref, s1_ref, s2_ref, smap_ref,
                    raw_ref, *, n_pages, past_len):
    G, H, CH = NSA_KV_HEADS, NSA_HPG, CMP_HIDDEN
    R = G * H
    nch = n_pages * (PAGE_SIZE // CMP_STRIDE)
    n_cmp = nch - CMP_RATIO + 1
    _TT = (((1,), (1,)), ((), ()))
    for p in range(n_pages):
        for kv in range(2):
            raw_ref[kv, p * PAGE_SIZE:(p + 1) * PAGE_SIZE, :] = cmp_refs[2 * p + kv][...].T
    t = past_len
    row = lax.broadcasted_iota(jnp.int32, (R, 1), 0)
    grp0 = row < H
    lane = lax.broadcasted_iota(jnp.int32, (1, LANE), 1)

    gelus = []
    for kv in range(2):
        acc = [jnp.zeros((nch, G * CH), F32) for _ in range(CMP_RATIO)]
        for sp in range(CMP_STRIDE // 2):
            xs = jnp.concatenate([raw_ref[kv, pl.ds(s, nch, stride=CMP_STRIDE), :] for s in (2 * sp, 2 * sp + 1)], axis=1)
            for r in range(CMP_RATIO):
                acc[r] = acc[r] + jnp.dot((xs + pe_ref[kv, r, sp:sp + 1, :]).astype(BF16), w1_ref[kv, r, sp],
                                          preferred_element_type=F32)
        h = b1_ref[kv] + acc[0] + pltpu.roll(acc[1], nch - 1, 0)
        gelus.append(jax.nn.gelu(h).astype(BF16))
    valid_c = lax.broadcasted_iota(jnp.int32, (nch, 1), 0) < n_cmp
    qlo = qlo_ref[...].astype(BF16)
    c_end = lax.broadcasted_iota(jnp.int32, (1, nch), 1) * CMP_STRIDE + (CMP_BLOCK - 1)
    smap = smap_ref[...]
    o_c, imp = [], []
    for g in range(G):
        cols = slice(g * CH, (g + 1) * CH)
        kvc = (jnp.dot(gelus[0][:, cols], w2_ref[0], preferred_element_type=F32)
               + jnp.dot(gelus[1][:, cols], w2_ref[1], preferred_element_type=F32) + b2_ref[...])
        kvc = jnp.where(valid_c, _rope128(kvc, c_ref[...], s1_ref[...], s2_ref[...]), 0.0).astype(BF16)
        s_c = lax.dot_general(qlo, kvc, _NT, preferred_element_type=F32)
        p_c = _masked_softmax2(s_c, c_end <= t)
        o_c.append(jnp.dot(p_c.astype(BF16), kvc, preferred_element_type=F32))
        mine = grp0 if g == 0 else jnp.logical_not(grp0)
        psum = jnp.sum(jnp.where(mine, p_c, 0.0), axis=0, keepdims=True)
        imp.append(_split_dot(jnp.broadcast_to(psum, (R, nch)), smap, 2))
    o_c = jnp.where(grp0, pltpu.roll(o_c[0], HEAD_DIM, 1), o_c[1])
    return o_c, jnp.where(grp0, imp[0], imp[1])


def _nsa_sample_attend(sel_refs, win_refs, selnew_refs, winnew_refs, qq_refs, gate_refs, wcol_refs, o_refs,
                       winout_refs, o_c, bias, *, n_pages, past_len, n_win):
    t = past_len
    S = len(qq_refs)
    R = NSA_KV_HEADS * NSA_HPG
    _TT = (((1,), (1,)), ((), ()))
    stack = lambda xs: jnp.concatenate(xs, axis=0)
    rows = lambda x, j: x[j * R:(j + 1) * R]
    qq32 = [r[...] for r in qq_refs]
    qq = [q.astype(BF16) for q in qq32]

    s_s = stack([jnp.concatenate([jnp.dot(qq[j], sel_refs[j][2 * p][...].astype(BF16), preferred_element_type=F32)
                                  for p in range(n_pages)], axis=1) for j in range(S)]) + bias
    s_new = stack([jnp.sum(qq32[j] * selnew_refs[j][:, 0:LANE], axis=1, keepdims=True) for j in range(S)])
    m = jnp.maximum(jnp.max(s_s, axis=1, keepdims=True), s_new)
    p = jnp.exp2(s_s - m)
    p_new = jnp.exp2(s_new - m)
    denom = jnp.sum(p, axis=1, keepdims=True) + p_new
    pb = p.astype(BF16)
    o_s = []
    for j in range(S):
        acc = rows(p_new, j) * selnew_refs[j][:, LANE:2 * LANE]
        for pg in range(n_pages):
            acc = acc + lax.dot_general(rows(pb, j)[:, pg * PAGE_SIZE:(pg + 1) * PAGE_SIZE],
                                        sel_refs[j][2 * pg + 1][...].astype(BF16), _TT, preferred_element_type=F32)
        o_s.append(acc)
    o_s = stack(o_s) / denom

    wins = [r[...] for r in win_refs]
    col = lax.broadcasted_iota(jnp.int32, (1, n_win), 1)
    w_ok = (n_win - col < WINDOW) & (t - n_win + col >= 0)
    s_w = stack([jnp.dot(qq[j], wins[j][0:LANE, :].astype(BF16), preferred_element_type=F32) for j in range(S)])
    s_w = jnp.where(w_ok, s_w, -1e30)
    s_wn = stack([jnp.sum(qq32[j] * winnew_refs[j][:, 0:LANE], axis=1, keepdims=True) for j in range(S)])
    m = jnp.maximum(jnp.max(s_w, axis=1, keepdims=True), s_wn)
    p = jnp.where(w_ok, jnp.exp2(s_w - m), 0.0)
    p_new = jnp.exp2(s_wn - m)
    denom = jnp.sum(p, axis=1, keepdims=True) + p_new
    pb = p.astype(BF16)
    o_w = stack([lax.dot_general(rows(pb, j), wins[j][LANE:2 * LANE, :].astype(BF16), _TT, preferred_element_type=F32)
                 + rows(p_new, j) * winnew_refs[j][:, LANE:2 * LANE] for j in range(S)]) / denom

    gate = stack([r[...] for r in gate_refs])
    o = gate[:, 0:1] * o_c + gate[:, 1:2] * o_s + gate[:, 2:3] * o_w
    for j in range(S):
        o_refs[j][...] = rows(o, j)
        winout_refs[j][...] = jnp.where(col == n_win - 1, wcol_refs[j][...], pltpu.roll(wins[j], n_win - 1, 1))


def nsa_sample(l, page_table, cache_cmp, cache_sel, cache_win, sel_new, win_new, qz, gate, cw):
    import numpy as np
    n, n_pages = page_table.shape
    past_len = n_pages * PAGE_SIZE
    n_win = cache_win.shape[2]
    depth, n_pool = cache_cmp.shape[:2]
    G, H = NSA_KV_HEADS, NSA_HPG
    R = G * H
    nch = past_len // CMP_STRIDE
    assert past_len // SEL_BLOCK + 1 <= LANE and n_win % 8 == 0
    w1bd, pe, b1, w2, b2 = cw
    w1p = w1bd.reshape(2, CMP_RATIO, CMP_STRIDE // 2, 2 * LANE, G * CMP_HIDDEN)
    pep = pe.reshape(2, CMP_RATIO, CMP_STRIDE // 2, 2 * LANE)
    c_end = jnp.arange(nch, dtype=jnp.int32) * CMP_STRIDE + CMP_BLOCK - 1
    tabs = rope_tables(c_end)[:3]
    c0 = np.arange(nch)[:, None] * CMP_STRIDE
    s0 = np.arange(LANE)[None, :] * SEL_BLOCK
    ov = (c0 < s0 + SEL_BLOCK) & (c0 + CMP_BLOCK > s0) & (np.arange(nch)[:, None] < nch - CMP_RATIO + 1)
    smap = jnp.asarray(ov, BF16)
    expand = jnp.asarray(np.where(np.arange(past_len)[None, :] // SEL_BLOCK == np.arange(LANE)[:, None], MASK_BIG, 0.0), BF16)
    qlo = qz.reshape(n, R, LANE).astype(F32)
    qq = jnp.concatenate([qlo[:, :H], jnp.roll(qlo[:, H:], HEAD_DIM, axis=-1)], axis=1)
    g3 = jnp.pad(gate[:, :, :3 * H].reshape(G, n, H, 3).transpose(1, 0, 2, 3).reshape(n, R, 3), ((0, 0), (0, 0), (0, LANE - 3)))
    cmp4 = cache_cmp.reshape(depth, n_pool, PAGE_SIZE, 2 * KV_W).transpose(0, 1, 3, 2)
    sel4 = cache_sel.reshape(depth, n_pool, PAGE_SIZE, 2 * KV_W).transpose(0, 1, 3, 2)
    win4 = cache_win.reshape(depth, n, n_win, 2 * KV_W).transpose(0, 1, 3, 2)

    spb = 2 if n % 2 == 0 else 1

    def page_spec(j, p, half):
        return pl.BlockSpec((None, None, LANE, PAGE_SIZE), lambda b, pt: (l, pt[b * spb + j, p], half, 0))

    page_specs = [page_spec(j, p, half) for j in range(spb) for p in range(n_pages) for half in range(2)]
    full = lambda shape: pl.BlockSpec(shape, lambda b, pt: (0,) * len(shape))
    per_seq = lambda shape: pl.BlockSpec((spb,) + shape, lambda b, pt: (b,) + (0,) * len(shape))
    grid_spec = pltpu.PrefetchScalarGridSpec(
        num_scalar_prefetch=1,
        grid=(n // spb,),
        in_specs=page_specs + page_specs + [
            pl.BlockSpec((None, spb, 2 * KV_W, n_win), lambda b, pt: (l, b, 0, 0)),
            per_seq((1, 2 * KV_W)), per_seq((1, 2 * KV_W)), per_seq((R, LANE)), per_seq((R, LANE)), per_seq((R, LANE)),
            full(pep.shape), full(w1p.shape), full(b1.shape), full(w2.shape), full(b2.shape),
            full((nch, LANE)), full((nch, LANE)), full((nch, LANE)), full((nch, LANE)), full((LANE, past_len)),
            per_seq((2 * KV_W, 1))],
        out_specs=[per_seq((R, LANE)), per_seq((2 * KV_W, n_win))],
        scratch_shapes=[pltpu.VMEM((spb, 2, past_len, LANE), F32)],
    )
    o8, win_out = pl.pallas_call(
        functools.partial(_nsa_sample_body, n_pages=n_pages, past_len=past_len, n_win=n_win, spb=spb),
        grid_spec=grid_spec,
        out_shape=[jax.ShapeDtypeStruct((n, R, LANE), F32), jax.ShapeDtypeStruct((n, 2 * KV_W, n_win), F32)],
        compiler_params=pltpu.CompilerParams(dimension_semantics=("arbitrary",), vmem_limit_bytes=VMEM_LIMIT),
        name="nsa_sample",
    )(page_table, *([cmp4] * (2 * n_pages * spb)), *([sel4] * (2 * n_pages * spb)), win4,
      sel_new.reshape(n, 1, 2 * KV_W), win_new.reshape(n, 1, 2 * KV_W), qlo, qq, g3,
      pep, w1p, b1, w2, b2, *tabs, smap, expand, win_new.reshape(n, 2 * KV_W, 1))
    o5 = o8.reshape(n, G, H, G, HEAD_DIM)
    o = jnp.stack([o5[:, g, :, g, :] for g in range(G)], axis=1).reshape(n, NSA_WIDTH)
    return o, win_out.transpose(0, 2, 1)


def _rglru_sample_body(rg_ref, b0_ref, b1_ref, b2_ref, h0_ref, cw_ref, cb_ref, wg_ref, bg_ref, lam_ref, o_ref, h_ref):
    W = RG_WIDTH
    x = rg_ref[:, 0:W]
    xc = (cb_ref[...] + cw_ref[0:1, :] * b0_ref[...] + cw_ref[1:2, :] * b1_ref[...] + cw_ref[2:3, :] * b2_ref[...]
          + cw_ref[3:4, :] * x)
    gates = jnp.dot(xc.astype(BF16), wg_ref[...], preferred_element_type=F32) + bg_ref[...]
    r = jax.nn.sigmoid(gates[:, 0:W])
    i = jax.nn.sigmoid(gates[:, W:2 * W])
    log_a = -RG_C * r * jax.nn.softplus(-lam_ref[...])
    th = jnp.tanh(log_a)
    h = jnp.exp(log_a) * h0_ref[...] + jnp.sqrt(-2.0 * th / (1.0 - th)) * (i * xc)
    h_ref[...] = h
    o_ref[...] = h * jax.nn.gelu(rg_ref[:, W:2 * W])


def rglru_sample(rg, conv_buf, h0, conv_w, conv_b, wa, ba, wx, bx, lam):
    assert CONV_W == 4
    n = rg.shape[0]
    W = RG_WIDTH
    bd = jax.scipy.linalg.block_diag
    wg = jnp.concatenate([bd(*[wa[k] for k in range(RG_BLOCKS)]), bd(*[wx[k] for k in range(RG_BLOCKS)])], axis=1).astype(BF16)
    bg = jnp.concatenate([ba, bx]).reshape(1, 2 * W)
    out, h = pl.pallas_call(
        _rglru_sample_body,
        out_shape=[jax.ShapeDtypeStruct((n, W), F32), jax.ShapeDtypeStruct((n, W), F32)],
        name="rglru_sample",
    )(rg, conv_buf[:, 0], conv_buf[:, 1], conv_buf[:, 2], h0, conv_w, conv_b.reshape(1, W), wg, bg, lam.reshape(1, W))
    new_buf = jnp.concatenate([conv_buf[:, 1:], rg[:, None, 0:W]], axis=1)
    return out, h, new_buf


def _hgrn_sample_body(q_ref, f_ref, lb_ref, v_ref, g_ref, gain_ref, s0_ref, o_ref, s_ref):
    DK = HG_DK
    lb = lb_ref[...]
    for j in range(q_ref.shape[0]):
        q = jax.nn.silu(q_ref[j])
        f = lb + (1.0 - lb) * jax.nn.sigmoid(f_ref[j])
        k = 1.0 - f
        outs = []
        for h in range(HG_HEADS):
            r = slice(h * DK, (h + 1) * DK)
            s_new = f[r] * s0_ref[j, h] + k[r] * v_ref[j, h:h + 1, :]
            s_ref[j, h] = s_new
            outs.append(jnp.sum(q[r] * s_new, axis=0, keepdims=True))
        o = jnp.concatenate(outs, axis=0)
        o = o * lax.rsqrt(jnp.mean(o * o, axis=-1, keepdims=True) + EPS) * gain_ref[...]
        o_ref[j] = o * jax.nn.silu(g_ref[j])


def hgrn_sample(hg, s0, lb, gain):
    n = hg.shape[0]
    H, DK, DV, W = HG_HEADS, HG_DK, HG_DV, HG_WIDTH
    col = lambda a: a.reshape(n, W, 1)
    spb = _row_tile(n, 8)
    per_seq = lambda shape: pl.BlockSpec((spb,) + shape, lambda b: (b,) + (0,) * len(shape))
    full = lambda shape: pl.BlockSpec(shape, lambda b: (0,) * len(shape))
    o, s_new = pl.pallas_call(
        _hgrn_sample_body,
        grid=(n // spb,),
        in_specs=[per_seq((W, 1)), per_seq((W, 1)), full((W, 1)), per_seq((H, DV)), per_seq((H, DV)), full((H, DV)),
                  per_seq((H, DK, DV))],
        out_specs=[per_seq((H, DV)), per_seq((H, DK, DV))],
        out_shape=[jax.ShapeDtypeStruct((n, H, DV), F32), jax.ShapeDtypeStruct((n, H, DK, DV), F32)],
        compiler_params=pltpu.CompilerParams(dimension_semantics=("arbitrary",)),
        name="hgrn_sample",
    )(col(hg[:, 0:W]), col(hg[:, W:2 * W]), lb.reshape(W, 1), hg[:, 2 * W:3 * W].reshape(n, H, DV),
      hg[:, 3 * W:4 * W].reshape(n, H, DV), gain.reshape(H, DV), s0)
    return o.reshape(n, W), s_new


def forward_layer_prompt(x, w, wb, pw):
    (norm_mix, w_in, w_out, norm_ffn, w_up, w_down, cmp_pos, cmp_w1, cmp_b1, cmp_w2, cmp_b2,
     rg_conv_w, rg_conv_b, rg_wa, rg_ba, rg_wx, rg_bx, rg_lambda, hg_lb, hg_gain) = w
    _, w_out_b, w_up_b, w_down_b = wb
    w_rel, cw, tables, tables_cmp = pw
    B, T = x.shape[:2]
    x2 = x.reshape(B * T, D_MODEL)
    qz, cmp, sel, win, kvs, kvw, gate, rg, hg = in_proj_prompt(x2, norm_mix, w_rel, tables, T)
    kvc = compress(cmp, cw, tables_cmp, T)
    o_nsa = nsa_prompt(qz, gate, kvc, kvs, kvw, B, T)
    kv_shape = (B, T, 2, NSA_KV_HEADS, HEAD_DIM)
    new_cmp, new_sel = cmp.reshape(kv_shape), sel.reshape(kv_shape)
    new_win = win.reshape(kv_shape)[:, T - min(WINDOW, T):]
    rg_buf = jnp.zeros((B, CONV_W - 1, RG_WIDTH), x.dtype)
    rg_h0 = jnp.zeros((B, RG_WIDTH), x.dtype)
    hg_s0 = jnp.zeros((B, HG_HEADS, HG_DK, HG_DV), x.dtype)
    o_rg, new_h, new_buf = rglru_prompt(rg, rg_buf, rg_h0, rg_conv_w, rg_conv_b, rg_wa, rg_ba, rg_wx, rg_bx,
                                        rg_lambda, T)
    o_hg, new_s = hgrn_prompt(hg, hg_s0, hg_lb, hg_gain, T)
    mix = jnp.concatenate([o_nsa, o_rg, o_hg], axis=-1)
    x2 = out_ffn(x2, mix, w_out_b, norm_ffn, w_up_b, w_down_b)
    return x2.reshape(B, T, D_MODEL), (new_cmp, new_sel, new_win, new_h, new_buf, new_s)


def forward_layer_sample(l, x, w, wb, pw, caches):
    (norm_mix, w_in, w_out, norm_ffn, w_up, w_down, cmp_pos, cmp_w1, cmp_b1, cmp_w2, cmp_b2,
     rg_conv_w, rg_conv_b, rg_wa, rg_ba, rg_wx, rg_bx, rg_lambda, hg_lb, hg_gain) = w
    _, w_out_b, w_up_b, w_down_b = wb
    w_rel, cw, tables_s = pw
    page_table, cache_cmp, cache_sel, cache_win, rg_h0, rg_buf, hg_s0 = caches
    n = x.shape[0]
    x2 = x.reshape(n, D_MODEL)
    qz, cmp_new, sel_new, win_new, _, _, gate, rg, hg = in_proj_prompt(x2, norm_mix, w_rel, tables_s, n)
    o_nsa, win_out = nsa_sample(l, page_table, cache_cmp, cache_sel, cache_win, sel_new, win_new, qz, gate, cw)
    o_rg, new_h, new_buf = rglru_sample(rg, rg_buf, rg_h0, rg_conv_w, rg_conv_b, rg_wa, rg_ba, rg_wx, rg_bx, rg_lambda)
    o_hg, new_s = hgrn_sample(hg, hg_s0, hg_lb, hg_gain)
    mix = jnp.concatenate([o_nsa, o_rg, o_hg], axis=-1)
    x2 = out_ffn(x2, mix, w_out_b, norm_ffn, w_up_b, w_down_b)
    kv_shape = (n, 1, 2, NSA_KV_HEADS, HEAD_DIM)
    new_win = win_out.reshape(n, win_out.shape[1], 2, NSA_KV_HEADS, HEAD_DIM)
    return x2.reshape(n, 1, D_MODEL), (cmp_new.reshape(kv_shape), sel_new.reshape(kv_shape), new_win, new_h, new_buf, new_s)


def kernel(x_prompt, x_sample, cache_nsa_cmp_kv, cache_nsa_sel_kv, cache_nsa_win_kv, state_rglru_h,
           state_rglru_conv, state_hgrn_s, page_table, norm_mix, w_in, w_out, norm_ffn, w_up, w_down,
           cmp_pos, cmp_w1, cmp_b1, cmp_w2, cmp_b2, rg_conv_w, rg_conv_b, rg_wa, rg_ba, rg_wx, rg_bx,
           rg_lambda, hg_lower_bounds, hg_gain, final_norm):
    lb = jnp.cumsum(jax.nn.softmax(hg_lower_bounds.astype(jnp.float32), axis=0), axis=0)
    lb = lb - lb[0]
    past_len = page_table.shape[1] * PAGE_SIZE
    n_dec = x_sample.shape[0]
    pos_p = jnp.arange(x_prompt.shape[1], dtype=jnp.int32)
    w_out_b = w_out.astype(BF16)
    w_up_b = w_up.astype(BF16)
    w_down_b = w_down.astype(BF16)
    w_rel = relayout_w_in(w_in)
    cws = compress_weights(cmp_pos, cmp_w1, cmp_b1, cmp_w2, cmp_b2)
    T = x_prompt.shape[1]
    tables = rope_tables(pos_p)
    c_end_p = jnp.arange(T // CMP_STRIDE, dtype=jnp.int32) * CMP_STRIDE + CMP_BLOCK - 1
    tables_cmp = rope_tables(c_end_p)[:3]
    assert x_sample.shape[1] == 1
    tables_s = rope_tables(jnp.full((n_dec,), past_len, jnp.int32))
    xp, xs = x_prompt, x_sample
    st_p, st_s = [], []
    for l in range(DEPTH):
        w = (norm_mix[l], w_in[l], w_out[l], norm_ffn[l], w_up[l], w_down[l], cmp_pos[l], cmp_w1[l], cmp_b1[l],
             cmp_w2[l], cmp_b2[l], rg_conv_w[l], rg_conv_b[l], rg_wa[l], rg_ba[l], rg_wx[l], rg_bx[l],
             rg_lambda[l], lb[l], hg_gain[l])
        wb = (None, w_out_b[l], w_up_b[l], w_down_b[l])
        pw = (w_rel[l], tuple(a[l] for a in cws), tables, tables_cmp)
        xp, sp = forward_layer_prompt(xp, w, wb, pw)
        caches = (page_table, cache_nsa_cmp_kv, cache_nsa_sel_kv, cache_nsa_win_kv,
                  state_rglru_h[l], state_rglru_conv[l], state_hgrn_s[l])
        xs, ss = forward_layer_sample(l, xs, w, wb, (w_rel[l], pw[1], tables_s), caches)
        st_p.append(sp)
        st_s.append(ss)

    def stack(sts, i):
        return jnp.stack([s[i] for s in sts], axis=0)

    y_prompt = final_rms(xp.reshape(-1, D_MODEL), final_norm).reshape(xp.shape)
    y_sample = final_rms(xs.reshape(-1, D_MODEL), final_norm).reshape(xs.shape)
    return (y_prompt, y_sample, stack(st_p, 0), stack(st_p, 1), stack(st_p, 2), stack(st_p, 3), stack(st_p, 4),
            stack(st_p, 5), stack(st_s, 0), stack(st_s, 1), stack(st_s, 2), stack(st_s, 3), stack(st_s, 4),
            stack(st_s, 5))
```
